```python
import jax
import jax.numpy as jnp
from jax import lax
import numpy as np

D_MODEL = 1024
BATCH = 8
SEQ = 4096
DEPTH = 4

CTX_LEN = 256
GRID_W = 64
EPS = 1e-6
MIX_W = D_MODEL // 2
N_BRANCH = 3
CHUNK = 64

A_HEADS = 4
A_DV = MIX_W // A_HEADS
A_DQK = A_DV // 2
A_CONV = 3

B_DH = 64
B_HEADS = MIX_W // B_DH
B_KV = 2
B_GROUP = B_HEADS // B_KV
Q_BLOCK = 128
ROPE_THETA = 10000.0

C_HEADS = 4
C_DV = MIX_W // C_HEADS
C_DK = C_DV // 2
C_RANK = 16
C_TAU = 16.0

N_EXPERTS = 32
TOP_K = 4
D_FF = D_MODEL
SWIGLU_LIMIT = 7.0
SWIGLU_ALPHA = 1.702

_SPLITS = (
    ("a_qk", 2 * A_HEADS * A_DQK), ("a_v", MIX_W), ("a_i", 2 * A_HEADS), ("a_f", 2 * A_HEADS), ("a_o", MIX_W),
    ("b_q", B_HEADS * B_DH), ("b_k", B_KV * B_DH), ("b_v", B_KV * B_DH),
    ("c_q", C_HEADS * C_DK), ("c_k", C_HEADS * C_DK), ("c_v", MIX_W), ("c_a", 2 * C_RANK), ("c_r", MIX_W),
    ("gate", N_BRANCH * D_MODEL),
)
IN_W = sum(n for _, n in _SPLITS)

kernel_name = "hybrid_flow_mlstm_gqa_gla_moe_trunk"


def split_cols(y):
    offs = np.cumsum([n for _, n in _SPLITS])[:-1]
    return dict(zip([k for k, _ in _SPLITS], jnp.split(y, offs, axis=-1)))


def rms_norm(x, g):
    xf = x.astype(jnp.float32)
    y = xf * lax.rsqrt(jnp.mean(xf * xf, axis=-1, keepdims=True) + EPS)
    return (y * g.astype(jnp.float32)).astype(x.dtype)


def modulate(h, shift, scale):
    return h * (1 + scale) + shift


def heads(y, n):
    return y.reshape(*y.shape[:-1], n, -1).swapaxes(1, 2)


def merge_heads(h):
    h = h.swapaxes(1, 2)
    return h.reshape(*h.shape[:2], -1)


def head_rms(y, g, n):
    shp = y.shape
    return rms_norm(y.reshape(*shp[:-1], n, -1), g.reshape(n, -1)).reshape(shp)


def dir_heads(y, n):
    return y.reshape(*y.shape[:-1], 2, n).transpose(2, 0, 3, 1)


def centred_dwconv(x, w, b):
    y = lax.conv_general_dilated(x, w[:, None, :].astype(x.dtype), window_strides=(1,), padding="SAME",
                                 dimension_numbers=("NWC", "WIO", "NWC"), feature_group_count=x.shape[-1])
    return y + b


def rope_axis(x, cos, sin):
    x1, x2 = jnp.split(x, 2, axis=-1)
    return jnp.concatenate([x1 * cos - x2 * sin, x2 * cos + x1 * sin], axis=-1)


def rope_2d(x, cos, sin):
    n = cos.shape[-1] // 2
    cos, sin = cos.astype(x.dtype), sin.astype(x.dtype)
    xr, xc = jnp.split(x, 2, axis=-1)
    return jnp.concatenate([rope_axis(xr, cos[:, :n], sin[:, :n]), rope_axis(xc, cos[:, n:], sin[:, n:])], axis=-1)


def to_chunks(t):
    b, h, L = t.shape[:3]
    return jnp.moveaxis(t.reshape(b, h, L // CHUNK, CHUNK, *t.shape[3:]), 2, 0)


def from_chunks(t):
    n, b, h, c = t.shape[:4]
    return jnp.moveaxis(t, 0, 2).reshape(b, h, n * c, *t.shape[4:])


def mlstm_scan(q, k, v, ig, lf, state):
    mask = jnp.tril(jnp.ones((CHUNK, CHUNK), bool))

    def step(carry, inp):
        C, n, m = carry
        qc, kc, vc, ic, fc = inp
        b = jnp.cumsum(fc, axis=-1)
        log_d = jnp.where(mask, b[..., :, None] - b[..., None, :] + ic[..., None, :], -jnp.inf)
        log_inter = b + m[..., None]
        m_j = jnp.maximum(log_inter, jnp.max(log_d, axis=-1))
        s = jnp.einsum("bhjd,bhsd->bhjs", qc, kc) * jnp.exp(log_d - m_j[..., None])
        w_inter = jnp.exp(log_inter - m_j)
        num = jnp.einsum("bhjs,bhsv->bhjv", s, vc) + w_inter[..., None] * jnp.einsum("bhjd,bhdv->bhjv", qc, C)
        den = jnp.sum(s, axis=-1) + w_inter * jnp.einsum("bhjd,bhd->bhj", qc, n)
        h = num / jnp.maximum(jnp.abs(den), jnp.exp(-m_j))[..., None]
        m_new = m_j[..., -1]
        w_s = jnp.exp(b[..., -1:] - b + ic - m_new[..., None])
        decay = jnp.exp(b[..., -1] + m - m_new)
        C = decay[..., None, None] * C + jnp.einsum("bhs,bhsd,bhsv->bhdv", w_s, kc, vc)
        n = decay[..., None] * n + jnp.einsum("bhs,bhsd->bhd", w_s, kc)
        return (C, n, m_new), h

    state, h = lax.scan(step, state, tuple(to_chunks(t) for t in (q, k, v, ig, lf)))
    return from_chunks(h), state


def gla_scan(q, k, v, la, S):
    mask = jnp.tril(jnp.ones((CHUNK, CHUNK), bool))[..., None]

    def step(S, inp):
        qc, kc, vc, lc = inp
        b = jnp.cumsum(lc, axis=2)
        rel = jnp.where(mask, b[:, :, :, None, :] - b[:, :, None, :, :], -jnp.inf)
        a = jnp.einsum("bhjd,bhsd,bhjsd->bhjs", qc, kc, jnp.exp(rel))
        o = jnp.einsum("bhjs,bhsv->bhjv", a, vc) + jnp.einsum("bhjd,bhdv->bhjv", qc * jnp.exp(b), S)
        k_dec = kc * jnp.exp(b[:, :, -1:, :] - b)
        S = jnp.exp(b[:, :, -1, :])[..., None] * S + jnp.einsum("bhsd,bhsv->bhdv", k_dec, vc)
        return S, o

    S, o = lax.scan(step, S, tuple(to_chunks(t) for t in (q, k, v, la)))
    return from_chunks(o), S


def bidir_scan(scan_fn, ctx_in, lat_in, state0):
    out_c, out_l = 0, 0
    for d, rev in enumerate((False, True)):
        flip = (lambda t: jnp.flip(t, axis=2)) if rev else (lambda t: t)
        h_c, st = scan_fn(*[flip(t) for t in ctx_in[d]], state0)
        h_l, _ = scan_fn(*[flip(t) for t in lat_in[d]], st)
        out_c = out_c + flip(h_c)
        out_l = out_l + flip(h_l)
    return out_c, out_l


def mlstm_prep(p, conv_w, conv_b, gate_b):
    qk = jax.nn.silu(centred_dwconv(p["a_qk"], conv_w, conv_b))
    q, k = jnp.split(qk, 2, axis=-1)
    q = heads(q, A_HEADS).astype(jnp.float32)
    k = heads(k, A_HEADS).astype(jnp.float32) * (A_DQK ** -0.5)
    v = heads(p["a_v"], A_HEADS).astype(jnp.float32)
    i_pre = dir_heads(p["a_i"].astype(jnp.float32) + gate_b[0], A_HEADS)
    f_log = dir_heads(jax.nn.log_sigmoid(p["a_f"].astype(jnp.float32) + gate_b[1]), A_HEADS)
    return tuple((q, k, v, i_pre[d], f_log[d]) for d in range(2))


def gla_prep(p, w_up, b_up):
    q = heads(p["c_q"], C_HEADS).astype(jnp.float32) * (C_DK ** -0.5)
    k = heads(p["c_k"], C_HEADS).astype(jnp.float32)
    v = heads(p["c_v"], C_HEADS).astype(jnp.float32)
    low = jnp.split(p["c_a"], 2, axis=-1)
    out = []
    for d in range(2):
        la = jax.nn.log_sigmoid((low[d] @ w_up[d] + b_up[d]).astype(jnp.float32)) / C_TAU
        out.append((q, k, v, heads(la, C_HEADS)))
    return tuple(out)


def gqa_prep(p, q_g, k_g):
    q = rms_norm(heads(p["b_q"], B_HEADS), q_g)
    k = rms_norm(heads(p["b_k"], B_KV), k_g)
    v = heads(p["b_v"], B_KV)
    return q, k, v


def block_attend(q, k, v):
    b, h, lq, dh = q.shape
    nb = lq // Q_BLOCK
    qb = jnp.moveaxis(q.reshape(b, B_KV, B_GROUP, nb, Q_BLOCK, dh), 3, 0)

    def one(qi):
        s = jnp.einsum("bkgqd,bkld->bkgql", qi, k).astype(jnp.float32) * (dh ** -0.5)
        pr = jax.nn.softmax(s, axis=-1).astype(v.dtype)
        return jnp.einsum("bkgql,bkld->bkgqd", pr, v)

    o = lax.map(one, qb)
    return merge_heads(jnp.moveaxis(o, 0, 3).reshape(b, h, lq, dh))


def mlstm_out(h, p, g):
    y = merge_heads(h).astype(p["a_o"].dtype)
    return head_rms(y, g, A_HEADS) * jax.nn.sigmoid(p["a_o"])


def gla_out(h, p, g):
    y = merge_heads(h).astype(p["c_r"].dtype)
    return head_rms(y, g, C_HEADS) * jax.nn.silu(p["c_r"])


def merge_branches(ys, gate, w_branch, w_out):
    g = jax.nn.sigmoid(gate.astype(jnp.float32)).astype(gate.dtype)
    g = g.reshape(*gate.shape[:-1], N_BRANCH, D_MODEL)
    merged = sum(g[..., n, :] * (y @ w_branch[n]) for n, y in enumerate(ys))
    return merged @ w_out


def mixer_sublayer(hl, hc, w_in, a_conv_w, a_conv_b, a_gate_b, a_norm_g, b_q_norm_g, b_k_norm_g,
                   c_w_up, c_b_up, c_norm_g, w_branch, w_out, cos, sin, with_ctx):
    pl = split_cols(hl @ w_in)
    pc = split_cols(hc @ w_in)
    bsz = hl.shape[0]
    zeros = lambda *s: jnp.zeros((bsz,) + s, jnp.float32)

    ha_c, ha_l = bidir_scan(mlstm_scan, mlstm_prep(pc, a_conv_w, a_conv_b, a_gate_b),
                            mlstm_prep(pl, a_conv_w, a_conv_b, a_gate_b),
                            (zeros(A_HEADS, A_DQK, A_DV), zeros(A_HEADS, A_DQK), zeros(A_HEADS)))
    q_l, k_l, v_l = gqa_prep(pl, b_q_norm_g, b_k_norm_g)
    q_l, k_l = rope_2d(q_l, cos, sin), rope_2d(k_l, cos, sin)
    q_c, k_c, v_c = gqa_prep(pc, b_q_norm_g, b_k_norm_g)
    att_l = block_attend(q_l, jnp.concatenate([k_c, k_l], axis=2), jnp.concatenate([v_c, v_l], axis=2))
    hg_c, hg_l = bidir_scan(gla_scan, gla_prep(pc, c_w_up, c_b_up), gla_prep(pl, c_w_up, c_b_up),
                            zeros(C_HEADS, C_DK, C_DV))

    yl = merge_branches((mlstm_out(ha_l, pl, a_norm_g), att_l, gla_out(hg_l, pl, c_norm_g)),
                        pl["gate"], w_branch, w_out)
    if not with_ctx:
        return yl, None
    att_c = block_attend(q_c, k_c, v_c)
    yc = merge_branches((mlstm_out(ha_c, pc, a_norm_g), att_c, gla_out(hg_c, pc, c_norm_g)),
                        pc["gate"], w_branch, w_out)
    return yl, yc


def moe(t, w_router, b_router, w1, b1, w2, b2):
    logits = (t @ w_router + b_router).astype(jnp.float32)
    top_v, top_i = lax.top_k(logits, TOP_K)
    w = jax.nn.softmax(top_v, axis=-1)
    dense_w = jnp.sum(jax.nn.one_hot(top_i, N_EXPERTS, dtype=jnp.float32) * w[..., None], axis=1).astype(t.dtype)
    out = jnp.zeros_like(t)
    for e in range(N_EXPERTS):
        gu = t @ w1[e] + b1[e]
        gate = jnp.minimum(gu[:, 0::2], SWIGLU_LIMIT)
        up = jnp.clip(gu[:, 1::2], -SWIGLU_LIMIT, SWIGLU_LIMIT)
        a = (up + 1) * gate * jax.nn.sigmoid(SWIGLU_ALPHA * gate)
        out = out + dense_w[:, e:e + 1] * (a @ w2[e] + b2[e])
    return out


def setup_inputs(seed: int = 0) -> dict:
    key = jax.random.key(seed)
    ks = iter(jax.random.split(key, 32))
    nrm = lambda shape, scale: jax.random.normal(next(ks), shape, jnp.float32) * scale
    near_one = lambda shape: 1.0 + nrm(shape, 0.02)
    gate_base = jnp.array([0.0, 3.0], jnp.float32)[None, :, None]
    return {
        "x": nrm((BATCH, SEQ, D_MODEL), 1.0),
        "c": nrm((BATCH, D_MODEL), 1.0),
        "ctx": nrm((BATCH, CTX_LEN, D_MODEL), 1.0),
        "c_ctx": nrm((D_MODEL,), 1.0),
        "w_ada": nrm((DEPTH, D_MODEL, 6 * D_MODEL), 0.5 * D_MODEL ** -0.5),
        "b_ada": nrm((DEPTH, 6 * D_MODEL), 0.01),
        "g_norm1": near_one((DEPTH, D_MODEL)),
        "w_in": nrm((DEPTH, D_MODEL, IN_W), D_MODEL ** -0.5),
        "a_conv_w": nrm((DEPTH, A_CONV, 2 * A_HEADS * A_DQK), A_CONV ** -0.5),
        "a_conv_b": nrm((DEPTH, 2 * A_HEADS * A_DQK), 0.01),
        "a_gate_b": gate_base + nrm((DEPTH, 2, 2 * A_HEADS), 0.1),
        "a_norm_g": near_one((DEPTH, MIX_W)),
        "b_q_norm_g": near_one((DEPTH, B_DH)),
        "b_k_norm_g": near_one((DEPTH, B_DH)),
        "c_w_up": nrm((DEPTH, 2, C_RANK, C_HEADS * C_DK), C_RANK ** -0.5),
        "c_b_up": nrm((DEPTH, 2, C_HEADS * C_DK), 0.1),
        "c_norm_g": near_one((DEPTH, MIX_W)),
        "w_branch": nrm((DEPTH, N_BRANCH, MIX_W, D_MODEL), MIX_W ** -0.5),
        "w_out": nrm((DEPTH, D_MODEL, D_MODEL), D_MODEL ** -0.5),
        "g_norm2": near_one((DEPTH, D_MODEL)),
        "w_router": nrm((DEPTH, D_MODEL, N_EXPERTS), D_MODEL ** -0.5),
        "b_router": nrm((DEPTH, N_EXPERTS), 0.01),
        "w_e1": nrm((DEPTH, N_EXPERTS, D_MODEL, 2 * D_FF), D_MODEL ** -0.5),
        "b_e1": nrm((DEPTH, N_EXPERTS, 2 * D_FF), 0.01),
        "w_e2": nrm((DEPTH, N_EXPERTS, D_FF, D_MODEL), D_FF ** -0.5),
        "b_e2": nrm((DEPTH, N_EXPERTS, D_MODEL), 0.01),
        "g_final": near_one((D_MODEL,)),
    }


def reference(x, c, ctx, c_ctx, w_ada, b_ada, g_norm1, w_in, a_conv_w, a_conv_b, a_gate_b, a_norm_g,
              b_q_norm_g, b_k_norm_g, c_w_up, c_b_up, c_norm_g, w_branch, w_out, g_norm2,
              w_router, b_router, w_e1, b_e1, w_e2, b_e2, g_final):
    seq = x.shape[1]
    rows = seq // GRID_W
    row = jnp.repeat(jnp.arange(rows), GRID_W)
    col = jnp.tile(jnp.arange(GRID_W), rows)
    n_f = B_DH // 4
    freqs = ROPE_THETA ** (-jnp.arange(n_f, dtype=jnp.float32) / n_f)
    ang = jnp.concatenate([row[:, None] * freqs, col[:, None] * freqs], axis=-1)
    cos, sin = jnp.cos(ang), jnp.sin(ang)
    s_lat = jax.nn.silu(c)
    s_ctx = jax.nn.silu(c_ctx)
    for l in range(DEPTH):
        last = l == DEPTH - 1
        sh1, sc1, g1, sh2, sc2, g2 = [m[:, None, :] for m in jnp.split(s_lat @ w_ada[l] + b_ada[l], 6, axis=-1)]
        csh1, csc1, cg1, csh2, csc2, cg2 = jnp.split(s_ctx @ w_ada[l] + b_ada[l], 6, axis=-1)
        hl = modulate(rms_norm(x, g_norm1[l]), sh1, sc1)
        hc = modulate(rms_norm(ctx, g_norm1[l]), csh1, csc1)
        yl, yc = mixer_sublayer(hl, hc, w_in[l], a_conv_w[l], a_conv_b[l], a_gate_b[l], a_norm_g[l],
                                b_q_norm_g[l], b_k_norm_g[l], c_w_up[l], c_b_up[l], c_norm_g[l],
                                w_branch[l], w_out[l], cos, sin, not last)
        x = x + g1 * yl
        hl = modulate(rms_norm(x, g_norm2[l]), sh2, sc2)
        n_lat = hl.shape[0] * hl.shape[1]
        if last:
            f = moe(hl.reshape(-1, D_MODEL), w_router[l], b_router[l], w_e1[l], b_e1[l], w_e2[l], b_e2[l])
            x = x + g2 * f.reshape(hl.shape)
        else:
            ctx = ctx + cg1 * yc
            hc = modulate(rms_norm(ctx, g_norm2[l]), csh2, csc2)
            tok = jnp.concatenate([hl.reshape(-1, D_MODEL), hc.reshape(-1, D_MODEL)], axis=0)
            f = moe(tok, w_router[l], b_router[l], w_e1[l], b_e1[l], w_e2[l], b_e2[l])
            x = x + g2 * f[:n_lat].reshape(hl.shape)
            ctx = ctx + cg2 * f[n_lat:].reshape(hc.shape)
    return rms_norm(x, g_final)
```

```python
import functools

import jax
import jax.numpy as jnp
import numpy as np
from jax import lax
from jax.experimental import pallas as pl
from jax.experimental.pallas import tpu as pltpu
from jax.experimental.pallas import tpu_sc as plsc

f32 = jnp.float32
bf16 = jnp.bfloat16
HI = lax.Precision.HIGHEST
SDS = jax.ShapeDtypeStruct

EPS = 1e-6
CHUNK = 64
GRID_W = 64
ROPE_THETA = 10000.0
MIX_W = 512
A_HEADS, A_DQK, A_DV = 4, 64, 128
B_HEADS, B_KV, B_DH = 8, 2, 64
B_GROUP = B_HEADS // B_KV
C_HEADS, C_DK, C_DV, C_RANK, C_TAU = 4, 64, 128, 16, 16.0
N_EXPERTS, TOP_K = 32, 4
SWIGLU_LIMIT, SWIGLU_ALPHA = 7.0, 1.702

LANES = 128
SC_NUM_CORES = 2
SC_NUM_SUBCORES = 16
VMEM_LIMIT = 56 * 2 ** 20

TM = 256
TQ = 128
TME = 512
SUB = 16
SC_WINDOW = 32


def _cp(sem, vmem=VMEM_LIMIT):
    return pltpu.CompilerParams(dimension_semantics=sem, vmem_limit_bytes=vmem)


def _dot(a, b, precision=None):
    return jnp.dot(a, b, preferred_element_type=f32, precision=precision)


def _dot_nt(a, b, precision=None):
    return lax.dot_general(a, b, (((1,), (1,)), ((), ())), preferred_element_type=f32, precision=precision)


def _dot_tn(a, b):
    return lax.dot_general(a, b, (((0,), (0,)), ((), ())), preferred_element_type=f32)


def _sigmoid(x):
    return 1.0 / (1.0 + jnp.exp(-x))


def _log_sigmoid(x):
    return jnp.minimum(x, 0.0) - jnp.log(1.0 + jnp.exp(-jnp.abs(x)))


def _iota(shape, axis):
    return lax.broadcasted_iota(jnp.int32, shape, axis)


def _rms_mod(x, g, shift, scale):
    y = x * lax.rsqrt(jnp.mean(x * x, axis=-1, keepdims=True) + EPS) * g
    return y * (1.0 + scale) + shift


def _adaln_kernel(cc_ref, w_ref, b_ref, o_ref):
    cc = cc_ref[...]
    s = cc * _sigmoid(cc)
    o_ref[0] = _dot(s, w_ref[0], HI) + b_ref[0]


def _adaln(cc, w_ada, b_ada):
    nl, d, n = w_ada.shape
    return pl.pallas_call(
        _adaln_kernel, out_shape=SDS((nl, 16, n), f32), grid=(nl, n // 1024),
        in_specs=[pl.BlockSpec((16, d), lambda l, j: (0, 0)),
                  pl.BlockSpec((1, d, 1024), lambda l, j: (l, 0, j)),
                  pl.BlockSpec((1, 1, 1024), lambda l, j: (l, 0, j))],
        out_specs=pl.BlockSpec((1, 16, 1024), lambda l, j: (l, 0, j)),
        compiler_params=_cp(("arbitrary", "arbitrary")), name="adaln",
    )(cc, w_ada, b_ada.reshape(nl, 1, n))


def _mod_spec(col, tpb, ctx_tiles, nb):
    def imap(i):
        return (jnp.where((i % tpb) >= ctx_tiles, i // tpb, nb), 0, col)
    return pl.BlockSpec((1, 1, 1024), imap)


_K1_GROUPS = (("a_qk", 512, bf16), ("a_v", 512, bf16), ("a_o", 512, bf16), ("b_q", 512, bf16),
              ("b_kv", 256, bf16), ("c_qk", 512, bf16), ("c_v", 512, bf16), ("c_r", 512, bf16),
              ("gate", 3072, bf16), ("small", 128, f32))
_K1_WIDTH = sum(w for _, w, _ in _K1_GROUPS)


def _arrange_w_in(w_in):
    z = jnp.zeros(w_in.shape[:-1] + (LANES - 48,), w_in.dtype)
    w = jnp.concatenate([w_in[..., 0:1024], w_in[..., 1040:1552], w_in[..., 1552:2320], w_in[..., 2320:3344],
                         w_in[..., 3376:3888], w_in[..., 3888:6960], w_in[..., 1024:1040], w_in[..., 3344:3376], z],
                        axis=-1)
    assert w.shape[-1] == _K1_WIDTH
    return w.astype(bf16)


def _inproj_kernel(x_ref, sh_ref, sc_ref, g_ref, w_ref, *out_refs):
    h = _rms_mod(x_ref[...], g_ref[...], sh_ref[0], sc_ref[0]).astype(bf16)
    c0 = 0
    for (_, width, dt), o_ref in zip(_K1_GROUPS, out_refs):
        for j in range(0, width, 512):
            wj = min(512, width - j)
            o_ref[:, j:j + wj] = _dot(h, w_ref[:, c0 + j:c0 + j + wj]).astype(dt)
        c0 += width


def _inproj(x, mods, g, w, tpb, ctx_tiles, nb):
    t, d = x.shape
    outs = tuple(SDS((t, wd), dt) for _, wd, dt in _K1_GROUPS)
    res = pl.pallas_call(
        _inproj_kernel, out_shape=outs, grid=(t // TM,),
        in_specs=[pl.BlockSpec((TM, d), lambda i: (i, 0)),
                  _mod_spec(0, tpb, ctx_tiles, nb), _mod_spec(1, tpb, ctx_tiles, nb),
                  pl.BlockSpec((1, d), lambda i: (0, 0)),
                  pl.BlockSpec((d, _K1_WIDTH), lambda i: (0, 0), pipeline_mode=pl.Buffered(1))],
        out_specs=tuple(pl.BlockSpec((TM, wd), lambda i: (i, 0)) for _, wd, _ in _K1_GROUPS),
        compiler_params=_cp(("arbitrary",)), name="inproj",
    )(x, mods, mods, g, w)
    return dict(zip([n for n, _, _ in _K1_GROUPS], res))


def _bwd_chunk(i, ncc, nch):
    return jnp.where(i < ncc, ncc - 1 - i, nch - 1 + ncc - i)


def _mlstm_kernel(qk_ref, v_ref, sm_ref, cw_ref, cb_ref, gb_ref, o_ref, qk_s, hacc, c_s, *, ctx_len):
    tb = qk_ref.shape[0]
    nch = tb // CHUNK
    ncc = ctx_len // CHUNK
    nu = 2 * A_HEADS

    cw = cw_ref[...]
    cb = cb_ref[...]
    row = _iota((CHUNK, 2 * A_HEADS * A_DQK), 0)
    kscale = jnp.where(_iota((1, 2 * A_HEADS * A_DQK), 1) >= A_HEADS * A_DQK, A_DQK ** -0.5, 1.0)

    def conv_body(c, carry):
        r0 = pl.multiple_of(c * CHUNK, CHUNK)
        x = qk_ref[pl.ds(r0, CHUNK), :].astype(f32)
        rp = pl.multiple_of(jnp.maximum(r0 - 16, 0), 16)
        rn = pl.multiple_of(jnp.minimum(r0 + CHUNK, tb - 16), 16)
        prev = qk_ref[pl.ds(rp, 16), :].astype(f32)[15:16, :]
        nxt = qk_ref[pl.ds(rn, 16), :].astype(f32)[0:1, :]
        has_prev = jnp.logical_and(r0 != 0, r0 != ctx_len)
        has_next = jnp.logical_and(r0 + CHUNK != ctx_len, r0 + CHUNK != tb)
        prev = jnp.where(has_prev, prev, 0.0)
        nxt = jnp.where(has_next, nxt, 0.0)
        xm = jnp.where(row == 0, prev, pltpu.roll(x, 1, 0))
        xp = jnp.where(row == CHUNK - 1, nxt, pltpu.roll(x, CHUNK - 1, 0))
        y = cw[0:1] * xm + cw[1:2] * x + cw[2:3] * xp + cb
        y = y * _sigmoid(y) * kscale
        qk_s[pl.ds(r0, CHUNK), :] = y.astype(bf16)
        return carry

    lax.fori_loop(0, nch, conv_body, 0)

    hacc[...] = jnp.zeros_like(hacc)
    c_s[...] = jnp.zeros_like(c_s)

    rr = _iota((CHUNK, CHUNK), 0)
    cc = _iota((CHUNK, CHUNK), 1)
    tril = (cc <= rr)
    triu = (cc >= rr)
    tril_f = tril.astype(f32)
    triu_f = triu.astype(f32)
    lane = _iota((CHUNK, LANES), 1)
    is_f = jnp.logical_and(lane >= nu, lane < 2 * nu)
    ones_blk = (_iota((CHUNK, A_DV), 1) == 0).astype(bf16)
    gb = gb_ref[...]
    neg_inf = jnp.float32(-jnp.inf)

    def body(i, ms):
        ms = list(ms)
        for d in range(2):
            c = i if d == 0 else _bwd_chunk(i, ncc, nch)
            r0 = pl.multiple_of(c * CHUNK, CHUNK)
            qk = qk_s[pl.ds(r0, CHUNK), :]
            v = v_ref[pl.ds(r0, CHUNK), :]
            sm = sm_ref[pl.ds(r0, CHUNK), :] + gb
            pre = jnp.where(is_f, _log_sigmoid(sm), sm)
            pre_t = pre.T
            t_col, t_row = (tril_f, triu_f) if d == 0 else (triu_f, tril_f)
            bcol = _dot(t_col, pre, HI)
            brow = _dot(pre_t, t_row, HI)
            mask = tril if d == 0 else triu
            last = CHUNK - 1 if d == 0 else 0
            for h in range(A_HEADS):
                u = d * A_HEADS + h
                ci, cf = u, nu + u
                b_c, i_c = bcol[:, cf:cf + 1], pre[:, ci:ci + 1]
                b_r, i_r = brow[cf:cf + 1, :], pre_t[ci:ci + 1, :]
                log_d = jnp.where(mask, b_c - b_r + i_r, neg_inf)
                rmax = jnp.max(log_d, axis=1, keepdims=True)
                pm = jnp.exp(log_d - rmax)
                qh = qk[:, A_DQK * h:A_DQK * (h + 1)]
                kh = qk[:, A_HEADS * A_DQK + A_DQK * h:A_HEADS * A_DQK + A_DQK * (h + 1)]
                s = _dot_nt(qh, kh) * pm
                vaug = jnp.concatenate([v[:, A_DV * h:A_DV * (h + 1)], ones_blk], axis=1)
                p1 = _dot(s.astype(bf16), vaug)
                m = ms[u]
                li = b_c + m
                mj = jnp.maximum(li, rmax)
                cst = c_s[u]
                qc = _dot(qh, cst.astype(bf16))
                na = jnp.exp(rmax - mj) * p1 + jnp.exp(li - mj) * qc
                den = na[:, A_DV:A_DV + 1]
                hout = na[:, :A_DV] / jnp.maximum(jnp.abs(den), jnp.exp(-mj))
                hacc[pl.ds(r0, CHUNK), A_DV * h:A_DV * (h + 1)] += hout
                b_last, m_new, mloc = b_c[last:last + 1], mj[last:last + 1], rmax[last:last + 1]
                wl = jnp.exp(b_last - b_c + i_c - mloc)
                upd = _dot_tn((kh.astype(f32) * wl).astype(bf16), vaug)
                c_s[u] = jnp.exp(b_last + m - m_new) * cst + jnp.exp(mloc - m_new) * upd
                ms[u] = m_new
        return tuple(ms)

    lax.fori_loop(0, nch, body, tuple(jnp.zeros((1, 1), f32) for _ in range(nu)))
    o_ref[...] = hacc[...].astype(bf16)


def _mlstm(a_qk, a_v, small, conv_w, conv_b, gate_b, nb, ctx_len):
    t = a_qk.shape[0]
    tb = t // nb
    gb = jnp.zeros((1, LANES), f32).at[0, :4 * A_HEADS].set(gate_b.reshape(-1))
    return pl.pallas_call(
        functools.partial(_mlstm_kernel, ctx_len=ctx_len),
        out_shape=SDS((t, MIX_W), bf16), grid=(nb,),
        in_specs=[pl.BlockSpec((tb, 512), lambda b: (b, 0)), pl.BlockSpec((tb, 512), lambda b: (b, 0)),
                  pl.BlockSpec((tb, LANES), lambda b: (b, 0)),
                  pl.BlockSpec((3, 512), lambda b: (0, 0)), pl.BlockSpec((1, 512), lambda b: (0, 0)),
                  pl.BlockSpec((1, LANES), lambda b: (0, 0))],
        out_specs=pl.BlockSpec((tb, MIX_W), lambda b: (b, 0)),
        scratch_shapes=[pltpu.VMEM((tb, 512), bf16), pltpu.VMEM((tb, MIX_W), f32),
                        pltpu.VMEM((2 * A_HEADS, A_DQK, 2 * A_DV), f32)],
        compiler_params=_cp(("arbitrary",)), name="mlstm",
    )(a_qk, a_v, small, conv_w, conv_b.reshape(1, -1), gb)


def _gla_kernel(qk_ref, v_ref, sm_ref, wup_ref, bup_ref, o_ref, hacc, s_s, *, ctx_len):
    tb = qk_ref.shape[0]
    nch = tb // CHUNK
    ncc = ctx_len // CHUNK
    nsub = CHUNK // SUB
    pw = 2 * C_DK
    pv = 2 * C_DV
    nrow = SUB * SUB

    hacc[...] = jnp.zeros_like(hacc)
    s_s[...] = jnp.zeros_like(s_s)

    r_j = _iota((nrow, SUB), 0) // SUB
    r_s = _iota((nrow, SUB), 0) % SUB
    col = _iota((nrow, SUB), 1)
    e_j = (r_j == col)
    e_s = (r_s == col)
    ej_b = e_j.astype(bf16)
    ejs_f = e_j.astype(f32) - e_s.astype(f32)
    sj = _iota((SUB, nrow), 0)
    sr = _iota((SUB, nrow), 1)
    sel_f = jnp.logical_and(sr // SUB == sj, sr % SUB <= sj).astype(bf16)
    sel_b = jnp.logical_and(sr // SUB == sj, sr % SUB >= sj).astype(bf16)
    t16r = _iota((SUB, SUB), 0)
    t16c = _iota((SUB, SUB), 1)
    tri_f = (t16c <= t16r).astype(f32)
    tri_b = (t16c >= t16r).astype(f32)
    ones2 = (_iota((pw, pv), 0) // C_DK == _iota((pw, pv), 1) // C_DV).astype(bf16)
    bd_mask = (_iota((pv, pw), 0) // C_DV == _iota((pv, pw), 1) // C_DK)
    qscale = C_DK ** -0.5

    def body(i, carry):
        for d in range(2):
            c = i if d == 0 else _bwd_chunk(i, ncc, nch)
            r0 = pl.multiple_of(c * CHUNK, CHUNK)
            sm = sm_ref[pl.ds(r0, CHUNK), :]
            la = _log_sigmoid(_dot(sm, wup_ref[d], HI) + bup_ref[d]) * (1.0 / C_TAU)
            qk = qk_ref[pl.ds(r0, CHUNK), :]
            q = (qk[:, :C_HEADS * C_DK].astype(f32) * qscale).astype(bf16)
            k = qk[:, C_HEADS * C_DK:]
            v = v_ref[pl.ds(r0, CHUNK), :]
            tri, sel = (tri_f, sel_f) if d == 0 else (tri_b, sel_b)
            last = SUB - 1 if d == 0 else 0
            for jj in range(nsub):
                j = jj if d == 0 else nsub - 1 - jj
                rs = slice(SUB * j, SUB * (j + 1))
                for p in range(C_HEADS // 2):
                    la2 = la[rs, pw * p:pw * (p + 1)]
                    q2 = q[rs, pw * p:pw * (p + 1)]
                    k2 = k[rs, pw * p:pw * (p + 1)].astype(f32)
                    v2 = v[rs, pv * p:pv * (p + 1)]
                    b2 = _dot(tri, la2, HI)
                    qx = _dot(ej_b, q2)
                    kx = jnp.tile(k2, (SUB, 1))
                    bdiff = _dot(ejs_f, b2, HI)
                    z = qx * kx * jnp.exp(jnp.minimum(bdiff, 0.0))
                    a = _dot(z.astype(bf16), ones2)
                    w = a * jnp.tile(v2.astype(f32), (SUB, 1))
                    o = _dot(sel, w.astype(bf16))
                    u = d * (C_HEADS // 2) + p
                    st = s_s[u]
                    qd = (q2.astype(f32) * jnp.exp(b2)).astype(bf16)
                    o = o + _dot_nt(qd, st.astype(bf16))
                    hacc[pl.ds(r0 + SUB * j, SUB), pv * p:pv * (p + 1)] += o
                    bl = b2[last:last + 1]
                    kd = (k2 * jnp.exp(bl - b2)).astype(bf16)
                    upd = _dot_tn(v2, kd)
                    s_s[u] = st * jnp.exp(bl) + jnp.where(bd_mask, upd, 0.0)
        return carry

    lax.fori_loop(0, nch, body, 0)
    o_ref[...] = hacc[...].astype(bf16)


def _gla(c_qk, c_v, small, w_up, b_up, nb, ctx_len):
    t = c_qk.shape[0]
    tb = t // nb
    nk = C_HEADS * C_DK
    wz = jnp.zeros((2, LANES, nk), f32)
    wz = wz.at[0, 16:16 + C_RANK].set(w_up[0]).at[1, 16 + C_RANK:16 + 2 * C_RANK].set(w_up[1])
    return pl.pallas_call(
        functools.partial(_gla_kernel, ctx_len=ctx_len),
        out_shape=SDS((t, MIX_W), bf16), grid=(nb,),
        in_specs=[pl.BlockSpec((tb, 512), lambda b: (b, 0)), pl.BlockSpec((tb, 512), lambda b: (b, 0)),
                  pl.BlockSpec((tb, LANES), lambda b: (b, 0)),
                  pl.BlockSpec((2, LANES, nk), lambda b: (0, 0, 0)), pl.BlockSpec((2, 1, nk), lambda b: (0, 0, 0))],
        out_specs=pl.BlockSpec((tb, MIX_W), lambda b: (b, 0)),
        scratch_shapes=[pltpu.VMEM((tb, MIX_W), f32), pltpu.VMEM((C_HEADS, 2 * C_DV, 2 * C_DK), f32)],
        compiler_params=_cp(("arbitrary",)), name="gla",
    )(c_qk, c_v, small, wz, b_up.reshape(2, 1, nk))


def _rope_tables(seq, ctx_len):
    n_f = B_DH // 4
    t = np.arange(seq)
    freqs = ROPE_THETA ** (-np.arange(n_f, dtype=np.float32) / n_f)
    hd = np.arange(B_DH)
    pos = np.where(hd[None, :] < B_DH // 2, (t // GRID_W)[:, None], (t % GRID_W)[:, None]).astype(np.float32)
    ang = jnp.asarray(pos * freqs[hd % n_f][None, :], f32)
    sign = np.where((hd % (2 * n_f)) < n_f, -1.0, 1.0).astype(np.float32)
    cos = jnp.concatenate([jnp.ones((ctx_len, B_DH), f32), jnp.cos(ang)], axis=0)
    sin = jnp.concatenate([jnp.zeros((ctx_len, B_DH), f32), jnp.sin(ang) * sign[None, :]], axis=0)
    return jnp.tile(cos, (1, 2)), jnp.tile(sin, (1, 2))


def _attn_prep_kernel(q_ref, kv_ref, cos_ref, sin_ref, qg_ref, kg_ref, qo_ref, ko_ref, vo_ref):
    cos = cos_ref[...]
    sin = sin_ref[...]

    def norm_rope(x, g):
        w = x.shape[1]
        bd = (_iota((w, w), 0) // B_DH == _iota((w, w), 1) // B_DH).astype(bf16)
        ss = _dot((x * x).astype(bf16), bd)
        xn = x * lax.rsqrt(ss * (1.0 / B_DH) + EPS) * g
        first = (_iota(x.shape, 1) % (B_DH // 2)) < (B_DH // 4)
        swapped = jnp.where(first, pltpu.roll(xn, w - B_DH // 4, 1), pltpu.roll(xn, B_DH // 4, 1))
        reps = w // LANES
        return xn * jnp.tile(cos, (1, reps)) + swapped * jnp.tile(sin, (1, reps))

    q = (norm_rope(q_ref[...].astype(f32), qg_ref[...]) * (B_DH ** -0.5)).astype(bf16)
    for h in range(B_HEADS):
        qo_ref[h] = q[:, B_DH * h:B_DH * (h + 1)]
    kv = kv_ref[...]
    k = norm_rope(kv[:, :B_KV * B_DH].astype(f32), kg_ref[...]).astype(bf16)
    for h in range(B_KV):
        ko_ref[h] = k[:, B_DH * h:B_DH * (h + 1)]
        vo_ref[h] = kv[:, B_KV * B_DH + B_DH * h:B_KV * B_DH + B_DH * (h + 1)]


def _attn_prep(b_q, b_kv, cos, sin, qg, kg, tpb):
    t = b_q.shape[0]
    return pl.pallas_call(
        _attn_prep_kernel,
        out_shape=(SDS((B_HEADS, t, B_DH), bf16), SDS((B_KV, t, B_DH), bf16), SDS((B_KV, t, B_DH), bf16)),
        grid=(t // TM,),
        in_specs=[pl.BlockSpec((TM, 512), lambda i: (i, 0)), pl.BlockSpec((TM, 256), lambda i: (i, 0)),
                  pl.BlockSpec((TM, LANES), lambda i: (i % tpb, 0)), pl.BlockSpec((TM, LANES), lambda i: (i % tpb, 0)),
                  pl.BlockSpec((1, 512), lambda i: (0, 0)), pl.BlockSpec((1, LANES), lambda i: (0, 0))],
        out_specs=(pl.BlockSpec((B_HEADS, TM, B_DH), lambda i: (0, i, 0)),
                   pl.BlockSpec((B_KV, TM, B_DH), lambda i: (0, i, 0)),
                   pl.BlockSpec((B_KV, TM, B_DH), lambda i: (0, i, 0))),
        compiler_params=_cp(("arbitrary",)), name="attn_prep",
    )(b_q, b_kv, cos, sin, jnp.tile(qg, B_HEADS).reshape(1, -1), jnp.tile(kg, B_KV).reshape(1, -1))


def _attn_kernel(q_ref, k_ref, v_ref, o_ref, *, ctx_len):
    tb = k_ref.shape[1]
    q = q_ref[...].reshape(B_GROUP * TQ, B_DH)

    def attend(klen):
        s = _dot_nt(q, k_ref[0, :klen, :])
        p = jnp.exp(s - jnp.max(s, axis=-1, keepdims=True))
        o = _dot(p.astype(bf16), v_ref[0, :klen, :]) / jnp.sum(p, axis=-1, keepdims=True)
        o_ref[...] = o.reshape(B_GROUP, TQ, B_DH).astype(bf16)

    is_ctx = pl.program_id(2) < ctx_len // TQ

    @pl.when(is_ctx)
    def _():
        attend(ctx_len)

    @pl.when(jnp.logical_not(is_ctx))
    def _():
        attend(tb)


def _attn(q, k, v, nb, ctx_len):
    t = q.shape[1]
    tb = t // nb
    nq = tb // TQ
    return pl.pallas_call(
        functools.partial(_attn_kernel, ctx_len=ctx_len),
        out_shape=SDS((B_HEADS, t, B_DH), bf16), grid=(nb, B_KV, nq),
        in_specs=[pl.BlockSpec((B_GROUP, TQ, B_DH), lambda b, g, i: (g, b * nq + i, 0)),
                  pl.BlockSpec((1, tb, B_DH), lambda b, g, i: (g, b, 0)),
                  pl.BlockSpec((1, tb, B_DH), lambda b, g, i: (g, b, 0))],
        out_specs=pl.BlockSpec((B_GROUP, TQ, B_DH), lambda b, g, i: (g, b * nq + i, 0)),
        compiler_params=_cp(("arbitrary", "arbitrary", "arbitrary")), name="attn",
    )(q, k, v)


def _head_rms(y, g, dv):
    parts = []
    for h in range(y.shape[1] // dv):
        yh = y[:, dv * h:dv * (h + 1)]
        parts.append(yh * lax.rsqrt(jnp.mean(yh * yh, axis=-1, keepdims=True) + EPS))
    return jnp.concatenate(parts, axis=1) * g


def _merge_kernel(ha_ref, ao_ref, att_ref, hc_ref, cr_ref, gate_ref, x_ref, g1_ref, sh2_ref, sc2_ref,
                  ag_ref, cg_ref, wb_ref, wo_ref, gn2_ref, wr_ref, br_ref,
                  xo_ref, h2_ref, ti_ref, tw_ref, rk_ref, cnt_ref, cnt_s):
    i = pl.program_id(0)

    @pl.when(i == 0)
    def _():
        cnt_s[...] = jnp.zeros_like(cnt_s)

    d = x_ref.shape[1]
    ya = _head_rms(ha_ref[...].astype(f32), ag_ref[...], A_DV) * _sigmoid(ao_ref[...].astype(f32))
    cr = cr_ref[...].astype(f32)
    yc = _head_rms(hc_ref[...].astype(f32), cg_ref[...], C_DV) * (cr * _sigmoid(cr))
    yb = jnp.concatenate([att_ref[h] for h in range(B_HEADS)], axis=1)
    merged = jnp.zeros((TM, d), f32)
    for n, y in enumerate((ya.astype(bf16), yb, yc.astype(bf16))):
        merged = merged + _sigmoid(gate_ref[:, d * n:d * (n + 1)].astype(f32)) * _dot(y, wb_ref[n])
    x = x_ref[...] + g1_ref[0] * _dot(merged.astype(bf16), wo_ref[...])
    xo_ref[...] = x
    h2 = _rms_mod(x, gn2_ref[...], sh2_ref[0], sc2_ref[0])
    h2_ref[...] = h2

    logits = _dot_nt(wr_ref[...], h2, HI) + br_ref[...]
    eid = _iota((N_EXPERTS, TM), 0)
    work = logits
    onehot = jnp.zeros((N_EXPERTS, TM), f32)
    vals, sels = [], []
    for k in range(TOP_K):
        mk = jnp.max(work, axis=0, keepdims=True)
        ik = jnp.min(jnp.where(work == mk, eid, N_EXPERTS), axis=0, keepdims=True)
        sel = eid == ik
        work = jnp.where(sel, -jnp.inf, work)
        onehot = onehot + sel.astype(f32)
        ti_ref[k:k + 1, :] = ik
        vals.append(mk)
        sels.append(sel)
    ex = [jnp.exp(vk - vals[0]) for vk in vals]
    tot = ex[0] + ex[1] + ex[2] + ex[3]
    for k in range(TOP_K):
        tw_ref[k:k + 1, :] = ex[k] / tot

    ut = (_iota((TM, TM), 0) <= _iota((TM, TM), 1)).astype(bf16)
    incl = _dot(onehot.astype(bf16), ut)
    rank = cnt_s[...][:, 0:1] + incl - onehot
    for k in range(TOP_K):
        rk_ref[k:k + 1, :] = jnp.sum(jnp.where(sels[k], rank, 0.0), axis=0, keepdims=True).astype(jnp.int32)
    cnt_s[...] = cnt_s[...] + incl[:, TM - 1:TM]
    cnt_ref[...] = cnt_s[...]


def _merge(p, h_a, att, h_c, x, mods, a_norm_g, c_norm_g, w_branch, w_out, g_norm2, w_router, b_router,
           tpb, ctx_tiles, nb):
    t, d = x.shape
    row = lambda w: pl.BlockSpec((TM, w), lambda i: (i, 0))
    const = lambda shape: pl.BlockSpec(shape, lambda i: tuple(0 for _ in shape))
    return pl.pallas_call(
        _merge_kernel,
        out_shape=(SDS((t, d), f32), SDS((t, d), f32), SDS((TOP_K, t), jnp.int32), SDS((TOP_K, t), f32),
                   SDS((TOP_K, t), jnp.int32), SDS((N_EXPERTS, LANES), f32)),
        grid=(t // TM,),
        in_specs=[row(512), row(512), pl.BlockSpec((B_HEADS, TM, B_DH), lambda i: (0, i, 0)), row(512), row(512),
                  row(3 * d), row(d),
                  _mod_spec(2, tpb, ctx_tiles, nb), _mod_spec(3, tpb, ctx_tiles, nb), _mod_spec(4, tpb, ctx_tiles, nb),
                  const((1, 512)), const((1, 512)), const((3, MIX_W, d)), const((d, d)), const((1, d)),
                  const((N_EXPERTS, d)), const((N_EXPERTS, 1))],
        out_specs=(row(d), row(d), pl.BlockSpec((TOP_K, TM), lambda i: (0, i)), pl.BlockSpec((TOP_K, TM), lambda i: (0, i)),
                   pl.BlockSpec((TOP_K, TM), lambda i: (0, i)), const((N_EXPERTS, LANES))),
        scratch_shapes=[pltpu.VMEM((N_EXPERTS, LANES), f32)],
        compiler_params=_cp(("arbitrary",)), name="merge",
    )(h_a, p["a_o"], att, h_c, p["c_r"], p["gate"], x, mods, mods, mods,
      a_norm_g.reshape(1, -1), c_norm_g.reshape(1, -1), w_branch, w_out, g_norm2.reshape(1, -1),
      w_router.T, b_router.reshape(-1, 1))


def _sc_mesh():
    return plsc.VectorSubcoreMesh(core_axis_name="c", subcore_axis_name="s")


def _sc_scatter_rows(src, idx, n_out):
    v, d = src.shape
    n = idx.shape[0]
    per_w = n // (SC_NUM_CORES * SC_NUM_SUBCORES)
    assert per_w % SC_WINDOW == 0 and v % SC_WINDOW == 0

    @functools.partial(pl.kernel, out_type=SDS((n_out, d), src.dtype), mesh=_sc_mesh(),
                       scratch_types=[pltpu.VMEM((SC_WINDOW,), jnp.int32), pltpu.VMEM((SC_WINDOW, d), src.dtype),
                                      pltpu.SemaphoreType.DMA])
    def k(x_hbm, i_hbm, o_hbm, idx_v, rows_v, sem):
        wid = lax.axis_index("s") * SC_NUM_CORES + lax.axis_index("c")

        @pl.loop(0, per_w // SC_WINDOW)
        def _(j):
            base = wid * per_w + j * SC_WINDOW
            pltpu.sync_copy(i_hbm.at[pl.ds(base, SC_WINDOW)], idx_v)
            pltpu.sync_copy(x_hbm.at[pl.ds(lax.rem(base, v), SC_WINDOW)], rows_v)
            pltpu.async_copy(rows_v, o_hbm.at[idx_v], sem).wait()

    return k(src, idx)


def _sc_gather_rows(table, idx):
    d = table.shape[1]
    n = idx.shape[0]
    per_w = n // (SC_NUM_CORES * SC_NUM_SUBCORES)
    assert per_w % SC_WINDOW == 0

    @functools.partial(pl.kernel, out_type=SDS((n, d), table.dtype), mesh=_sc_mesh(),
                       scratch_types=[pltpu.VMEM((SC_WINDOW,), jnp.int32), pltpu.VMEM((SC_WINDOW, d), table.dtype),
                                      pltpu.SemaphoreType.DMA])
    def k(x_hbm, i_hbm, o_hbm, idx_v, rows_v, sem):
        wid = lax.axis_index("s") * SC_NUM_CORES + lax.axis_index("c")

        @pl.loop(0, per_w // SC_WINDOW)
        def _(j):
            base = wid * per_w + j * SC_WINDOW
            pltpu.sync_copy(i_hbm.at[pl.ds(base, SC_WINDOW)], idx_v)
            pltpu.async_copy(x_hbm.at[idx_v], rows_v, sem).wait()
            pltpu.sync_copy(rows_v, o_hbm.at[pl.ds(base, SC_WINDOW)])

    return k(table, idx)


def _expert_kernel(te_ref, nv_ref, x_ref, w1g_ref, w1u_ref, b1g_ref, b1u_ref, w2_ref, b2_ref, y_ref):
    @pl.when(pl.program_id(0) < nv_ref[0])
    def _():
        x = x_ref[...].astype(bf16)
        gate = jnp.minimum(_dot(x, w1g_ref[0]) + b1g_ref[0], SWIGLU_LIMIT)
        up = jnp.clip(_dot(x, w1u_ref[0]) + b1u_ref[0], -SWIGLU_LIMIT, SWIGLU_LIMIT)
        a = (up + 1.0) * gate * _sigmoid(SWIGLU_ALPHA * gate)
        y_ref[...] = _dot(a.astype(bf16), w2_ref[0]) + b2_ref[0]


def _experts(xs, tile_e, n_valid, w1g, w1u, b1g, b1u, w2, b2):
    p, d = xs.shape
    dff = w1g.shape[-1]
    row = lambda i, te, nv: (jnp.minimum(i, nv[0] - 1), 0)
    wsel = lambda i, te, nv: (te[i], 0, 0)
    return pl.pallas_call(
        _expert_kernel, out_shape=SDS((p, d), f32),
        grid_spec=pltpu.PrefetchScalarGridSpec(
            num_scalar_prefetch=2, grid=(p // TME,),
            in_specs=[pl.BlockSpec((TME, d), row),
                      pl.BlockSpec((1, d, dff), wsel), pl.BlockSpec((1, d, dff), wsel),
                      pl.BlockSpec((1, 1, dff), wsel), pl.BlockSpec((1, 1, dff), wsel),
                      pl.BlockSpec((1, dff, d), wsel), pl.BlockSpec((1, 1, d), wsel)],
            out_specs=pl.BlockSpec((TME, d), row)),
        compiler_params=_cp(("arbitrary",)), name="experts",
    )(tile_e, n_valid, xs, w1g, w1u, b1g, b1u, w2, b2)


def _combine_kernel(x_ref, yg_ref, w_ref, g2_ref, *rest, final):
    w = w_ref[...]
    acc = w[:, 0:1] * yg_ref[0]
    for k in range(1, TOP_K):
        acc = acc + w[:, k:k + 1] * yg_ref[k]
    x = x_ref[...] + g2_ref[0] * acc
    if final:
        gf_ref, o_ref = rest
        x = x * lax.rsqrt(jnp.mean(x * x, axis=-1, keepdims=True) + EPS) * gf_ref[...]
    else:
        (o_ref,) = rest
    o_ref[...] = x


def _combine(x, yg, wcol, mods, tpb, ctx_tiles, nb, g_final=None):
    t, d = x.shape
    final = g_final is not None
    if final:
        lat = tpb - ctx_tiles
        rmap = lambda i: ((i // lat) * tpb + ctx_tiles + i % lat)
        grid = (nb * lat,)
        mod = pl.BlockSpec((1, 1, 1024), lambda i: (i // lat, 0, 5))
        n_out = nb * lat * TM
    else:
        rmap = lambda i: i
        grid = (t // TM,)
        mod = _mod_spec(5, tpb, ctx_tiles, nb)
        n_out = t
    in_specs = [pl.BlockSpec((TM, d), lambda i: (rmap(i), 0)),
                pl.BlockSpec((TOP_K, TM, d), lambda i: (0, rmap(i), 0)),
                pl.BlockSpec((TM, TOP_K), lambda i: (rmap(i), 0)), mod]
    args = [x, yg, wcol, mods]
    if final:
        in_specs.append(pl.BlockSpec((1, d), lambda i: (0, 0)))
        args.append(g_final.reshape(1, -1))
    return pl.pallas_call(
        functools.partial(_combine_kernel, final=final), out_shape=SDS((n_out, d), f32), grid=grid,
        in_specs=in_specs, out_specs=pl.BlockSpec((TM, d), lambda i: (i, 0)),
        compiler_params=_cp(("arbitrary",)), name="combine_final" if final else "combine",
    )(*args)


def _routing_tables(top_i, rank, counts, n_tiles):
    cnt = counts[:, 0].astype(jnp.int32)
    padded = ((cnt + TME - 1) // TME) * TME
    ends = jnp.cumsum(padded)
    starts = ends - padded
    pos = (starts[top_i] + rank).reshape(-1)
    n_valid = ends[-1] // TME
    tile_start = jnp.arange(n_tiles, dtype=jnp.int32) * TME
    tile_e = jnp.sum(tile_start[:, None] >= ends[None, :], axis=1).astype(jnp.int32)
    tile_e = jnp.where(jnp.arange(n_tiles) < n_valid, tile_e, tile_e[jnp.maximum(n_valid - 1, 0)])
    tile_e = jnp.minimum(tile_e, N_EXPERTS - 1)
    return pos, tile_e, n_valid.reshape(1).astype(jnp.int32)


def kernel(x, c, ctx, c_ctx, w_ada, b_ada, g_norm1, w_in, a_conv_w, a_conv_b, a_gate_b, a_norm_g, b_q_norm_g, b_k_norm_g, c_w_up, c_b_up, c_norm_g, w_branch, w_out, g_norm2, w_router, b_router, w_e1, b_e1, w_e2, b_e2, g_final):
    nb, seq, d = x.shape
    ctx_len = ctx.shape[1]
    depth = w_ada.shape[0]
    tb = ctx_len + seq
    t = nb * tb
    tpb, ctx_tiles = tb // TM, ctx_len // TM
    assert d == 1024 and nb < 16 and seq % TM == 0 and ctx_len % TM == 0 and ctx_len % TQ == 0
    n_assign = TOP_K * t
    n_sorted = n_assign + N_EXPERTS * TME
    n_tiles = n_sorted // TME

    xs = jnp.concatenate([ctx, x], axis=1).reshape(t, d)
    cc = jnp.zeros((16, d), f32).at[:nb].set(c).at[nb].set(c_ctx)
    mods_all = _adaln(cc, w_ada, b_ada)
    cos, sin = _rope_tables(seq, ctx_len)
    w_in_r = _arrange_w_in(w_in)
    w_branch_b, w_out_b = w_branch.astype(bf16), w_out.astype(bf16)
    w1g, w1u = w_e1[..., 0::2].astype(bf16), w_e1[..., 1::2].astype(bf16)
    b1g, b1u = b_e1[..., None, 0::2], b_e1[..., None, 1::2]
    w2b, b2 = w_e2.astype(bf16), b_e2[..., None, :]

    out = None
    for l in range(depth):
        mods = mods_all[l].reshape(16, 1, 6 * d)
        p = _inproj(xs, mods, g_norm1[l].reshape(1, -1), w_in_r[l], tpb, ctx_tiles, nb)
        h_a = _mlstm(p["a_qk"], p["a_v"], p["small"], a_conv_w[l], a_conv_b[l], a_gate_b[l], nb, ctx_len)
        h_c = _gla(p["c_qk"], p["c_v"], p["small"], c_w_up[l], c_b_up[l], nb, ctx_len)
        qn, kn, vn = _attn_prep(p["b_q"], p["b_kv"], cos, sin, b_q_norm_g[l], b_k_norm_g[l], tpb)
        att = _attn(qn, kn, vn, nb, ctx_len)
        xs, h2, top_i, top_w, rank, counts = _merge(
            p, h_a, att, h_c, xs, mods, a_norm_g[l], c_norm_g[l], w_branch_b[l], w_out_b[l], g_norm2[l],
            w_router[l], b_router[l], tpb, ctx_tiles, nb)
        pos, tile_e, n_valid = _routing_tables(top_i, rank, counts, n_tiles)
        x_sorted = _sc_scatter_rows(h2, pos, n_sorted)
        y_sorted = _experts(x_sorted, tile_e, n_valid, w1g[l], w1u[l], b1g[l], b1u[l], w2b[l], b2[l])
        yg = _sc_gather_rows(y_sorted, pos).reshape(TOP_K, t, d)
        if l == depth - 1:
            out = _combine(xs, yg, top_w.T, mods, tpb, ctx_tiles, nb, g_final=g_final)
        else:
            xs = _combine(xs, yg, top_w.T, mods, tpb, ctx_tiles, nb)
    return out.reshape(nb, seq, d)
```

```python
import functools

import jax
import jax.numpy as jnp
import numpy as np
from jax import lax
from jax.experimental import pallas as pl
from jax.experimental.pallas import tpu as pltpu
from jax.experimental.pallas import tpu_sc as plsc

f32 = jnp.float32
bf16 = jnp.bfloat16
HI = lax.Precision.HIGHEST
SDS = jax.ShapeDtypeStruct

EPS = 1e-6
CHUNK = 64
GRID_W = 64
ROPE_THETA = 10000.0
MIX_W = 512
A_HEADS, A_DQK, A_DV = 4, 64, 128
B_HEADS, B_KV, B_DH = 8, 2, 64
B_GROUP = B_HEADS // B_KV
C_HEADS, C_DK, C_DV, C_RANK, C_TAU = 4, 64, 128, 16, 16.0
N_EXPERTS, TOP_K = 32, 4
SWIGLU_LIMIT, SWIGLU_ALPHA = 7.0, 1.702

LANES = 128
SC_NUM_CORES = 2
SC_NUM_SUBCORES = 16
VMEM_LIMIT = 56 * 2 ** 20

TM = 256
TQ = 128
TME = 512
KEY_CHUNK = 512
SC_WINDOW = 32


def _cp(sem, vmem=VMEM_LIMIT):
    return pltpu.CompilerParams(dimension_semantics=sem, vmem_limit_bytes=vmem)


def _dot(a, b, precision=None):
    return jnp.dot(a, b, preferred_element_type=f32, precision=precision)


def _dot_nt(a, b, precision=None):
    return lax.dot_general(a, b, (((1,), (1,)), ((), ())), preferred_element_type=f32, precision=precision)


def _dot_tn(a, b):
    return lax.dot_general(a, b, (((0,), (0,)), ((), ())), preferred_element_type=f32)


def _sigmoid(x):
    return 1.0 / (1.0 + jnp.exp(-x))


def _log_sigmoid(x):
    return jnp.minimum(x, 0.0) - jnp.log(1.0 + jnp.exp(-jnp.abs(x)))


def _iota(shape, axis):
    return lax.broadcasted_iota(jnp.int32, shape, axis)


def _rms_mod(x, g, shift, scale):
    y = x * lax.rsqrt(jnp.mean(x * x, axis=-1, keepdims=True) + EPS) * g
    return y * (1.0 + scale) + shift


def _adaln_kernel(cc_ref, w_ref, b_ref, o_ref):
    cc = cc_ref[...]
    s = cc * _sigmoid(cc)
    o_ref[0] = _dot(s, w_ref[0], HI) + b_ref[0]


def _adaln(cc, w_ada, b_ada):
    nl, d, n = w_ada.shape
    return pl.pallas_call(
        _adaln_kernel, out_shape=SDS((nl, 16, n), f32), grid=(nl, n // 1024),
        in_specs=[pl.BlockSpec((16, d), lambda l, j: (0, 0)),
                  pl.BlockSpec((1, d, 1024), lambda l, j: (l, 0, j)),
                  pl.BlockSpec((1, 1, 1024), lambda l, j: (l, 0, j))],
        out_specs=pl.BlockSpec((1, 16, 1024), lambda l, j: (l, 0, j)),
        compiler_params=_cp(("arbitrary", "arbitrary")), name="adaln",
    )(cc, w_ada, b_ada.reshape(nl, 1, n))


def _mod_spec(col, tpb, ctx_tiles, nb):
    def imap(i):
        return (jnp.where((i % tpb) >= ctx_tiles, i // tpb, nb), 0, col)
    return pl.BlockSpec((1, 1, 1024), imap)


_K1_GROUPS = (("a_qk", 512, bf16), ("a_v", 512, bf16), ("a_o", 512, bf16), ("b_q", 512, bf16),
              ("b_kv", 256, bf16), ("c_qk", 512, bf16), ("c_v", 512, bf16), ("c_r", 512, bf16),
              ("gate", 3072, bf16), ("small", 128, f32))
_K1_WIDTH = sum(w for _, w, _ in _K1_GROUPS)


def _arrange_w_in(w_in):
    z = jnp.zeros(w_in.shape[:-1] + (LANES - 48,), w_in.dtype)
    w = jnp.concatenate([w_in[..., 0:1024], w_in[..., 1040:1552], w_in[..., 1552:2320], w_in[..., 2320:3344],
                         w_in[..., 3376:3888], w_in[..., 3888:6960], w_in[..., 1024:1040], w_in[..., 3344:3376], z],
                        axis=-1)
    assert w.shape[-1] == _K1_WIDTH
    return w.astype(bf16)


def _inproj_kernel(x_ref, sh_ref, sc_ref, g_ref, w_ref, *out_refs):
    h = _rms_mod(x_ref[...], g_ref[...], sh_ref[0], sc_ref[0]).astype(bf16)
    c0 = 0
    for (_, width, dt), o_ref in zip(_K1_GROUPS, out_refs):
        for j in range(0, width, 512):
            wj = min(512, width - j)
            o_ref[:, j:j + wj] = _dot(h, w_ref[:, c0 + j:c0 + j + wj]).astype(dt)
        c0 += width


def _inproj(x, mods, g, w, tpb, ctx_tiles, nb):
    t, d = x.shape
    outs = tuple(SDS((t, wd), dt) for _, wd, dt in _K1_GROUPS)
    res = pl.pallas_call(
        _inproj_kernel, out_shape=outs, grid=(t // TM,),
        in_specs=[pl.BlockSpec((TM, d), lambda i: (i, 0)),
                  _mod_spec(0, tpb, ctx_tiles, nb), _mod_spec(1, tpb, ctx_tiles, nb),
                  pl.BlockSpec((1, d), lambda i: (0, 0)),
                  pl.BlockSpec((d, _K1_WIDTH), lambda i: (0, 0), pipeline_mode=pl.Buffered(1))],
        out_specs=tuple(pl.BlockSpec((TM, wd), lambda i: (i, 0)) for _, wd, _ in _K1_GROUPS),
        compiler_params=_cp(("arbitrary",)), name="inproj",
    )(x, mods, mods, g, w)
    return dict(zip([n for n, _, _ in _K1_GROUPS], res))


def _bwd_chunk(i, ncc, nch):
    return jnp.where(i < ncc, ncc - 1 - i, nch - 1 + ncc - i)


def _mlstm_kernel(qk_ref, v_ref, sm_ref, cw_ref, cb_ref, gb_ref, o_ref, qk_s, hacc, c_s, *, ctx_len):
    tb = qk_ref.shape[0]
    nch = tb // CHUNK
    ncc = ctx_len // CHUNK
    nu = 2 * A_HEADS

    cw = cw_ref[...]
    cb = cb_ref[...]
    row = _iota((CHUNK, 2 * A_HEADS * A_DQK), 0)
    kscale = jnp.where(_iota((1, 2 * A_HEADS * A_DQK), 1) >= A_HEADS * A_DQK, A_DQK ** -0.5, 1.0)

    def conv_body(c, carry):
        r0 = pl.multiple_of(c * CHUNK, CHUNK)
        x = qk_ref[pl.ds(r0, CHUNK), :].astype(f32)
        rp = pl.multiple_of(jnp.maximum(r0 - 16, 0), 16)
        rn = pl.multiple_of(jnp.minimum(r0 + CHUNK, tb - 16), 16)
        prev = qk_ref[pl.ds(rp, 16), :].astype(f32)[15:16, :]
        nxt = qk_ref[pl.ds(rn, 16), :].astype(f32)[0:1, :]
        has_prev = jnp.logical_and(r0 != 0, r0 != ctx_len)
        has_next = jnp.logical_and(r0 + CHUNK != ctx_len, r0 + CHUNK != tb)
        prev = jnp.where(has_prev, prev, 0.0)
        nxt = jnp.where(has_next, nxt, 0.0)
        xm = jnp.where(row == 0, prev, pltpu.roll(x, 1, 0))
        xp = jnp.where(row == CHUNK - 1, nxt, pltpu.roll(x, CHUNK - 1, 0))
        y = cw[0:1] * xm + cw[1:2] * x + cw[2:3] * xp + cb
        y = y * _sigmoid(y) * kscale
        qk_s[pl.ds(r0, CHUNK), :] = y.astype(bf16)
        return carry

    lax.fori_loop(0, nch, conv_body, 0)

    hacc[...] = jnp.zeros_like(hacc)
    c_s[...] = jnp.zeros_like(c_s)

    rr = _iota((CHUNK, CHUNK), 0)
    cc = _iota((CHUNK, CHUNK), 1)
    tril = (cc <= rr)
    triu = (cc >= rr)
    tril_f = tril.astype(f32)
    triu_f = triu.astype(f32)
    lane = _iota((CHUNK, LANES), 1)
    is_f = jnp.logical_and(lane >= nu, lane < 2 * nu)
    ones_blk = (_iota((CHUNK, A_DV), 1) == 0).astype(bf16)
    gb = gb_ref[...]
    neg_inf = jnp.float32(-jnp.inf)

    def body(i, ms):
        ms = list(ms)
        for d in range(2):
            c = i if d == 0 else _bwd_chunk(i, ncc, nch)
            r0 = pl.multiple_of(c * CHUNK, CHUNK)
            qk = qk_s[pl.ds(r0, CHUNK), :]
            v = v_ref[pl.ds(r0, CHUNK), :]
            sm = sm_ref[pl.ds(r0, CHUNK), :] + gb
            pre = jnp.where(is_f, _log_sigmoid(sm), sm)
            pre_t = pre.T
            t_col, t_row = (tril_f, triu_f) if d == 0 else (triu_f, tril_f)
            bcol = _dot(t_col, pre, HI)
            brow = _dot(pre_t, t_row, HI)
            mask = tril if d == 0 else triu
            last = CHUNK - 1 if d == 0 else 0
            houts, new_c = [], []
            c_all = c_s[d]
            for h in range(A_HEADS):
                u = d * A_HEADS + h
                ci, cf = u, nu + u
                b_c, i_c = bcol[:, cf:cf + 1], pre[:, ci:ci + 1]
                b_r, i_r = brow[cf:cf + 1, :], pre_t[ci:ci + 1, :]
                log_d = jnp.where(mask, b_c - b_r + i_r, neg_inf)
                rmax = jnp.max(log_d, axis=1, keepdims=True)
                pm = jnp.exp(log_d - rmax)
                qh = qk[:, A_DQK * h:A_DQK * (h + 1)]
                kh = qk[:, A_HEADS * A_DQK + A_DQK * h:A_HEADS * A_DQK + A_DQK * (h + 1)]
                s = _dot_nt(qh, kh) * pm
                vaug = jnp.concatenate([v[:, A_DV * h:A_DV * (h + 1)], ones_blk], axis=1)
                p1 = _dot(s.astype(bf16), vaug)
                m = ms[u]
                li = b_c + m
                mj = jnp.maximum(li, rmax)
                cst = c_all[h]
                qc = _dot(qh, cst.astype(bf16))
                na = jnp.exp(rmax - mj) * p1 + jnp.exp(li - mj) * qc
                den = na[:, A_DV:A_DV + 1]
                houts.append(na[:, :A_DV] / jnp.maximum(jnp.abs(den), jnp.exp(-mj)))
                b_last, m_new, mloc = b_c[last:last + 1], mj[last:last + 1], rmax[last:last + 1]
                wl = jnp.exp(b_last - b_c + i_c - mloc)
                upd = _dot_tn((kh.astype(f32) * wl).astype(bf16), vaug)
                new_c.append(jnp.exp(b_last + m - m_new) * cst + jnp.exp(mloc - m_new) * upd)
                ms[u] = m_new
            hacc[pl.ds(r0, CHUNK), :] += jnp.concatenate(houts, axis=1)
            c_s[d] = jnp.stack(new_c)
        return tuple(ms)

    lax.fori_loop(0, nch, body, tuple(jnp.zeros((1, 1), f32) for _ in range(nu)))
    o_ref[...] = hacc[...].astype(bf16)


def _mlstm(a_qk, a_v, small, conv_w, conv_b, gate_b, nb, ctx_len):
    t = a_qk.shape[0]
    tb = t // nb
    gb = jnp.zeros((1, LANES), f32).at[0, :4 * A_HEADS].set(gate_b.reshape(-1))
    return pl.pallas_call(
        functools.partial(_mlstm_kernel, ctx_len=ctx_len),
        out_shape=SDS((t, MIX_W), bf16), grid=(nb,),
        in_specs=[pl.BlockSpec((tb, 512), lambda b: (b, 0)), pl.BlockSpec((tb, 512), lambda b: (b, 0)),
                  pl.BlockSpec((tb, LANES), lambda b: (b, 0)),
                  pl.BlockSpec((3, 512), lambda b: (0, 0)), pl.BlockSpec((1, 512), lambda b: (0, 0)),
                  pl.BlockSpec((1, LANES), lambda b: (0, 0))],
        out_specs=pl.BlockSpec((tb, MIX_W), lambda b: (b, 0)),
        scratch_shapes=[pltpu.VMEM((tb, 512), bf16), pltpu.VMEM((tb, MIX_W), f32),
                        pltpu.VMEM((2, A_HEADS, A_DQK, 2 * A_DV), f32)],
        compiler_params=_cp(("arbitrary",)), name="mlstm",
    )(a_qk, a_v, small, conv_w, conv_b.reshape(1, -1), gb)


def _split3(x):
    hi = x.astype(bf16)
    r1 = x - hi.astype(f32)
    mid = r1.astype(bf16)
    lo = (r1 - mid.astype(f32)).astype(bf16)
    return jnp.concatenate([hi, mid, lo], axis=0)


GLA_LEVELS = 6


def _gla_constants():
    t = np.arange(CHUNK)
    tri, gref, lmask = [], [], []
    for d in range(2):
        pos = t if d == 0 else CHUNK - 1 - t
        row_of = np.argsort(pos)
        tri.append(pos[None, :] <= pos[:, None])
        g, m = [], []
        for lvl in range(GLA_LEVELS):
            half = CHUNK >> (lvl + 1)
            ref_pos = (pos // (2 * half)) * (2 * half) + half
            g.append(t[None, :] == row_of[ref_pos][:, None])
            late, early = (pos % (2 * half)) >= half, (pos % (2 * half)) < half
            same = (pos[:, None] // (2 * half)) == (pos[None, :] // (2 * half))
            m.append(same & late[:, None] & early[None, :])
        m.append(t[:, None] == t[None, :])
        gref.append(np.concatenate(g, axis=0))
        lmask.append(np.stack([np.tile(x, (1, C_HEADS)) for x in m]))
    tri3 = np.stack([np.tile(x, (1, 3)) for x in tri])
    gref3 = np.stack([np.tile(x, (1, 3)) for x in gref])
    hs = np.arange(C_HEADS * CHUNK)
    kmask = (hs[:, None] // CHUNK) == (np.arange(C_HEADS * C_DK)[None, :] // C_DK)
    vmask = (hs[:, None] // CHUNK) == (np.arange(C_HEADS * C_DV)[None, :] // C_DV)
    bdm = (np.arange(2 * C_DV)[:, None] // C_DV) == (np.arange(2 * C_DK)[None, :] // C_DK)
    as_b = lambda x: jnp.asarray(x, bf16)
    return as_b(tri3), as_b(gref3), jnp.asarray(np.stack(lmask), f32), as_b(kmask), as_b(vmask), jnp.asarray(bdm, f32)


def _gla_kernel(qk_ref, v_ref, sm_ref, wup_ref, bup_ref, tri_ref, gref_ref, lmask_ref, kmask_ref, vmask_ref, bdm_ref,
                o_ref, hacc, s_s, *, ctx_len):
    tb = qk_ref.shape[0]
    nch = tb // CHUNK
    ncc = ctx_len // CHUNK
    nk = C_HEADS * C_DK
    pw = 2 * C_DK
    pv = 2 * C_DV
    npair = C_HEADS // 2

    hacc[...] = jnp.zeros_like(hacc)
    s_s[...] = jnp.zeros_like(s_s)
    qscale = C_DK ** -0.5

    def body(i, carry):
        for d in range(2):
            c = i if d == 0 else _bwd_chunk(i, ncc, nch)
            r0 = pl.multiple_of(c * CHUNK, CHUNK)
            sm = sm_ref[pl.ds(r0, CHUNK), :]
            la = _log_sigmoid(_dot(sm, wup_ref[d], HI) + bup_ref[d]) * (1.0 / C_TAU)
            qk = qk_ref[pl.ds(r0, CHUNK), :]
            q = qk[:, :nk].astype(f32) * qscale
            k = qk[:, nk:].astype(f32)
            v = v_ref[pl.ds(r0, CHUNK), :]
            b = _dot(tri_ref[d], _split3(la))
            bref = _dot(gref_ref[d], _split3(b))

            def scores(qe, ke):
                kst = jnp.tile(ke.astype(bf16), (C_HEADS, 1)) * kmask_ref[...]
                return _dot_nt(qe.astype(bf16), kst)

            a = scores(q, k) * lmask_ref[d, GLA_LEVELS]
            for lvl in range(GLA_LEVELS):
                rel = b - bref[CHUNK * lvl:CHUNK * (lvl + 1)]
                s = scores(q * jnp.exp(jnp.minimum(rel, 0.0)), k * jnp.exp(jnp.minimum(-rel, 0.0)))
                a = a + s * lmask_ref[d, lvl]
            vst = jnp.tile(v, (C_HEADS, 1)) * vmask_ref[...]
            o = _dot(a.astype(bf16), vst)

            last = CHUNK - 1 if d == 0 else 0
            bl = b[last:last + 1]
            qd = (q * jnp.exp(b)).astype(bf16)
            kd = (k * jnp.exp(bl - b)).astype(bf16)
            dec = jnp.exp(bl)
            o_int = []
            for p in range(npair):
                u = d * npair + p
                st = s_s[u]
                o_int.append(_dot_nt(qd[:, pw * p:pw * (p + 1)], st.astype(bf16)))
                upd = _dot_tn(v[:, pv * p:pv * (p + 1)], kd[:, pw * p:pw * (p + 1)])
                s_s[u] = st * dec[:, pw * p:pw * (p + 1)] + bdm_ref[...] * upd
            hacc[pl.ds(r0, CHUNK), :] += o + jnp.concatenate(o_int, axis=1)
        return carry

    lax.fori_loop(0, nch, body, 0)
    o_ref[...] = hacc[...].astype(bf16)


def _gla(c_qk, c_v, small, w_up, b_up, nb, ctx_len):
    t = c_qk.shape[0]
    tb = t // nb
    nk = C_HEADS * C_DK
    wz = jnp.zeros((2, LANES, nk), f32)
    wz = wz.at[0, 16:16 + C_RANK].set(w_up[0]).at[1, 16 + C_RANK:16 + 2 * C_RANK].set(w_up[1])
    consts = _gla_constants()
    const_spec = lambda a: pl.BlockSpec(a.shape, lambda b, n=a.ndim: (0,) * n)
    return pl.pallas_call(
        functools.partial(_gla_kernel, ctx_len=ctx_len),
        out_shape=SDS((t, MIX_W), bf16), grid=(nb,),
        in_specs=[pl.BlockSpec((tb, 512), lambda b: (b, 0)), pl.BlockSpec((tb, 512), lambda b: (b, 0)),
                  pl.BlockSpec((tb, LANES), lambda b: (b, 0)),
                  pl.BlockSpec((2, LANES, nk), lambda b: (0, 0, 0)), pl.BlockSpec((2, 1, nk), lambda b: (0, 0, 0))]
                 + [const_spec(a) for a in consts],
        out_specs=pl.BlockSpec((tb, MIX_W), lambda b: (b, 0)),
        scratch_shapes=[pltpu.VMEM((tb, MIX_W), f32), pltpu.VMEM((C_HEADS, 2 * C_DV, 2 * C_DK), f32)],
        compiler_params=_cp(("arbitrary",)), name="gla",
    )(c_qk, c_v, small, wz, b_up.reshape(2, 1, nk), *consts)


def _rope_tables(seq, ctx_len):
    n_f = B_DH // 4
    t = np.arange(seq)
    freqs = ROPE_THETA ** (-np.arange(n_f, dtype=np.float32) / n_f)
    hd = np.arange(B_DH)
    pos = np.where(hd[None, :] < B_DH // 2, (t // GRID_W)[:, None], (t % GRID_W)[:, None]).astype(np.float32)
    ang = jnp.asarray(pos * freqs[hd % n_f][None, :], f32)
    sign = np.where((hd % (2 * n_f)) < n_f, -1.0, 1.0).astype(np.float32)
    cos = jnp.concatenate([jnp.ones((ctx_len, B_DH), f32), jnp.cos(ang)], axis=0)
    sin = jnp.concatenate([jnp.zeros((ctx_len, B_DH), f32), jnp.sin(ang) * sign[None, :]], axis=0)
    return jnp.tile(cos, (1, 2)), jnp.tile(sin, (1, 2))


def _attn_prep_kernel(q_ref, kv_ref, cos_ref, sin_ref, qg_ref, kg_ref, qo_ref, ko_ref, vo_ref):
    cos = cos_ref[...]
    sin = sin_ref[...]

    def norm_rope(x, g):
        w = x.shape[1]
        bd = (_iota((w, w), 0) // B_DH == _iota((w, w), 1) // B_DH).astype(bf16)
        ss = _dot((x * x).astype(bf16), bd)
        xn = x * lax.rsqrt(ss * (1.0 / B_DH) + EPS) * g
        first = (_iota(x.shape, 1) % (B_DH // 2)) < (B_DH // 4)
        swapped = jnp.where(first, pltpu.roll(xn, w - B_DH // 4, 1), pltpu.roll(xn, B_DH // 4, 1))
        reps = w // LANES
        return xn * jnp.tile(cos, (1, reps)) + swapped * jnp.tile(sin, (1, reps))

    q = (norm_rope(q_ref[...].astype(f32), qg_ref[...]) * (B_DH ** -0.5)).astype(bf16)
    for h in range(B_HEADS):
        qo_ref[h] = q[:, B_DH * h:B_DH * (h + 1)]
    kv = kv_ref[...]
    k = norm_rope(kv[:, :B_KV * B_DH].astype(f32), kg_ref[...]).astype(bf16)
    for h in range(B_KV):
        ko_ref[h] = k[:, B_DH * h:B_DH * (h + 1)]
        vo_ref[h] = kv[:, B_KV * B_DH + B_DH * h:B_KV * B_DH + B_DH * (h + 1)]


def _attn_prep(b_q, b_kv, cos, sin, qg, kg, tpb):
    t = b_q.shape[0]
    return pl.pallas_call(
        _attn_prep_kernel,
        out_shape=(SDS((B_HEADS, t, B_DH), bf16), SDS((B_KV, t, B_DH), bf16), SDS((B_KV, t, B_DH), bf16)),
        grid=(t // TM,),
        in_specs=[pl.BlockSpec((TM, 512), lambda i: (i, 0)), pl.BlockSpec((TM, 256), lambda i: (i, 0)),
                  pl.BlockSpec((TM, LANES), lambda i: (i % tpb, 0)), pl.BlockSpec((TM, LANES), lambda i: (i % tpb, 0)),
                  pl.BlockSpec((1, 512), lambda i: (0, 0)), pl.BlockSpec((1, LANES), lambda i: (0, 0))],
        out_specs=(pl.BlockSpec((B_HEADS, TM, B_DH), lambda i: (0, i, 0)),
                   pl.BlockSpec((B_KV, TM, B_DH), lambda i: (0, i, 0)),
                   pl.BlockSpec((B_KV, TM, B_DH), lambda i: (0, i, 0))),
        compiler_params=_cp(("arbitrary",)), name="attn_prep",
    )(b_q, b_kv, cos, sin, jnp.tile(qg, B_HEADS).reshape(1, -1), jnp.tile(kg, B_KV).reshape(1, -1))


def _attn_kernel(q_ref, k_ref, v_ref, o_ref, *, ctx_len):
    tb = k_ref.shape[1]
    q = q_ref[...].reshape(B_GROUP * TQ, B_DH)

    def attend(klen):
        starts = [0] + list(range(ctx_len, klen, KEY_CHUNK))
        m = l = acc = None
        for s0, s1 in zip(starts, starts[1:] + [klen]):
            s = _dot_nt(q, k_ref[0, s0:s1, :])
            smax = jnp.max(s, axis=-1, keepdims=True)
            if m is None:
                m = smax
                p = jnp.exp(s - m)
                l = jnp.sum(p, axis=-1, keepdims=True)
                acc = _dot(p.astype(bf16), v_ref[0, s0:s1, :])
            else:
                m_new = jnp.maximum(m, smax)
                alpha = jnp.exp(m - m_new)
                p = jnp.exp(s - m_new)
                l = alpha * l + jnp.sum(p, axis=-1, keepdims=True)
                acc = alpha * acc + _dot(p.astype(bf16), v_ref[0, s0:s1, :])
                m = m_new
        o_ref[...] = (acc / l).reshape(B_GROUP, TQ, B_DH).astype(bf16)

    is_ctx = pl.program_id(2) < ctx_len // TQ

    @pl.when(is_ctx)
    def _():
        attend(ctx_len)

    @pl.when(jnp.logical_not(is_ctx))
    def _():
        attend(tb)


def _attn(q, k, v, nb, ctx_len):
    t = q.shape[1]
    tb = t // nb
    nq = tb // TQ
    return pl.pallas_call(
        functools.partial(_attn_kernel, ctx_len=ctx_len),
        out_shape=SDS((B_HEADS, t, B_DH), bf16), grid=(nb, B_KV, nq),
        in_specs=[pl.BlockSpec((B_GROUP, TQ, B_DH), lambda b, g, i: (g, b * nq + i, 0)),
                  pl.BlockSpec((1, tb, B_DH), lambda b, g, i: (g, b, 0)),
                  pl.BlockSpec((1, tb, B_DH), lambda b, g, i: (g, b, 0))],
        out_specs=pl.BlockSpec((B_GROUP, TQ, B_DH), lambda b, g, i: (g, b * nq + i, 0)),
        compiler_params=_cp(("arbitrary", "arbitrary", "arbitrary")), name="attn",
    )(q, k, v)


def _head_rms(y, g, dv):
    parts = []
    for h in range(y.shape[1] // dv):
        yh = y[:, dv * h:dv * (h + 1)]
        parts.append(yh * lax.rsqrt(jnp.mean(yh * yh, axis=-1, keepdims=True) + EPS))
    return jnp.concatenate(parts, axis=1) * g


def _merge_kernel(ha_ref, ao_ref, att_ref, hc_ref, cr_ref, gate_ref, x_ref, g1_ref, sh2_ref, sc2_ref,
                  ag_ref, cg_ref, wb_ref, wo_ref, gn2_ref, wr_ref, br_ref,
                  xo_ref, h2_ref, ti_ref, tw_ref, rk_ref, cnt_ref, cnt_s):
    i = pl.program_id(0)

    @pl.when(i == 0)
    def _():
        cnt_s[...] = jnp.zeros_like(cnt_s)

    d = x_ref.shape[1]
    ya = _head_rms(ha_ref[...].astype(f32), ag_ref[...], A_DV) * _sigmoid(ao_ref[...].astype(f32))
    cr = cr_ref[...].astype(f32)
    yc = _head_rms(hc_ref[...].astype(f32), cg_ref[...], C_DV) * (cr * _sigmoid(cr))
    yb = jnp.concatenate([att_ref[h] for h in range(B_HEADS)], axis=1)
    merged = jnp.zeros((TM, d), f32)
    for n, y in enumerate((ya.astype(bf16), yb, yc.astype(bf16))):
        merged = merged + _sigmoid(gate_ref[:, d * n:d * (n + 1)].astype(f32)) * _dot(y, wb_ref[n])
    x = x_ref[...] + g1_ref[0] * _dot(merged.astype(bf16), wo_ref[...])
    xo_ref[...] = x
    h2 = _rms_mod(x, gn2_ref[...], sh2_ref[0], sc2_ref[0])
    h2_ref[...] = h2

    logits = _dot_nt(wr_ref[...], h2, HI) + br_ref[...]
    eid = _iota((N_EXPERTS, TM), 0)
    work = logits
    onehot = jnp.zeros((N_EXPERTS, TM), f32)
    vals, sels = [], []
    for k in range(TOP_K):
        mk = jnp.max(work, axis=0, keepdims=True)
        ik = jnp.min(jnp.where(work == mk, eid, N_EXPERTS), axis=0, keepdims=True)
        sel = eid == ik
        work = jnp.where(sel, -jnp.inf, work)
        onehot = onehot + sel.astype(f32)
        ti_ref[k:k + 1, :] = ik
        vals.append(mk)
        sels.append(sel)
    ex = [jnp.exp(vk - vals[0]) for vk in vals]
    tot = ex[0] + ex[1] + ex[2] + ex[3]
    for k in range(TOP_K):
        tw_ref[k:k + 1, :] = ex[k] / tot

    ut = (_iota((TM, TM), 0) <= _iota((TM, TM), 1)).astype(bf16)
    incl = _dot(onehot.astype(bf16), ut)
    rank = cnt_s[...][:, 0:1] + incl - onehot
    for k in range(TOP_K):
        rk_ref[k:k + 1, :] = jnp.sum(jnp.where(sels[k], rank, 0.0), axis=0, keepdims=True).astype(jnp.int32)
    cnt_s[...] = cnt_s[...] + incl[:, TM - 1:TM]
    cnt_ref[...] = cnt_s[...]


def _merge(p, h_a, att, h_c, x, mods, a_norm_g, c_norm_g, w_branch, w_out, g_norm2, w_router, b_router,
           tpb, ctx_tiles, nb):
    t, d = x.shape
    row = lambda w: pl.BlockSpec((TM, w), lambda i: (i, 0))
    const = lambda shape: pl.BlockSpec(shape, lambda i: tuple(0 for _ in shape))
    return pl.pallas_call(
        _merge_kernel,
        out_shape=(SDS((t, d), f32), SDS((t, d), f32), SDS((TOP_K, t), jnp.int32), SDS((TOP_K, t), f32),
                   SDS((TOP_K, t), jnp.int32), SDS((N_EXPERTS, LANES), f32)),
        grid=(t // TM,),
        in_specs=[row(512), row(512), pl.BlockSpec((B_HEADS, TM, B_DH), lambda i: (0, i, 0)), row(512), row(512),
                  row(3 * d), row(d),
                  _mod_spec(2, tpb, ctx_tiles, nb), _mod_spec(3, tpb, ctx_tiles, nb), _mod_spec(4, tpb, ctx_tiles, nb),
                  const((1, 512)), const((1, 512)), const((3, MIX_W, d)), const((d, d)), const((1, d)),
                  const((N_EXPERTS, d)), const((N_EXPERTS, 1))],
        out_specs=(row(d), row(d), pl.BlockSpec((TOP_K, TM), lambda i: (0, i)), pl.BlockSpec((TOP_K, TM), lambda i: (0, i)),
                   pl.BlockSpec((TOP_K, TM), lambda i: (0, i)), const((N_EXPERTS, LANES))),
        scratch_shapes=[pltpu.VMEM((N_EXPERTS, LANES), f32)],
        compiler_params=_cp(("arbitrary",)), name="merge",
    )(h_a, p["a_o"], att, h_c, p["c_r"], p["gate"], x, mods, mods, mods,
      a_norm_g.reshape(1, -1), c_norm_g.reshape(1, -1), w_branch, w_out, g_norm2.reshape(1, -1),
      w_router.T, b_router.reshape(-1, 1))


def _sc_mesh():
    return plsc.VectorSubcoreMesh(core_axis_name="c", subcore_axis_name="s")


def _sc_scatter_rows(src, idx, n_out):
    v, d = src.shape
    n = idx.shape[0]
    per_w = n // (SC_NUM_CORES * SC_NUM_SUBCORES)
    assert per_w % SC_WINDOW == 0 and v % SC_WINDOW == 0

    @functools.partial(pl.kernel, out_type=SDS((n_out, d), src.dtype), mesh=_sc_mesh(),
                       scratch_types=[pltpu.VMEM((SC_WINDOW,), jnp.int32), pltpu.VMEM((SC_WINDOW, d), src.dtype),
                                      pltpu.SemaphoreType.DMA])
    def k(x_hbm, i_hbm, o_hbm, idx_v, rows_v, sem):
        wid = lax.axis_index("s") * SC_NUM_CORES + lax.axis_index("c")

        @pl.loop(0, per_w // SC_WINDOW)
        def _(j):
            base = wid * per_w + j * SC_WINDOW
            pltpu.sync_copy(i_hbm.at[pl.ds(base, SC_WINDOW)], idx_v)
            pltpu.sync_copy(x_hbm.at[pl.ds(lax.rem(base, v), SC_WINDOW)], rows_v)
            pltpu.async_copy(rows_v, o_hbm.at[idx_v], sem).wait()

    return k(src, idx)


def _sc_gather_rows(table, idx):
    d = table.shape[1]
    n = idx.shape[0]
    per_w = n // (SC_NUM_CORES * SC_NUM_SUBCORES)
    assert per_w % SC_WINDOW == 0

    @functools.partial(pl.kernel, out_type=SDS((n, d), table.dtype), mesh=_sc_mesh(),
                       scratch_types=[pltpu.VMEM((SC_WINDOW,), jnp.int32), pltpu.VMEM((SC_WINDOW, d), table.dtype),
                                      pltpu.SemaphoreType.DMA])
    def k(x_hbm, i_hbm, o_hbm, idx_v, rows_v, sem):
        wid = lax.axis_index("s") * SC_NUM_CORES + lax.axis_index("c")

        @pl.loop(0, per_w // SC_WINDOW)
        def _(j):
            base = wid * per_w + j * SC_WINDOW
            pltpu.sync_copy(i_hbm.at[pl.ds(base, SC_WINDOW)], idx_v)
            pltpu.async_copy(x_hbm.at[idx_v], rows_v, sem).wait()
            pltpu.sync_copy(rows_v, o_hbm.at[pl.ds(base, SC_WINDOW)])

    return k(table, idx)


GU_BLOCK = 2 * LANES


def _deinterleave_perm():
    n = np.arange(GU_BLOCK)
    src = np.where(n < LANES, 2 * n, 2 * (n - LANES) + 1)
    return jnp.asarray(np.arange(GU_BLOCK)[:, None] == src[None, :], bf16)


def _expert_kernel(te_ref, nv_ref, x_ref, w1_ref, b1_ref, w2_ref, b2_ref, perm_ref, y_ref, w1_s, w2_s):
    i = pl.program_id(0)
    valid = i < nv_ref[0]
    new_expert = jnp.logical_or(i == 0, te_ref[i] != te_ref[jnp.maximum(i - 1, 0)])
    dff2 = w1_ref.shape[2]

    @pl.when(jnp.logical_and(valid, new_expert))
    def _():
        for cb in range(dff2 // GU_BLOCK):
            cs = slice(GU_BLOCK * cb, GU_BLOCK * (cb + 1))
            w1_s[:, cs] = _dot(w1_ref[0, :, cs].astype(bf16), perm_ref[...]).astype(bf16)
        w2_s[...] = w2_ref[0].astype(bf16)

    @pl.when(valid)
    def _():
        x = x_ref[...].astype(bf16)
        gu = _dot(x, w1_s[...]) + b1_ref[0]
        nblk = dff2 // GU_BLOCK
        g = jnp.concatenate([gu[:, GU_BLOCK * cb:GU_BLOCK * cb + LANES] for cb in range(nblk)], axis=1)
        u = jnp.concatenate([gu[:, GU_BLOCK * cb + LANES:GU_BLOCK * (cb + 1)] for cb in range(nblk)], axis=1)
        gate = jnp.minimum(g, SWIGLU_LIMIT)
        up = jnp.clip(u, -SWIGLU_LIMIT, SWIGLU_LIMIT)
        a = (up + 1.0) * gate * _sigmoid(SWIGLU_ALPHA * gate)
        y_ref[...] = _dot(a.astype(bf16), w2_s[...]) + b2_ref[0]


def _experts(xs, tile_e, n_valid, layer, w1, b1, w2, b2):
    p, d = xs.shape
    dff2 = w1.shape[-1]
    dff = w2.shape[2]
    row = lambda i, te, nv: (jnp.minimum(i, nv[0] - 1), 0)
    wsel = lambda i, te, nv: (layer, te[i], 0, 0)
    sq = pl.Squeezed()
    return pl.pallas_call(
        _expert_kernel, out_shape=SDS((p, d), f32),
        grid_spec=pltpu.PrefetchScalarGridSpec(
            num_scalar_prefetch=2, grid=(p // TME,),
            in_specs=[pl.BlockSpec((TME, d), row),
                      pl.BlockSpec((sq, 1, d, dff2), wsel), pl.BlockSpec((sq, 1, 1, dff2), wsel),
                      pl.BlockSpec((sq, 1, dff, d), wsel), pl.BlockSpec((sq, 1, 1, d), wsel),
                      pl.BlockSpec((GU_BLOCK, GU_BLOCK), lambda i, te, nv: (0, 0))],
            out_specs=pl.BlockSpec((TME, d), row),
            scratch_shapes=[pltpu.VMEM((d, dff2), bf16), pltpu.VMEM((dff, d), bf16)]),
        compiler_params=_cp(("arbitrary",)), name="experts",
    )(tile_e, n_valid, xs, w1, b1, w2, b2, _deinterleave_perm())


def _combine_kernel(x_ref, yg_ref, w_ref, g2_ref, *rest, final):
    w = w_ref[...]
    acc = w[:, 0:1] * yg_ref[0]
    for k in range(1, TOP_K):
        acc = acc + w[:, k:k + 1] * yg_ref[k]
    x = x_ref[...] + g2_ref[0] * acc
    if final:
        gf_ref, o_ref = rest
        x = x * lax.rsqrt(jnp.mean(x * x, axis=-1, keepdims=True) + EPS) * gf_ref[...]
    else:
        (o_ref,) = rest
    o_ref[...] = x


def _combine(x, yg, wcol, mods, tpb, ctx_tiles, nb, g_final=None):
    t, d = x.shape
    final = g_final is not None
    if final:
        lat = tpb - ctx_tiles
        rmap = lambda i: ((i // lat) * tpb + ctx_tiles + i % lat)
        grid = (nb * lat,)
        mod = pl.BlockSpec((1, 1, 1024), lambda i: (i // lat, 0, 5))
        n_out = nb * lat * TM
    else:
        rmap = lambda i: i
        grid = (t // TM,)
        mod = _mod_spec(5, tpb, ctx_tiles, nb)
        n_out = t
    in_specs = [pl.BlockSpec((TM, d), lambda i: (rmap(i), 0)),
                pl.BlockSpec((TOP_K, TM, d), lambda i: (0, rmap(i), 0)),
                pl.BlockSpec((TM, TOP_K), lambda i: (rmap(i), 0)), mod]
    args = [x, yg, wcol, mods]
    if final:
        in_specs.append(pl.BlockSpec((1, d), lambda i: (0, 0)))
        args.append(g_final.reshape(1, -1))
    return pl.pallas_call(
        functools.partial(_combine_kernel, final=final), out_shape=SDS((n_out, d), f32), grid=grid,
        in_specs=in_specs, out_specs=pl.BlockSpec((TM, d), lambda i: (i, 0)),
        compiler_params=_cp(("arbitrary",)), name="combine_final" if final else "combine",
    )(*args)


def _routing_tables(top_i, rank, counts, n_tiles):
    cnt = counts[:, 0].astype(jnp.int32)
    padded = ((cnt + TME - 1) // TME) * TME
    ends = jnp.cumsum(padded)
    starts = ends - padded
    eids = jnp.arange(N_EXPERTS, dtype=jnp.int32)
    start_of = jnp.sum(jnp.where(top_i[..., None] == eids, starts, 0), axis=-1)
    pos = (start_of + rank).reshape(-1)
    n_valid = ends[-1] // TME
    tile_start = jnp.arange(n_tiles, dtype=jnp.int32) * TME
    tile_e = jnp.sum(tile_start[:, None] >= ends[None, :], axis=1).astype(jnp.int32)
    tile_e = jnp.where(jnp.arange(n_tiles) < n_valid, tile_e, tile_e[jnp.maximum(n_valid - 1, 0)])
    tile_e = jnp.minimum(tile_e, N_EXPERTS - 1)
    return pos, tile_e, n_valid.reshape(1).astype(jnp.int32)


def kernel(x, c, ctx, c_ctx, w_ada, b_ada, g_norm1, w_in, a_conv_w, a_conv_b, a_gate_b, a_norm_g, b_q_norm_g, b_k_norm_g, c_w_up, c_b_up, c_norm_g, w_branch, w_out, g_norm2, w_router, b_router, w_e1, b_e1, w_e2, b_e2, g_final):
    nb, seq, d = x.shape
    ctx_len = ctx.shape[1]
    depth = w_ada.shape[0]
    tb = ctx_len + seq
    t = nb * tb
    tpb, ctx_tiles = tb // TM, ctx_len // TM
    assert d == 1024 and nb < 16 and seq % TM == 0 and ctx_len % TM == 0 and ctx_len % TQ == 0
    n_assign = TOP_K * t
    n_sorted = n_assign + N_EXPERTS * TME
    n_tiles = n_sorted // TME

    xs = jnp.concatenate([ctx, x], axis=1).reshape(t, d)
    cc = jnp.zeros((16, d), f32).at[:nb].set(c).at[nb].set(c_ctx)
    mods_all = _adaln(cc, w_ada, b_ada)
    cos, sin = _rope_tables(seq, ctx_len)
    w_in_r = _arrange_w_in(w_in)
    w_branch_b, w_out_b = w_branch.astype(bf16), w_out.astype(bf16)
    col = np.arange(b_e1.shape[-1])
    within = col % GU_BLOCK
    src = (col // GU_BLOCK) * GU_BLOCK + np.where(within < LANES, 2 * within, 2 * (within - LANES) + 1)
    b1 = b_e1[..., src][..., None, :]
    b2 = b_e2[..., None, :]

    out = None
    for l in range(depth):
        mods = mods_all[l].reshape(16, 1, 6 * d)
        p = _inproj(xs, mods, g_norm1[l].reshape(1, -1), w_in_r[l], tpb, ctx_tiles, nb)
        h_a = _mlstm(p["a_qk"], p["a_v"], p["small"], a_conv_w[l], a_conv_b[l], a_gate_b[l], nb, ctx_len)
        h_c = _gla(p["c_qk"], p["c_v"], p["small"], c_w_up[l], c_b_up[l], nb, ctx_len)
        qn, kn, vn = _attn_prep(p["b_q"], p["b_kv"], cos, sin, b_q_norm_g[l], b_k_norm_g[l], tpb)
        att = _attn(qn, kn, vn, nb, ctx_len)
        xs, h2, top_i, top_w, rank, counts = _merge(
            p, h_a, att, h_c, xs, mods, a_norm_g[l], c_norm_g[l], w_branch_b[l], w_out_b[l], g_norm2[l],
            w_router[l], b_router[l], tpb, ctx_tiles, nb)
        pos, tile_e, n_valid = _routing_tables(top_i, rank, counts, n_tiles)
        x_sorted = _sc_scatter_rows(h2, pos, n_sorted)
        y_sorted = _experts(x_sorted, tile_e, n_valid, l, w_e1, b1, w_e2, b2)
        yg = _sc_gather_rows(y_sorted, pos).reshape(TOP_K, t, d)
        if l == depth - 1:
            out = _combine(xs, yg, top_w.T, mods, tpb, ctx_tiles, nb, g_final=g_final)
        else:
            xs = _combine(xs, yg, top_w.T, mods, tpb, ctx_tiles, nb)
    return out.reshape(nb, seq, d)
```

```python
import functools

import jax
import jax.numpy as jnp
import numpy as np
from jax import lax
from jax.experimental import pallas as pl
from jax.experimental.pallas import tpu as pltpu
from jax.experimental.pallas import tpu_sc as plsc

f32 = jnp.float32
bf16 = jnp.bfloat16
HI = lax.Precision.HIGHEST
SDS = jax.ShapeDtypeStruct

EPS = 1e-6
CHUNK = 64
GRID_W = 64
ROPE_THETA = 10000.0
MIX_W = 512
A_HEADS, A_DQK, A_DV = 4, 64, 128
B_HEADS, B_KV, B_DH = 8, 2, 64
B_GROUP = B_HEADS // B_KV
C_HEADS, C_DK, C_DV, C_RANK, C_TAU = 4, 64, 128, 16, 16.0
N_EXPERTS, TOP_K = 32, 4
SWIGLU_LIMIT, SWIGLU_ALPHA = 7.0, 1.702

LANES = 128
SC_NUM_CORES = 2
SC_NUM_SUBCORES = 16
VMEM_LIMIT = 56 * 2 ** 20

TM = 256
TQ = 128
TME = 512
KEY_CHUNK = 2048
SC_WINDOW = 64


def _cp(sem, vmem=VMEM_LIMIT):
    return pltpu.CompilerParams(dimension_semantics=sem, vmem_limit_bytes=vmem)


def _dot(a, b, precision=None):
    return jnp.dot(a, b, preferred_element_type=f32, precision=precision)


def _dot_nt(a, b, precision=None):
    return lax.dot_general(a, b, (((1,), (1,)), ((), ())), preferred_element_type=f32, precision=precision)


def _dot_tn(a, b):
    return lax.dot_general(a, b, (((0,), (0,)), ((), ())), preferred_element_type=f32)


def _sigmoid(x):
    return 1.0 / (1.0 + jnp.exp(-x))


def _log_sigmoid(x):
    return jnp.minimum(x, 0.0) - jnp.log(1.0 + jnp.exp(-jnp.abs(x)))


def _iota(shape, axis):
    return lax.broadcasted_iota(jnp.int32, shape, axis)


def _pack_bf16_pairs(x):
    n = x.shape[1] // 2
    lo = pltpu.bitcast(x[:, :n].astype(bf16).astype(f32), jnp.uint32)
    hi = pltpu.bitcast(x[:, n:].astype(bf16).astype(f32), jnp.uint32)
    return pltpu.bitcast((lo >> 16) | hi, jnp.int32)


def _unpack_bf16_pairs(w):
    u = pltpu.bitcast(w, jnp.uint32)
    return pltpu.bitcast(u << 16, f32), pltpu.bitcast(u & jnp.uint32(0xFFFF0000), f32)


def _rms_mod(x, g, shift, scale):
    y = x * lax.rsqrt(jnp.mean(x * x, axis=-1, keepdims=True) + EPS) * g
    return y * (1.0 + scale) + shift


def _adaln_kernel(cc_ref, w_ref, b_ref, o_ref):
    cc = cc_ref[...]
    s = cc * _sigmoid(cc)
    o_ref[0] = _dot(s, w_ref[0], HI) + b_ref[0]


def _adaln(cc, w_ada, b_ada):
    nl, d, n = w_ada.shape
    return pl.pallas_call(
        _adaln_kernel, out_shape=SDS((nl, 16, n), f32), grid=(nl, n // 1024),
        in_specs=[pl.BlockSpec((16, d), lambda l, j: (0, 0)),
                  pl.BlockSpec((1, d, 1024), lambda l, j: (l, 0, j)),
                  pl.BlockSpec((1, 1, 1024), lambda l, j: (l, 0, j))],
        out_specs=pl.BlockSpec((1, 16, 1024), lambda l, j: (l, 0, j)),
        compiler_params=_cp(("arbitrary", "arbitrary")), name="adaln",
    )(cc, w_ada, b_ada.reshape(nl, 1, n))


def _mod_spec(col, tpb, ctx_tiles, nb):
    def imap(i):
        return (jnp.where((i % tpb) >= ctx_tiles, i // tpb, nb), 0, col)
    return pl.BlockSpec((1, 1, 1024), imap)


_K1_GROUPS = (("a_qk", 512, bf16), ("a_v", 512, bf16), ("a_o", 512, bf16), ("b_q", 512, bf16),
              ("b_kv", 256, bf16), ("c_qk", 512, bf16), ("c_v", 512, bf16), ("c_r", 512, bf16),
              ("gate", 3072, bf16), ("small", 128, f32))
_K1_WIDTH = sum(w for _, w, _ in _K1_GROUPS)


def _arrange_w_in(w_in):
    z = jnp.zeros(w_in.shape[:-1] + (LANES - 48,), w_in.dtype)
    w = jnp.concatenate([w_in[..., 0:1024], w_in[..., 1040:1552], w_in[..., 1552:2320], w_in[..., 2320:3344],
                         w_in[..., 3376:3888], w_in[..., 3888:6960], w_in[..., 1024:1040], w_in[..., 3344:3376], z],
                        axis=-1)
    assert w.shape[-1] == _K1_WIDTH
    return w.astype(bf16)


def _inproj_kernel(x_ref, sh_ref, sc_ref, g_ref, w_ref, *out_refs):
    h = _rms_mod(x_ref[...], g_ref[...], sh_ref[0], sc_ref[0]).astype(bf16)
    c0 = 0
    for (_, width, dt), o_ref in zip(_K1_GROUPS, out_refs):
        for j in range(0, width, 512):
            wj = min(512, width - j)
            o_ref[:, j:j + wj] = _dot(h, w_ref[:, c0 + j:c0 + j + wj]).astype(dt)
        c0 += width


def _inproj(x, mods, g, w, tpb, ctx_tiles, nb):
    t, d = x.shape
    outs = tuple(SDS((t, wd), dt) for _, wd, dt in _K1_GROUPS)
    res = pl.pallas_call(
        _inproj_kernel, out_shape=outs, grid=(t // TM,),
        in_specs=[pl.BlockSpec((TM, d), lambda i: (i, 0)),
                  _mod_spec(0, tpb, ctx_tiles, nb), _mod_spec(1, tpb, ctx_tiles, nb),
                  pl.BlockSpec((1, d), lambda i: (0, 0)),
                  pl.BlockSpec((d, _K1_WIDTH), lambda i: (0, 0), pipeline_mode=pl.Buffered(1))],
        out_specs=tuple(pl.BlockSpec((TM, wd), lambda i: (i, 0)) for _, wd, _ in _K1_GROUPS),
        compiler_params=_cp(("arbitrary",)), name="inproj",
    )(x, mods, mods, g, w)
    return dict(zip([n for n, _, _ in _K1_GROUPS], res))


def _bwd_chunk(i, ncc, nch):
    return jnp.where(i < ncc, ncc - 1 - i, nch - 1 + ncc - i)


def _mlstm_kernel(qk_ref, v_ref, sm_ref, cw_ref, cb_ref, gb_ref, o_ref, qk_s, hacc, c_s, *, ctx_len):
    tb = qk_ref.shape[0]
    nch = tb // CHUNK
    ncc = ctx_len // CHUNK
    nu = 2 * A_HEADS

    cw = cw_ref[...]
    cb = cb_ref[...]
    row = _iota((CHUNK, 2 * A_HEADS * A_DQK), 0)
    kscale = jnp.where(_iota((1, 2 * A_HEADS * A_DQK), 1) >= A_HEADS * A_DQK, A_DQK ** -0.5, 1.0)

    def conv_body(c, carry):
        r0 = pl.multiple_of(c * CHUNK, CHUNK)
        x = qk_ref[pl.ds(r0, CHUNK), :].astype(f32)
        rp = pl.multiple_of(jnp.maximum(r0 - 16, 0), 16)
        rn = pl.multiple_of(jnp.minimum(r0 + CHUNK, tb - 16), 16)
        prev = qk_ref[pl.ds(rp, 16), :].astype(f32)[15:16, :]
        nxt = qk_ref[pl.ds(rn, 16), :].astype(f32)[0:1, :]
        has_prev = jnp.logical_and(r0 != 0, r0 != ctx_len)
        has_next = jnp.logical_and(r0 + CHUNK != ctx_len, r0 + CHUNK != tb)
        prev = jnp.where(has_prev, prev, 0.0)
        nxt = jnp.where(has_next, nxt, 0.0)
        xm = jnp.where(row == 0, prev, pltpu.roll(x, 1, 0))
        xp = jnp.where(row == CHUNK - 1, nxt, pltpu.roll(x, CHUNK - 1, 0))
        y = cw[0:1] * xm + cw[1:2] * x + cw[2:3] * xp + cb
        y = y * _sigmoid(y) * kscale
        qk_s[pl.ds(r0, CHUNK), :] = y.astype(bf16)
        return carry

    lax.fori_loop(0, nch, conv_body, 0)

    hacc[...] = jnp.zeros_like(hacc)
    c_s[...] = jnp.zeros_like(c_s)

    rr = _iota((CHUNK, CHUNK), 0)
    cc = _iota((CHUNK, CHUNK), 1)
    tril = (cc <= rr)
    triu = (cc >= rr)
    tril_f = tril.astype(f32)
    triu_f = triu.astype(f32)
    lane = _iota((CHUNK, LANES), 1)
    is_f = jnp.logical_and(lane >= nu, lane < 2 * nu)
    ones_blk = (_iota((CHUNK, A_DV), 1) == 0).astype(bf16)
    gb = gb_ref[...]
    neg_inf = jnp.float32(-jnp.inf)

    def body(i, ms):
        ms = list(ms)
        for d in range(2):
            c = i if d == 0 else _bwd_chunk(i, ncc, nch)
            r0 = pl.multiple_of(c * CHUNK, CHUNK)
            qk = qk_s[pl.ds(r0, CHUNK), :]
            v = v_ref[pl.ds(r0, CHUNK), :]
            sm = sm_ref[pl.ds(r0, CHUNK), :] + gb
            pre = jnp.where(is_f, _log_sigmoid(sm), sm)
            pre_t = pre.T
            t_col, t_row = (tril_f, triu_f) if d == 0 else (triu_f, tril_f)
            bcol = _dot(t_col, pre, HI)
            brow = _dot(pre_t, t_row, HI)
            mask = tril if d == 0 else triu
            last = CHUNK - 1 if d == 0 else 0
            houts, new_c = [], []
            c_all = c_s[d]
            for h in range(A_HEADS):
                u = d * A_HEADS + h
                ci, cf = u, nu + u
                b_c, i_c = bcol[:, cf:cf + 1], pre[:, ci:ci + 1]
                b_r, i_r = brow[cf:cf + 1, :], pre_t[ci:ci + 1, :]
                log_d = jnp.where(mask, b_c - b_r + i_r, neg_inf)
                rmax = jnp.max(log_d, axis=1, keepdims=True)
                pm = jnp.exp(log_d - rmax)
                qh = qk[:, A_DQK * h:A_DQK * (h + 1)]
                kh = qk[:, A_HEADS * A_DQK + A_DQK * h:A_HEADS * A_DQK + A_DQK * (h + 1)]
                s = _dot_nt(qh, kh) * pm
                vaug = jnp.concatenate([v[:, A_DV * h:A_DV * (h + 1)], ones_blk], axis=1)
                p1 = _dot(s.astype(bf16), vaug)
                m = ms[u]
                li = b_c + m
                mj = jnp.maximum(li, rmax)
                cst = c_all[h]
                qc = _dot(qh, cst.astype(bf16))
                na = jnp.exp(rmax - mj) * p1 + jnp.exp(li - mj) * qc
                den = na[:, A_DV:A_DV + 1]
                houts.append(na[:, :A_DV] / jnp.maximum(jnp.abs(den), jnp.exp(-mj)))
                b_last, m_new, mloc = b_c[last:last + 1], mj[last:last + 1], rmax[last:last + 1]
                wl = jnp.exp(b_last - b_c + i_c - mloc)
                upd = _dot_tn((kh.astype(f32) * wl).astype(bf16), vaug)
                new_c.append(jnp.exp(b_last + m - m_new) * cst + jnp.exp(mloc - m_new) * upd)
                ms[u] = m_new
            hacc[pl.ds(r0, CHUNK), :] += jnp.concatenate(houts, axis=1)
            c_s[d] = jnp.stack(new_c)
        return tuple(ms)

    lax.fori_loop(0, nch, body, tuple(jnp.zeros((1, 1), f32) for _ in range(nu)))
    o_ref[...] = hacc[...].astype(bf16)


def _mlstm(a_qk, a_v, small, conv_w, conv_b, gate_b, nb, ctx_len):
    t = a_qk.shape[0]
    tb = t // nb
    gb = jnp.zeros((1, LANES), f32).at[0, :4 * A_HEADS].set(gate_b.reshape(-1))
    return pl.pallas_call(
        functools.partial(_mlstm_kernel, ctx_len=ctx_len),
        out_shape=SDS((t, MIX_W), bf16), grid=(nb,),
        in_specs=[pl.BlockSpec((tb, 512), lambda b: (b, 0)), pl.BlockSpec((tb, 512), lambda b: (b, 0)),
                  pl.BlockSpec((tb, LANES), lambda b: (b, 0)),
                  pl.BlockSpec((3, 512), lambda b: (0, 0)), pl.BlockSpec((1, 512), lambda b: (0, 0)),
                  pl.BlockSpec((1, LANES), lambda b: (0, 0))],
        out_specs=pl.BlockSpec((tb, MIX_W), lambda b: (b, 0)),
        scratch_shapes=[pltpu.VMEM((tb, 512), bf16), pltpu.VMEM((tb, MIX_W), f32),
                        pltpu.VMEM((2, A_HEADS, A_DQK, 2 * A_DV), f32)],
        compiler_params=_cp(("arbitrary",)), name="mlstm",
    )(a_qk, a_v, small, conv_w, conv_b.reshape(1, -1), gb)


def _split3(x):
    hi = x.astype(bf16)
    r1 = x - hi.astype(f32)
    mid = r1.astype(bf16)
    lo = (r1 - mid.astype(f32)).astype(bf16)
    return jnp.concatenate([hi, mid, lo], axis=0)


GLA_LEVELS = 6


def _gla_constants():
    t = np.arange(CHUNK)
    tri, gref, lmask = [], [], []
    for d in range(2):
        pos = t if d == 0 else CHUNK - 1 - t
        row_of = np.argsort(pos)
        tri.append(pos[None, :] <= pos[:, None])
        g, m = [], []
        for lvl in range(GLA_LEVELS):
            half = CHUNK >> (lvl + 1)
            ref_pos = (pos // (2 * half)) * (2 * half) + half
            g.append(t[None, :] == row_of[ref_pos][:, None])
            late, early = (pos % (2 * half)) >= half, (pos % (2 * half)) < half
            same = (pos[:, None] // (2 * half)) == (pos[None, :] // (2 * half))
            m.append(same & late[:, None] & early[None, :])
        m.append(t[:, None] == t[None, :])
        gref.append(np.concatenate(g, axis=0))
        lmask.append(np.stack([np.tile(x, (1, C_HEADS)) for x in m]))
    tri3 = np.stack([np.tile(x, (1, 3)) for x in tri])
    gref3 = np.stack([np.tile(x, (1, 3)) for x in gref])
    hs = np.arange(C_HEADS * CHUNK)
    kmask = (hs[:, None] // CHUNK) == (np.arange(C_HEADS * C_DK)[None, :] // C_DK)
    vmask = (hs[:, None] // CHUNK) == (np.arange(C_HEADS * C_DV)[None, :] // C_DV)
    bdm = (np.arange(2 * C_DV)[:, None] // C_DV) == (np.arange(2 * C_DK)[None, :] // C_DK)
    as_b = lambda x: jnp.asarray(x, bf16)
    return as_b(tri3), as_b(gref3), jnp.asarray(np.stack(lmask), f32), as_b(kmask), as_b(vmask), jnp.asarray(bdm, f32)


def _gla_kernel(qk_ref, v_ref, sm_ref, wup_ref, bup_ref, tri_ref, gref_ref, lmask_ref, kmask_ref, vmask_ref, bdm_ref,
                o_ref, hacc, s_s, *, ctx_len):
    tb = qk_ref.shape[0]
    nch = tb // CHUNK
    ncc = ctx_len // CHUNK
    nk = C_HEADS * C_DK
    pw = 2 * C_DK
    pv = 2 * C_DV
    npair = C_HEADS // 2

    hacc[...] = jnp.zeros_like(hacc)
    s_s[...] = jnp.zeros_like(s_s)
    qscale = C_DK ** -0.5

    def body(i, carry):
        for d in range(2):
            c = i if d == 0 else _bwd_chunk(i, ncc, nch)
            r0 = pl.multiple_of(c * CHUNK, CHUNK)
            sm = sm_ref[pl.ds(r0, CHUNK), :]
            la = _log_sigmoid(_dot(sm, wup_ref[d], HI) + bup_ref[d]) * (1.0 / C_TAU)
            qk = qk_ref[pl.ds(r0, CHUNK), :]
            q = qk[:, :nk].astype(f32) * qscale
            k = qk[:, nk:].astype(f32)
            v = v_ref[pl.ds(r0, CHUNK), :]
            b = _dot(tri_ref[d], _split3(la))
            bref = _dot(gref_ref[d], _split3(b))

            def scores(qe, ke):
                kst = jnp.tile(ke.astype(bf16), (C_HEADS, 1)) * kmask_ref[...]
                return _dot_nt(qe.astype(bf16), kst)

            a = scores(q, k) * lmask_ref[d, GLA_LEVELS]
            for lvl in range(GLA_LEVELS):
                rel = b - bref[CHUNK * lvl:CHUNK * (lvl + 1)]
                s = scores(q * jnp.exp(jnp.minimum(rel, 0.0)), k * jnp.exp(jnp.minimum(-rel, 0.0)))
                a = a + s * lmask_ref[d, lvl]
            vst = jnp.tile(v, (C_HEADS, 1)) * vmask_ref[...]
            o = _dot(a.astype(bf16), vst)

            last = CHUNK - 1 if d == 0 else 0
            bl = b[last:last + 1]
            qd = (q * jnp.exp(b)).astype(bf16)
            kd = (k * jnp.exp(bl - b)).astype(bf16)
            dec = jnp.exp(bl)
            o_int = []
            for p in range(npair):
                u = d * npair + p
                st = s_s[u]
                o_int.append(_dot_nt(qd[:, pw * p:pw * (p + 1)], st.astype(bf16)))
                upd = _dot_tn(v[:, pv * p:pv * (p + 1)], kd[:, pw * p:pw * (p + 1)])
                s_s[u] = st * dec[:, pw * p:pw * (p + 1)] + bdm_ref[...] * upd
            hacc[pl.ds(r0, CHUNK), :] += o + jnp.concatenate(o_int, axis=1)
        return carry

    lax.fori_loop(0, nch, body, 0)
    o_ref[...] = hacc[...].astype(bf16)


def _gla(c_qk, c_v, small, w_up, b_up, nb, ctx_len):
    t = c_qk.shape[0]
    tb = t // nb
    nk = C_HEADS * C_DK
    wz = jnp.zeros((2, LANES, nk), f32)
    wz = wz.at[0, 16:16 + C_RANK].set(w_up[0]).at[1, 16 + C_RANK:16 + 2 * C_RANK].set(w_up[1])
    consts = _gla_constants()
    const_spec = lambda a: pl.BlockSpec(a.shape, lambda b, n=a.ndim: (0,) * n)
    return pl.pallas_call(
        functools.partial(_gla_kernel, ctx_len=ctx_len),
        out_shape=SDS((t, MIX_W), bf16), grid=(nb,),
        in_specs=[pl.BlockSpec((tb, 512), lambda b: (b, 0)), pl.BlockSpec((tb, 512), lambda b: (b, 0)),
                  pl.BlockSpec((tb, LANES), lambda b: (b, 0)),
                  pl.BlockSpec((2, LANES, nk), lambda b: (0, 0, 0)), pl.BlockSpec((2, 1, nk), lambda b: (0, 0, 0))]
                 + [const_spec(a) for a in consts],
        out_specs=pl.BlockSpec((tb, MIX_W), lambda b: (b, 0)),
        scratch_shapes=[pltpu.VMEM((tb, MIX_W), f32), pltpu.VMEM((C_HEADS, 2 * C_DV, 2 * C_DK), f32)],
        compiler_params=_cp(("arbitrary",)), name="gla",
    )(c_qk, c_v, small, wz, b_up.reshape(2, 1, nk), *consts)


def _rope_tables(seq, ctx_len):
    n_f = B_DH // 4
    t = np.arange(seq)
    freqs = ROPE_THETA ** (-np.arange(n_f, dtype=np.float32) / n_f)
    hd = np.arange(B_DH)
    pos = np.where(hd[None, :] < B_DH // 2, (t // GRID_W)[:, None], (t % GRID_W)[:, None]).astype(np.float32)
    ang = jnp.asarray(pos * freqs[hd % n_f][None, :], f32)
    sign = np.where((hd % (2 * n_f)) < n_f, -1.0, 1.0).astype(np.float32)
    cos = jnp.concatenate([jnp.ones((ctx_len, B_DH), f32), jnp.cos(ang)], axis=0)
    sin = jnp.concatenate([jnp.zeros((ctx_len, B_DH), f32), jnp.sin(ang) * sign[None, :]], axis=0)
    return jnp.tile(cos, (1, 2)), jnp.tile(sin, (1, 2))


def _attn_prep_kernel(q_ref, kv_ref, cos_ref, sin_ref, qg_ref, kg_ref, qo_ref, ko_ref, vo_ref):
    cos = cos_ref[...]
    sin = sin_ref[...]

    def norm_rope(x, g):
        w = x.shape[1]
        bd = (_iota((w, w), 0) // B_DH == _iota((w, w), 1) // B_DH).astype(bf16)
        ss = _dot((x * x).astype(bf16), bd)
        xn = x * lax.rsqrt(ss * (1.0 / B_DH) + EPS) * g
        first = (_iota(x.shape, 1) % (B_DH // 2)) < (B_DH // 4)
        swapped = jnp.where(first, pltpu.roll(xn, w - B_DH // 4, 1), pltpu.roll(xn, B_DH // 4, 1))
        reps = w // LANES
        return xn * jnp.tile(cos, (1, reps)) + swapped * jnp.tile(sin, (1, reps))

    q = (norm_rope(q_ref[...].astype(f32), qg_ref[...]) * (B_DH ** -0.5)).astype(bf16)
    for h in range(B_HEADS):
        qo_ref[h] = q[:, B_DH * h:B_DH * (h + 1)]
    kv = kv_ref[...]
    k = norm_rope(kv[:, :B_KV * B_DH].astype(f32), kg_ref[...]).astype(bf16)
    ones_col = (_iota((kv.shape[0], LANES - B_DH), 1) == 0).astype(bf16)
    for h in range(B_KV):
        ko_ref[h] = k[:, B_DH * h:B_DH * (h + 1)]
        vh = kv[:, B_KV * B_DH + B_DH * h:B_KV * B_DH + B_DH * (h + 1)]
        vo_ref[h] = jnp.concatenate([vh, ones_col], axis=1)


def _attn_prep(b_q, b_kv, cos, sin, qg, kg, tpb):
    t = b_q.shape[0]
    return pl.pallas_call(
        _attn_prep_kernel,
        out_shape=(SDS((B_HEADS, t, B_DH), bf16), SDS((B_KV, t, B_DH), bf16), SDS((B_KV, t, LANES), bf16)),
        grid=(t // TM,),
        in_specs=[pl.BlockSpec((TM, 512), lambda i: (i, 0)), pl.BlockSpec((TM, 256), lambda i: (i, 0)),
                  pl.BlockSpec((TM, LANES), lambda i: (i % tpb, 0)), pl.BlockSpec((TM, LANES), lambda i: (i % tpb, 0)),
                  pl.BlockSpec((1, 512), lambda i: (0, 0)), pl.BlockSpec((1, LANES), lambda i: (0, 0))],
        out_specs=(pl.BlockSpec((B_HEADS, TM, B_DH), lambda i: (0, i, 0)),
                   pl.BlockSpec((B_KV, TM, B_DH), lambda i: (0, i, 0)),
                   pl.BlockSpec((B_KV, TM, LANES), lambda i: (0, i, 0))),
        compiler_params=_cp(("arbitrary",)), name="attn_prep",
    )(b_q, b_kv, cos, sin, jnp.tile(qg, B_HEADS).reshape(1, -1), jnp.tile(kg, B_KV).reshape(1, -1))


def _attn_kernel(q_ref, k_ref, v_ref, o_ref, *, ctx_len):
    tb = k_ref.shape[1]
    q = q_ref[...].reshape(B_GROUP * TQ, B_DH)

    def attend(klen):
        starts = [0] + list(range(ctx_len, klen, KEY_CHUNK))
        m = acc = None
        for s0, s1 in zip(starts, starts[1:] + [klen]):
            s = _dot_nt(q, k_ref[0, s0:s1, :])
            smax = jnp.max(s, axis=-1, keepdims=True)
            if m is None:
                m = smax
                acc = _dot(jnp.exp((s - m).astype(bf16)), v_ref[0, s0:s1, :])
            else:
                m_new = jnp.maximum(m, smax)
                acc = jnp.exp(m - m_new) * acc + _dot(jnp.exp((s - m_new).astype(bf16)), v_ref[0, s0:s1, :])
                m = m_new
        o = acc[:, :B_DH] / acc[:, B_DH:B_DH + 1]
        o_ref[...] = o.reshape(B_GROUP, TQ, B_DH).astype(bf16)

    is_ctx = pl.program_id(2) < ctx_len // TQ

    @pl.when(is_ctx)
    def _():
        attend(ctx_len)

    @pl.when(jnp.logical_not(is_ctx))
    def _():
        attend(tb)


def _attn(q, k, v, nb, ctx_len):
    t = q.shape[1]
    tb = t // nb
    nq = tb // TQ
    return pl.pallas_call(
        functools.partial(_attn_kernel, ctx_len=ctx_len),
        out_shape=SDS((B_HEADS, t, B_DH), bf16), grid=(nb, B_KV, nq),
        in_specs=[pl.BlockSpec((B_GROUP, TQ, B_DH), lambda b, g, i: (g, b * nq + i, 0)),
                  pl.BlockSpec((1, tb, B_DH), lambda b, g, i: (g, b, 0)),
                  pl.BlockSpec((1, tb, LANES), lambda b, g, i: (g, b, 0))],
        out_specs=pl.BlockSpec((B_GROUP, TQ, B_DH), lambda b, g, i: (g, b * nq + i, 0)),
        compiler_params=_cp(("arbitrary", "arbitrary", "arbitrary")), name="attn",
    )(q, k, v)


def _head_rms(y, g, dv):
    parts = []
    for h in range(y.shape[1] // dv):
        yh = y[:, dv * h:dv * (h + 1)]
        parts.append(yh * lax.rsqrt(jnp.mean(yh * yh, axis=-1, keepdims=True) + EPS))
    return jnp.concatenate(parts, axis=1) * g


def _merge_kernel(ha_ref, ao_ref, att_ref, hc_ref, cr_ref, gate_ref, x_ref, g1_ref, sh2_ref, sc2_ref,
                  ag_ref, cg_ref, wb_ref, wo_ref, gn2_ref, wr_ref, br_ref,
                  xo_ref, h2_ref, ti_ref, tw_ref, rk_ref, cnt_ref, cnt_s):
    i = pl.program_id(0)

    @pl.when(i == 0)
    def _():
        cnt_s[...] = jnp.zeros_like(cnt_s)

    d = x_ref.shape[1]
    ya = _head_rms(ha_ref[...].astype(f32), ag_ref[...], A_DV) * _sigmoid(ao_ref[...].astype(f32))
    cr = cr_ref[...].astype(f32)
    yc = _head_rms(hc_ref[...].astype(f32), cg_ref[...], C_DV) * (cr * _sigmoid(cr))
    yb = jnp.concatenate([att_ref[h] for h in range(B_HEADS)], axis=1)
    merged = jnp.zeros((TM, d), f32)
    for n, y in enumerate((ya.astype(bf16), yb, yc.astype(bf16))):
        merged = merged + _sigmoid(gate_ref[:, d * n:d * (n + 1)]) * _dot(y, wb_ref[n])
    x = x_ref[...] + g1_ref[0] * _dot(merged.astype(bf16), wo_ref[...])
    xo_ref[...] = x
    h2 = _rms_mod(x, gn2_ref[...], sh2_ref[0], sc2_ref[0])
    h2_ref[...] = _pack_bf16_pairs(h2)

    logits = _dot_nt(wr_ref[...], h2, HI) + br_ref[...]
    eid = _iota((N_EXPERTS, TM), 0)
    work = logits
    onehot = jnp.zeros((N_EXPERTS, TM), f32)
    vals, sels = [], []
    for k in range(TOP_K):
        mk = jnp.max(work, axis=0, keepdims=True)
        ik = jnp.min(jnp.where(work == mk, eid, N_EXPERTS), axis=0, keepdims=True)
        sel = eid == ik
        work = jnp.where(sel, -jnp.inf, work)
        onehot = onehot + sel.astype(f32)
        ti_ref[k:k + 1, :] = ik
        vals.append(mk)
        sels.append(sel)
    ex = [jnp.exp(vk - vals[0]) for vk in vals]
    tot = ex[0] + ex[1] + ex[2] + ex[3]
    for k in range(TOP_K):
        tw_ref[k:k + 1, :] = ex[k] / tot

    ut = (_iota((TM, TM), 0) <= _iota((TM, TM), 1)).astype(bf16)
    incl = _dot(onehot.astype(bf16), ut)
    rank = cnt_s[...][:, 0:1] + incl - onehot
    for k in range(TOP_K):
        rk_ref[k:k + 1, :] = jnp.sum(jnp.where(sels[k], rank, 0.0), axis=0, keepdims=True).astype(jnp.int32)
    cnt_s[...] = cnt_s[...] + incl[:, TM - 1:TM]
    cnt_ref[...] = cnt_s[...]


def _merge(p, h_a, att, h_c, x, mods, a_norm_g, c_norm_g, w_branch, w_out, g_norm2, w_router, b_router,
           tpb, ctx_tiles, nb):
    t, d = x.shape
    row = lambda w: pl.BlockSpec((TM, w), lambda i: (i, 0))
    const = lambda shape: pl.BlockSpec(shape, lambda i: tuple(0 for _ in shape))
    return pl.pallas_call(
        _merge_kernel,
        out_shape=(SDS((t, d), f32), SDS((t, d // 2), jnp.int32), SDS((TOP_K, t), jnp.int32), SDS((TOP_K, t), f32),
                   SDS((TOP_K, t), jnp.int32), SDS((N_EXPERTS, LANES), f32)),
        grid=(t // TM,),
        in_specs=[row(512), row(512), pl.BlockSpec((B_HEADS, TM, B_DH), lambda i: (0, i, 0)), row(512), row(512),
                  row(3 * d), row(d),
                  _mod_spec(2, tpb, ctx_tiles, nb), _mod_spec(3, tpb, ctx_tiles, nb), _mod_spec(4, tpb, ctx_tiles, nb),
                  const((1, 512)), const((1, 512)), const((3, MIX_W, d)), const((d, d)), const((1, d)),
                  const((N_EXPERTS, d)), const((N_EXPERTS, 1))],
        out_specs=(row(d), row(d // 2), pl.BlockSpec((TOP_K, TM), lambda i: (0, i)), pl.BlockSpec((TOP_K, TM), lambda i: (0, i)),
                   pl.BlockSpec((TOP_K, TM), lambda i: (0, i)), const((N_EXPERTS, LANES))),
        scratch_shapes=[pltpu.VMEM((N_EXPERTS, LANES), f32)],
        compiler_params=_cp(("arbitrary",)), name="merge",
    )(h_a, p["a_o"], att, h_c, p["c_r"], p["gate"], x, mods, mods, mods,
      a_norm_g.reshape(1, -1), c_norm_g.reshape(1, -1), w_branch, w_out, g_norm2.reshape(1, -1),
      w_router.T, b_router.reshape(-1, 1))


def _sc_mesh():
    return plsc.VectorSubcoreMesh(core_axis_name="c", subcore_axis_name="s")


def _sc_scatter_rows(src, idx, n_out):
    v, d = src.shape
    n = idx.shape[0]
    per_w = n // (SC_NUM_CORES * SC_NUM_SUBCORES)
    assert per_w % SC_WINDOW == 0 and v % SC_WINDOW == 0

    @functools.partial(pl.kernel, out_type=SDS((n_out, d), src.dtype), mesh=_sc_mesh(),
                       scratch_types=[pltpu.VMEM((SC_WINDOW,), jnp.int32), pltpu.VMEM((SC_WINDOW, d), src.dtype),
                                      pltpu.SemaphoreType.DMA])
    def k(x_hbm, i_hbm, o_hbm, idx_v, rows_v, sem):
        wid = lax.axis_index("s") * SC_NUM_CORES + lax.axis_index("c")

        @pl.loop(0, per_w // SC_WINDOW)
        def _(j):
            base = wid * per_w + j * SC_WINDOW
            pltpu.sync_copy(i_hbm.at[pl.ds(base, SC_WINDOW)], idx_v)
            pltpu.sync_copy(x_hbm.at[pl.ds(lax.rem(base, v), SC_WINDOW)], rows_v)
            pltpu.async_copy(rows_v, o_hbm.at[idx_v], sem).wait()

    return k(src, idx)


def _sc_gather_rows(table, idx):
    d = table.shape[1]
    n = idx.shape[0]
    per_w = n // (SC_NUM_CORES * SC_NUM_SUBCORES)
    assert per_w % SC_WINDOW == 0

    @functools.partial(pl.kernel, out_type=SDS((n, d), table.dtype), mesh=_sc_mesh(),
                       scratch_types=[pltpu.VMEM((SC_WINDOW,), jnp.int32), pltpu.VMEM((SC_WINDOW, d), table.dtype),
                                      pltpu.SemaphoreType.DMA])
    def k(x_hbm, i_hbm, o_hbm, idx_v, rows_v, sem):
        wid = lax.axis_index("s") * SC_NUM_CORES + lax.axis_index("c")

        @pl.loop(0, per_w // SC_WINDOW)
        def _(j):
            base = wid * per_w + j * SC_WINDOW
            pltpu.sync_copy(i_hbm.at[pl.ds(base, SC_WINDOW)], idx_v)
            pltpu.async_copy(x_hbm.at[idx_v], rows_v, sem).wait()
            pltpu.sync_copy(rows_v, o_hbm.at[pl.ds(base, SC_WINDOW)])

    return k(table, idx)


GU_BLOCK = 2 * LANES


def _deinterleave_perm():
    n = np.arange(GU_BLOCK)
    src = np.where(n < LANES, 2 * n, 2 * (n - LANES) + 1)
    return jnp.asarray(np.arange(GU_BLOCK)[:, None] == src[None, :], bf16)


def _expert_kernel(te_ref, nv_ref, x_ref, w1_ref, b1_ref, w2_ref, b2_ref, perm_ref, y_ref, w1_s, w2_s):
    i = pl.program_id(0)
    valid = i < nv_ref[0]
    new_expert = jnp.logical_or(i == 0, te_ref[i] != te_ref[jnp.maximum(i - 1, 0)])
    dff2 = w1_ref.shape[2]

    @pl.when(jnp.logical_and(valid, new_expert))
    def _():
        for cb in range(dff2 // GU_BLOCK):
            cs = slice(GU_BLOCK * cb, GU_BLOCK * (cb + 1))
            w1_s[:, cs] = _dot(w1_ref[0, :, cs].astype(bf16), perm_ref[...]).astype(bf16)
        w2_s[...] = w2_ref[0].astype(bf16)

    @pl.when(valid)
    def _():
        x = jnp.concatenate(_unpack_bf16_pairs(x_ref[...]), axis=1).astype(bf16)
        gu = _dot(x, w1_s[...]) + b1_ref[0]
        nblk = dff2 // GU_BLOCK
        g = jnp.concatenate([gu[:, GU_BLOCK * cb:GU_BLOCK * cb + LANES] for cb in range(nblk)], axis=1)
        u = jnp.concatenate([gu[:, GU_BLOCK * cb + LANES:GU_BLOCK * (cb + 1)] for cb in range(nblk)], axis=1)
        gate = jnp.minimum(g, SWIGLU_LIMIT)
        up = jnp.clip(u, -SWIGLU_LIMIT, SWIGLU_LIMIT)
        a = (up + 1.0) * gate * _sigmoid(SWIGLU_ALPHA * gate)
        y_ref[...] = _pack_bf16_pairs(_dot(a.astype(bf16), w2_s[...]) + b2_ref[0])


def _experts(xs, tile_e, n_valid, layer, w1, b1, w2, b2):
    p, dw = xs.shape
    d = 2 * dw
    dff2 = w1.shape[-1]
    dff = w2.shape[2]
    row = lambda i, te, nv: (jnp.minimum(i, nv[0] - 1), 0)
    wsel = lambda i, te, nv: (layer, te[i], 0, 0)
    sq = pl.Squeezed()
    return pl.pallas_call(
        _expert_kernel, out_shape=SDS((p, dw), jnp.int32),
        grid_spec=pltpu.PrefetchScalarGridSpec(
            num_scalar_prefetch=2, grid=(p // TME,),
            in_specs=[pl.BlockSpec((TME, dw), row),
                      pl.BlockSpec((sq, 1, d, dff2), wsel), pl.BlockSpec((sq, 1, 1, dff2), wsel),
                      pl.BlockSpec((sq, 1, dff, d), wsel), pl.BlockSpec((sq, 1, 1, d), wsel),
                      pl.BlockSpec((GU_BLOCK, GU_BLOCK), lambda i, te, nv: (0, 0))],
            out_specs=pl.BlockSpec((TME, dw), row),
            scratch_shapes=[pltpu.VMEM((d, dff2), bf16), pltpu.VMEM((dff, d), bf16)]),
        compiler_params=_cp(("arbitrary",)), name="experts",
    )(tile_e, n_valid, xs, w1, b1, w2, b2, _deinterleave_perm())


def _combine_kernel(x_ref, yg_ref, w_ref, g2_ref, *rest, final):
    w = w_ref[...]
    acc_lo = acc_hi = None
    for k in range(TOP_K):
        lo, hi = _unpack_bf16_pairs(yg_ref[k])
        wk = w[:, k:k + 1]
        acc_lo = wk * lo if acc_lo is None else acc_lo + wk * lo
        acc_hi = wk * hi if acc_hi is None else acc_hi + wk * hi
    x = x_ref[...] + g2_ref[0] * jnp.concatenate([acc_lo, acc_hi], axis=1)
    if final:
        gf_ref, o_ref = rest
        x = x * lax.rsqrt(jnp.mean(x * x, axis=-1, keepdims=True) + EPS) * gf_ref[...]
    else:
        (o_ref,) = rest
    o_ref[...] = x


def _combine(x, yg, wcol, mods, tpb, ctx_tiles, nb, g_final=None):
    t, d = x.shape
    final = g_final is not None
    if final:
        lat = tpb - ctx_tiles
        rmap = lambda i: ((i // lat) * tpb + ctx_tiles + i % lat)
        grid = (nb * lat,)
        mod = pl.BlockSpec((1, 1, 1024), lambda i: (i // lat, 0, 5))
        n_out = nb * lat * TM
    else:
        rmap = lambda i: i
        grid = (t // TM,)
        mod = _mod_spec(5, tpb, ctx_tiles, nb)
        n_out = t
    in_specs = [pl.BlockSpec((TM, d), lambda i: (rmap(i), 0)),
                pl.BlockSpec((TOP_K, TM, d // 2), lambda i: (0, rmap(i), 0)),
                pl.BlockSpec((TM, TOP_K), lambda i: (rmap(i), 0)), mod]
    args = [x, yg, wcol, mods]
    if final:
        in_specs.append(pl.BlockSpec((1, d), lambda i: (0, 0)))
        args.append(g_final.reshape(1, -1))
    return pl.pallas_call(
        functools.partial(_combine_kernel, final=final), out_shape=SDS((n_out, d), f32), grid=grid,
        in_specs=in_specs, out_specs=pl.BlockSpec((TM, d), lambda i: (i, 0)),
        compiler_params=_cp(("arbitrary",)), name="combine_final" if final else "combine",
    )(*args)


def _routing_tables(top_i, rank, counts, n_tiles):
    cnt = counts[:, 0].astype(jnp.int32)
    padded = ((cnt + TME - 1) // TME) * TME
    ends = jnp.cumsum(padded)
    starts = ends - padded
    eids = jnp.arange(N_EXPERTS, dtype=jnp.int32)
    start_of = jnp.sum(jnp.where(top_i[..., None] == eids, starts, 0), axis=-1)
    pos = (start_of + rank).reshape(-1)
    n_valid = ends[-1] // TME
    tile_start = jnp.arange(n_tiles, dtype=jnp.int32) * TME
    tile_e = jnp.sum(tile_start[:, None] >= ends[None, :], axis=1).astype(jnp.int32)
    tile_e = jnp.where(jnp.arange(n_tiles) < n_valid, tile_e, tile_e[jnp.maximum(n_valid - 1, 0)])
    tile_e = jnp.minimum(tile_e, N_EXPERTS - 1)
    return pos, tile_e, n_valid.reshape(1).astype(jnp.int32)


def kernel(x, c, ctx, c_ctx, w_ada, b_ada, g_norm1, w_in, a_conv_w, a_conv_b, a_gate_b, a_norm_g, b_q_norm_g, b_k_norm_g, c_w_up, c_b_up, c_norm_g, w_branch, w_out, g_norm2, w_router, b_router, w_e1, b_e1, w_e2, b_e2, g_final):
    nb, seq, d = x.shape
    ctx_len = ctx.shape[1]
    depth = w_ada.shape[0]
    tb = ctx_len + seq
    t = nb * tb
    tpb, ctx_tiles = tb // TM, ctx_len // TM
    assert d == 1024 and nb < 16 and seq % TM == 0 and ctx_len % TM == 0 and ctx_len % TQ == 0
    n_assign = TOP_K * t
    n_sorted = n_assign + N_EXPERTS * TME
    n_tiles = n_sorted // TME

    xs = jnp.concatenate([ctx, x], axis=1).reshape(t, d)
    cc = jnp.zeros((16, d), f32).at[:nb].set(c).at[nb].set(c_ctx)
    mods_all = _adaln(cc, w_ada, b_ada)
    cos, sin = _rope_tables(seq, ctx_len)
    w_in_r = _arrange_w_in(w_in)
    w_branch_b, w_out_b = w_branch.astype(bf16), w_out.astype(bf16)
    col = np.arange(b_e1.shape[-1])
    within = col % GU_BLOCK
    src = (col // GU_BLOCK) * GU_BLOCK + np.where(within < LANES, 2 * within, 2 * (within - LANES) + 1)
    b1 = b_e1[..., src][..., None, :]
    b2 = b_e2[..., None, :]

    out = None
    for l in range(depth):
        mods = mods_all[l].reshape(16, 1, 6 * d)
        p = _inproj(xs, mods, g_norm1[l].reshape(1, -1), w_in_r[l], tpb, ctx_tiles, nb)
        h_a = _mlstm(p["a_qk"], p["a_v"], p["small"], a_conv_w[l], a_conv_b[l], a_gate_b[l], nb, ctx_len)
        h_c = _gla(p["c_qk"], p["c_v"], p["small"], c_w_up[l], c_b_up[l], nb, ctx_len)
        qn, kn, vn = _attn_prep(p["b_q"], p["b_kv"], cos, sin, b_q_norm_g[l], b_k_norm_g[l], tpb)
        att = _attn(qn, kn, vn, nb, ctx_len)
        xs, h2, top_i, top_w, rank, counts = _merge(
            p, h_a, att, h_c, xs, mods, a_norm_g[l], c_norm_g[l], w_branch_b[l], w_out_b[l], g_norm2[l],
            w_router[l], b_router[l], tpb, ctx_tiles, nb)
        pos, tile_e, n_valid = _routing_tables(top_i, rank, counts, n_tiles)
        x_sorted = _sc_scatter_rows(h2, pos, n_sorted)
        y_sorted = _experts(x_sorted, tile_e, n_valid, l, w_e1, b1, w_e2, b2)
        yg = _sc_gather_rows(y_sorted, pos).reshape(TOP_K, t, d // 2)
        if l == depth - 1:
            out = _combine(xs, yg, top_w.T, mods, tpb, ctx_tiles, nb, g_final=g_final)
        else:
            xs = _combine(xs, yg, top_w.T, mods, tpb, ctx_tiles, nb)
    return out.reshape(nb, seq, d)
```

```python
import functools

import jax
import jax.numpy as jnp
import numpy as np
from jax import lax
from jax.experimental import pallas as pl
from jax.experimental.pallas import tpu as pltpu
from jax.experimental.pallas import tpu_sc as plsc

f32 = jnp.float32
bf16 = jnp.bfloat16
HI = lax.Precision.HIGHEST
SDS = jax.ShapeDtypeStruct

EPS = 1e-6
CHUNK = 128
GRID_W = 64
ROPE_THETA = 10000.0
MIX_W = 512
A_HEADS, A_DQK, A_DV = 4, 64, 128
B_HEADS, B_KV, B_DH = 8, 2, 64
B_GROUP = B_HEADS // B_KV
C_HEADS, C_DK, C_DV, C_RANK, C_TAU = 4, 64, 128, 16, 16.0
N_EXPERTS, TOP_K = 32, 4
SWIGLU_LIMIT, SWIGLU_ALPHA = 7.0, 1.702

LANES = 128
SC_NUM_CORES = 2
SC_NUM_SUBCORES = 16
VMEM_LIMIT = 56 * 2 ** 20

TM = 256
TQ = 128
TME = 512
KEY_CHUNK = 2048
SC_WINDOW = 64


def _cp(sem, vmem=VMEM_LIMIT):
    return pltpu.CompilerParams(dimension_semantics=sem, vmem_limit_bytes=vmem)


def _dot(a, b, precision=None):
    return jnp.dot(a, b, preferred_element_type=f32, precision=precision)


def _dot_nt(a, b, precision=None):
    return lax.dot_general(a, b, (((1,), (1,)), ((), ())), preferred_element_type=f32, precision=precision)


def _dot_tn(a, b):
    return lax.dot_general(a, b, (((0,), (0,)), ((), ())), preferred_element_type=f32)


def _sigmoid(x):
    return 1.0 / (1.0 + jnp.exp(-x))


def _log_sigmoid(x):
    return jnp.minimum(x, 0.0) - jnp.log(1.0 + jnp.exp(-jnp.abs(x)))


def _iota(shape, axis):
    return lax.broadcasted_iota(jnp.int32, shape, axis)


def _pack_bf16_pairs(x):
    n = x.shape[1] // 2
    lo = pltpu.bitcast(x[:, :n].astype(bf16).astype(f32), jnp.uint32)
    hi = pltpu.bitcast(x[:, n:].astype(bf16).astype(f32), jnp.uint32)
    return pltpu.bitcast((lo >> 16) | hi, jnp.int32)


def _unpack_bf16_pairs(w):
    u = pltpu.bitcast(w, jnp.uint32)
    return pltpu.bitcast(u << 16, f32), pltpu.bitcast(u & jnp.uint32(0xFFFF0000), f32)


def _rms_mod(x, g, shift, scale):
    y = x * lax.rsqrt(jnp.mean(x * x, axis=-1, keepdims=True) + EPS) * g
    return y * (1.0 + scale) + shift


def _adaln_kernel(cc_ref, w_ref, b_ref, o_ref):
    cc = cc_ref[...]
    s = cc * _sigmoid(cc)
    o_ref[0] = _dot(s, w_ref[0], HI) + b_ref[0]


def _adaln(cc, w_ada, b_ada):
    nl, d, n = w_ada.shape
    return pl.pallas_call(
        _adaln_kernel, out_shape=SDS((nl, 16, n), f32), grid=(nl, n // 1024),
        in_specs=[pl.BlockSpec((16, d), lambda l, j: (0, 0)),
                  pl.BlockSpec((1, d, 1024), lambda l, j: (l, 0, j)),
                  pl.BlockSpec((1, 1, 1024), lambda l, j: (l, 0, j))],
        out_specs=pl.BlockSpec((1, 16, 1024), lambda l, j: (l, 0, j)),
        compiler_params=_cp(("arbitrary", "arbitrary")), name="adaln",
    )(cc, w_ada, b_ada.reshape(nl, 1, n))


def _mod_spec(col, tpb, ctx_tiles, nb):
    def imap(i):
        return (jnp.where((i % tpb) >= ctx_tiles, i // tpb, nb), 0, col)
    return pl.BlockSpec((1, 1, 1024), imap)


_K1_GROUPS = (("a_qk", 512, bf16), ("a_v", 512, bf16), ("a_o", 512, bf16), ("b_q", 512, bf16),
              ("b_kv", 256, bf16), ("c_qk", 512, bf16), ("c_v", 512, bf16), ("c_r", 512, bf16),
              ("gate", 3072, bf16), ("small", 128, f32))
_K1_WIDTH = sum(w for _, w, _ in _K1_GROUPS)


def _arrange_w_in(w_in):
    z = jnp.zeros(w_in.shape[:-1] + (LANES - 48,), w_in.dtype)
    w = jnp.concatenate([w_in[..., 0:1024], w_in[..., 1040:1552], w_in[..., 1552:2320], w_in[..., 2320:3344],
                         w_in[..., 3376:3888], w_in[..., 3888:6960], w_in[..., 1024:1040], w_in[..., 3344:3376], z],
                        axis=-1)
    assert w.shape[-1] == _K1_WIDTH
    return w.astype(bf16)


def _inproj_kernel(x_ref, sh_ref, sc_ref, g_ref, w_ref, *out_refs):
    h = _rms_mod(x_ref[...], g_ref[...], sh_ref[0], sc_ref[0]).astype(bf16)
    c0 = 0
    for (_, width, dt), o_ref in zip(_K1_GROUPS, out_refs):
        for j in range(0, width, 512):
            wj = min(512, width - j)
            o_ref[:, j:j + wj] = _dot(h, w_ref[:, c0 + j:c0 + j + wj]).astype(dt)
        c0 += width


def _inproj(x, mods, g, w, tpb, ctx_tiles, nb):
    t, d = x.shape
    outs = tuple(SDS((t, wd), dt) for _, wd, dt in _K1_GROUPS)
    res = pl.pallas_call(
        _inproj_kernel, out_shape=outs, grid=(t // TM,),
        in_specs=[pl.BlockSpec((TM, d), lambda i: (i, 0)),
                  _mod_spec(0, tpb, ctx_tiles, nb), _mod_spec(1, tpb, ctx_tiles, nb),
                  pl.BlockSpec((1, d), lambda i: (0, 0)),
                  pl.BlockSpec((d, _K1_WIDTH), lambda i: (0, 0), pipeline_mode=pl.Buffered(1))],
        out_specs=tuple(pl.BlockSpec((TM, wd), lambda i: (i, 0)) for _, wd, _ in _K1_GROUPS),
        compiler_params=_cp(("arbitrary",)), name="inproj",
    )(x, mods, mods, g, w)
    return dict(zip([n for n, _, _ in _K1_GROUPS], res))


def _bwd_chunk(i, ncc, nch):
    return jnp.where(i < ncc, ncc - 1 - i, nch - 1 + ncc - i)


def _split3(x, axis=0):
    hi = x.astype(bf16)
    r1 = x - hi.astype(f32)
    mid = r1.astype(bf16)
    lo = (r1 - mid.astype(f32)).astype(bf16)
    return jnp.concatenate([hi, mid, lo], axis=axis)


def _mlstm_kernel(qk_ref, v_ref, sm_ref, cw_ref, cb_ref, gb_ref, tri_ref, lmask_ref, exps_ref, expd_ref, expv_ref,
                  oseg_ref, kmask_ref, vmask_ref, cmask_ref, nmask_ref, o_ref, qk_s, hacc, c_s, n_s, *, ctx_len):
    tb = qk_ref.shape[0]
    nch = tb // CHUNK
    ncc = ctx_len // CHUNK
    nu = 2 * A_HEADS

    cw = cw_ref[...]
    cb = cb_ref[...]
    row = _iota((CHUNK, 2 * A_HEADS * A_DQK), 0)
    kscale = jnp.where(_iota((1, 2 * A_HEADS * A_DQK), 1) >= A_HEADS * A_DQK, A_DQK ** -0.5, 1.0)

    def conv_body(c, carry):
        r0 = pl.multiple_of(c * CHUNK, CHUNK)
        x = qk_ref[pl.ds(r0, CHUNK), :].astype(f32)
        rp = pl.multiple_of(jnp.maximum(r0 - 16, 0), 16)
        rn = pl.multiple_of(jnp.minimum(r0 + CHUNK, tb - 16), 16)
        prev = qk_ref[pl.ds(rp, 16), :].astype(f32)[15:16, :]
        nxt = qk_ref[pl.ds(rn, 16), :].astype(f32)[0:1, :]
        has_prev = jnp.logical_and(r0 != 0, r0 != ctx_len)
        has_next = jnp.logical_and(r0 + CHUNK != ctx_len, r0 + CHUNK != tb)
        prev = jnp.where(has_prev, prev, 0.0)
        nxt = jnp.where(has_next, nxt, 0.0)
        xm = jnp.where(row == 0, prev, pltpu.roll(x, 1, 0))
        xp = jnp.where(row == CHUNK - 1, nxt, pltpu.roll(x, CHUNK - 1, 0))
        y = cw[0:1] * xm + cw[1:2] * x + cw[2:3] * xp + cb
        y = y * _sigmoid(y) * kscale
        qk_s[pl.ds(r0, CHUNK), :] = y.astype(bf16)
        return carry

    lax.fori_loop(0, nch, conv_body, 0)

    hacc[...] = jnp.zeros_like(hacc)
    c_s[...] = jnp.zeros_like(c_s)
    n_s[...] = jnp.zeros_like(n_s)

    lane = _iota((CHUNK, LANES), 1)
    row = _iota((CHUNK, LANES), 0)
    is_f = jnp.logical_and(lane >= nu, lane < 2 * nu)
    gb = gb_ref[...]
    neg_inf = jnp.float32(-jnp.inf)
    nk = A_HEADS * A_DQK
    ones_rows = jnp.ones((CHUNK, LANES), bf16)

    def cummax(x, d):
        for sh in [1 << e for e in range(CHUNK.bit_length() - 1)]:
            if d == 0:
                y = jnp.where(row >= sh, pltpu.roll(x, sh, 0), neg_inf)
            else:
                y = jnp.where(row < CHUNK - sh, pltpu.roll(x, CHUNK - sh, 0), neg_inf)
            x = jnp.maximum(x, y)
        return x

    def body(i, ms):
        ms = list(ms)
        pending = []
        for d in range(2):
            c = i if d == 0 else _bwd_chunk(i, ncc, nch)
            r0 = pl.multiple_of(c * CHUNK, CHUNK)
            qk = qk_s[pl.ds(r0, CHUNK), :]
            q, k = qk[:, :nk], qk[:, nk:]
            v = v_ref[pl.ds(r0, CHUNK), :]
            mine = jnp.logical_and(lane >= d * A_HEADS, lane < (d + 1) * A_HEADS)
            keep = lambda x: jnp.where(mine[:x.shape[0]], x, 0.0)
            sm = sm_ref[pl.ds(r0, CHUNK), :] + gb
            pre = jnp.where(is_f, _log_sigmoid(sm), sm)
            b = pltpu.roll(_dot(tri_ref[d], _split3(pre)), LANES - nu, 1)
            g = keep(pre - b)
            cm = cummax(g, d)
            g_t = g.T
            g_row = jnp.concatenate([g_t[d * A_HEADS + h:d * A_HEADS + h + 1, :] for h in range(A_HEADS)], axis=1)
            pm = jnp.exp(jnp.where(lmask_ref[d] > 0, g_row - _dot(_split3(cm, 1), exps_ref[d]), neg_inf))
            kst = jnp.tile(k, (A_HEADS, 1)) * kmask_ref[...]
            s = (_dot_nt(q, kst) * pm).astype(bf16)
            p1 = _dot(s, jnp.tile(v, (A_HEADS, 1)) * vmask_ref[...])
            sden = _dot(s, oseg_ref[d])
            cbd, nbd = c_s[d], n_s[d]
            qc = _dot(q, cbd.astype(bf16))
            qn = _dot(q, nbd.astype(bf16))
            m = ms[d]
            rmax, li = b + cm, b + m
            mj = jnp.maximum(li, rmax)
            alpha, beta = jnp.exp(rmax - mj), jnp.exp(li - mj)
            inv = 1.0 / jnp.maximum(jnp.abs(alpha * sden + beta * qn), jnp.exp(-mj))
            ab = _dot(_split3(jnp.concatenate([keep(alpha * inv), keep(beta * inv)], axis=0), 1), expv_ref[d])
            hout = ab[:CHUNK] * p1 + ab[CHUNK:] * qc
            last = CHUNK - 1 if d == 0 else 0
            b_last, m_new, mloc = b[last:last + 1], mj[last:last + 1], rmax[last:last + 1]
            wl = keep(jnp.exp(b_last - b + pre - mloc))
            kw = (k.astype(f32) * _dot(_split3(wl, 1), expd_ref[d])).astype(bf16)
            dec, gain = keep(jnp.exp(b_last + m - m_new)), keep(jnp.exp(mloc - m_new))
            dg = _dot(_split3(jnp.concatenate([dec, gain, jnp.zeros((6, LANES), f32)], axis=0), 1), expv_ref[d])
            c_new = cbd * dg[0:1] + (_dot_tn(kw, v) * cmask_ref[...]) * dg[1:2]
            n_new = nbd * dec + (_dot_tn(kw, ones_rows) * nmask_ref[d]) * gain
            ms[d] = keep(m_new)
            pending.append((r0, hout, c_new, n_new))
        for d, (r0, hout, c_new, n_new) in enumerate(pending):
            hacc[pl.ds(r0, CHUNK), :] += hout
            c_s[d] = c_new
            n_s[d] = n_new
        return tuple(ms)

    lax.fori_loop(0, nch, body, tuple(jnp.zeros((1, LANES), f32) for _ in range(2)))
    o_ref[...] = hacc[...].astype(bf16)


def _mlstm_constants():
    t = np.arange(CHUNK)
    lane = np.arange(LANES)
    hs = np.arange(A_HEADS * CHUNK)
    hd = np.arange(A_HEADS * A_DQK)
    hv = np.arange(A_HEADS * A_DV)
    tri3, lmask, exps, expd, expv, oseg, nmask = [], [], [], [], [], [], []
    for d in range(2):
        pos = t if d == 0 else CHUNK - 1 - t
        tri = pos[None, :] <= pos[:, None]
        tri3.append(np.tile(tri, (1, 3)))
        lmask.append(np.tile(tri, (1, A_HEADS)))
        unit = lambda idx, width: lane[:, None] == d * A_HEADS + idx[None, :] // width
        exps.append(np.tile(unit(hs, CHUNK), (3, 1)))
        expd.append(np.tile(unit(hd, A_DQK), (3, 1)))
        expv.append(np.tile(unit(hv, A_DV), (3, 1)))
        oseg.append(unit(hs, CHUNK).T)
        nmask.append(unit(hd, A_DQK).T)
    kmask = hs[:, None] // CHUNK == hd[None, :] // A_DQK
    vmask = hs[:, None] // CHUNK == hv[None, :] // A_DV
    cmask = hd[:, None] // A_DQK == hv[None, :] // A_DV
    as_b = lambda x: jnp.asarray(np.stack(x) if isinstance(x, list) else x, bf16)
    return (as_b(tri3), as_b(lmask), as_b(exps), as_b(expd), as_b(expv), as_b(oseg), as_b(kmask), as_b(vmask),
            jnp.asarray(cmask, f32), jnp.asarray(np.stack(nmask), f32))


def _mlstm(a_qk, a_v, small, conv_w, conv_b, gate_b, nb, ctx_len):
    t = a_qk.shape[0]
    tb = t // nb
    gb = jnp.zeros((1, LANES), f32).at[0, :4 * A_HEADS].set(gate_b.reshape(-1))
    consts = _mlstm_constants()
    const_spec = lambda a: pl.BlockSpec(a.shape, lambda b, n=a.ndim: (0,) * n, pipeline_mode=pl.Buffered(1))
    return pl.pallas_call(
        functools.partial(_mlstm_kernel, ctx_len=ctx_len),
        out_shape=SDS((t, MIX_W), bf16), grid=(nb,),
        in_specs=[pl.BlockSpec((tb, 512), lambda b: (b, 0)), pl.BlockSpec((tb, 512), lambda b: (b, 0)),
                  pl.BlockSpec((tb, LANES), lambda b: (b, 0)),
                  pl.BlockSpec((3, 512), lambda b: (0, 0)), pl.BlockSpec((1, 512), lambda b: (0, 0)),
                  pl.BlockSpec((1, LANES), lambda b: (0, 0))] + [const_spec(a) for a in consts],
        out_specs=pl.BlockSpec((tb, MIX_W), lambda b: (b, 0)),
        scratch_shapes=[pltpu.VMEM((tb, 512), bf16), pltpu.VMEM((tb, MIX_W), f32),
                        pltpu.VMEM((2, A_HEADS * A_DQK, A_HEADS * A_DV), f32),
                        pltpu.VMEM((2, A_HEADS * A_DQK, LANES), f32)],
        compiler_params=_cp(("arbitrary",)), name="mlstm",
    )(a_qk, a_v, small, conv_w, conv_b.reshape(1, -1), gb, *consts)


GLA_LEVELS = CHUNK.bit_length() - 1


def _gla_constants():
    t = np.arange(CHUNK)
    tri, gref, lmask = [], [], []
    for d in range(2):
        pos = t if d == 0 else CHUNK - 1 - t
        row_of = np.argsort(pos)
        tri.append(pos[None, :] <= pos[:, None])
        g, m = [], []
        for lvl in range(GLA_LEVELS):
            half = CHUNK >> (lvl + 1)
            ref_pos = (pos // (2 * half)) * (2 * half) + half
            g.append(t[None, :] == row_of[ref_pos][:, None])
            late, early = (pos % (2 * half)) >= half, (pos % (2 * half)) < half
            same = (pos[:, None] // (2 * half)) == (pos[None, :] // (2 * half))
            m.append(same & late[:, None] & early[None, :])
        m.append(t[:, None] == t[None, :])
        gref.append(np.concatenate(g, axis=0))
        lmask.append(np.stack([np.tile(x, (1, C_HEADS)) for x in m]))
    tri3 = np.stack([np.tile(x, (1, 3)) for x in tri])
    gref3 = np.stack([np.tile(x, (1, 3)) for x in gref])
    hs = np.arange(C_HEADS * CHUNK)
    kmask = (hs[:, None] // CHUNK) == (np.arange(C_HEADS * C_DK)[None, :] // C_DK)
    vmask = (hs[:, None] // CHUNK) == (np.arange(C_HEADS * C_DV)[None, :] // C_DV)
    bdm = (np.arange(2 * C_DV)[:, None] // C_DV) == (np.arange(2 * C_DK)[None, :] // C_DK)
    as_b = lambda x: jnp.asarray(x, bf16)
    return as_b(tri3), as_b(gref3), as_b(np.stack(lmask)), as_b(kmask), as_b(vmask), jnp.asarray(bdm, f32)


def _gla_kernel(qk_ref, v_ref, sm_ref, wup_ref, bup_ref, tri_ref, gref_ref, lmask_ref, kmask_ref, vmask_ref, bdm_ref,
                o_ref, hacc, s_s, *, ctx_len):
    tb = qk_ref.shape[0]
    nch = tb // CHUNK
    ncc = ctx_len // CHUNK
    nk = C_HEADS * C_DK
    pw = 2 * C_DK
    pv = 2 * C_DV
    npair = C_HEADS // 2

    hacc[...] = jnp.zeros_like(hacc)
    s_s[...] = jnp.zeros_like(s_s)
    qscale = C_DK ** -0.5

    def body(i, carry):
        for d in range(2):
            c = i if d == 0 else _bwd_chunk(i, ncc, nch)
            r0 = pl.multiple_of(c * CHUNK, CHUNK)
            sm = sm_ref[pl.ds(r0, CHUNK), :]
            la = _log_sigmoid(_dot(sm, wup_ref[d], HI) + bup_ref[d]) * (1.0 / C_TAU)
            qk = qk_ref[pl.ds(r0, CHUNK), :]
            q = qk[:, :nk].astype(f32) * qscale
            k = qk[:, nk:].astype(f32)
            v = v_ref[pl.ds(r0, CHUNK), :]
            b = _dot(tri_ref[d], _split3(la))
            bref = _dot(gref_ref[d], _split3(b))

            def scores(qe, ke):
                kst = jnp.tile(ke.astype(bf16), (C_HEADS, 1)) * kmask_ref[...]
                return _dot_nt(qe.astype(bf16), kst)

            a = scores(q, k) * lmask_ref[d, GLA_LEVELS]
            for lvl in range(GLA_LEVELS):
                rel = b - bref[CHUNK * lvl:CHUNK * (lvl + 1)]
                s = scores(q * jnp.exp(jnp.minimum(rel, 0.0)), k * jnp.exp(jnp.minimum(-rel, 0.0)))
                a = a + s * lmask_ref[d, lvl]
            vst = jnp.tile(v, (C_HEADS, 1)) * vmask_ref[...]
            o = _dot(a.astype(bf16), vst)

            last = CHUNK - 1 if d == 0 else 0
            bl = b[last:last + 1]
            qd = (q * jnp.exp(b)).astype(bf16)
            kd = (k * jnp.exp(bl - b)).astype(bf16)
            dec = jnp.exp(bl)
            o_int = []
            for p in range(npair):
                u = d * npair + p
                st = s_s[u]
                o_int.append(_dot_nt(qd[:, pw * p:pw * (p + 1)], st.astype(bf16)))
                upd = _dot_tn(v[:, pv * p:pv * (p + 1)], kd[:, pw * p:pw * (p + 1)])
                s_s[u] = st * dec[:, pw * p:pw * (p + 1)] + bdm_ref[...] * upd
            hacc[pl.ds(r0, CHUNK), :] += o + jnp.concatenate(o_int, axis=1)
        return carry

    lax.fori_loop(0, nch, body, 0)
    o_ref[...] = hacc[...].astype(bf16)


def _gla(c_qk, c_v, small, w_up, b_up, nb, ctx_len):
    t = c_qk.shape[0]
    tb = t // nb
    nk = C_HEADS * C_DK
    wz = jnp.zeros((2, LANES, nk), f32)
    wz = wz.at[0, 16:16 + C_RANK].set(w_up[0]).at[1, 16 + C_RANK:16 + 2 * C_RANK].set(w_up[1])
    consts = _gla_constants()
    const_spec = lambda a: pl.BlockSpec(a.shape, lambda b, n=a.ndim: (0,) * n, pipeline_mode=pl.Buffered(1))
    return pl.pallas_call(
        functools.partial(_gla_kernel, ctx_len=ctx_len),
        out_shape=SDS((t, MIX_W), bf16), grid=(nb,),
        in_specs=[pl.BlockSpec((tb, 512), lambda b: (b, 0)), pl.BlockSpec((tb, 512), lambda b: (b, 0)),
                  pl.BlockSpec((tb, LANES), lambda b: (b, 0)),
                  pl.BlockSpec((2, LANES, nk), lambda b: (0, 0, 0)), pl.BlockSpec((2, 1, nk), lambda b: (0, 0, 0))]
                 + [const_spec(a) for a in consts],
        out_specs=pl.BlockSpec((tb, MIX_W), lambda b: (b, 0)),
        scratch_shapes=[pltpu.VMEM((tb, MIX_W), f32), pltpu.VMEM((C_HEADS, 2 * C_DV, 2 * C_DK), f32)],
        compiler_params=_cp(("arbitrary",)), name="gla",
    )(c_qk, c_v, small, wz, b_up.reshape(2, 1, nk), *consts)


def _rope_tables(seq, ctx_len):
    n_f = B_DH // 4
    t = np.arange(seq)
    freqs = ROPE_THETA ** (-np.arange(n_f, dtype=np.float32) / n_f)
    hd = np.arange(B_DH)
    pos = np.where(hd[None, :] < B_DH // 2, (t // GRID_W)[:, None], (t % GRID_W)[:, None]).astype(np.float32)
    ang = jnp.asarray(pos * freqs[hd % n_f][None, :], f32)
    sign = np.where((hd % (2 * n_f)) < n_f, -1.0, 1.0).astype(np.float32)
    cos = jnp.concatenate([jnp.ones((ctx_len, B_DH), f32), jnp.cos(ang)], axis=0)
    sin = jnp.concatenate([jnp.zeros((ctx_len, B_DH), f32), jnp.sin(ang) * sign[None, :]], axis=0)
    return jnp.tile(cos, (1, 2)), jnp.tile(sin, (1, 2))


def _attn_prep_kernel(q_ref, kv_ref, cos_ref, sin_ref, qg_ref, kg_ref, qo_ref, ko_ref, vo_ref):
    cos = cos_ref[...]
    sin = sin_ref[...]

    def norm_rope(x, g):
        w = x.shape[1]
        bd = (_iota((w, w), 0) // B_DH == _iota((w, w), 1) // B_DH).astype(bf16)
        ss = _dot((x * x).astype(bf16), bd)
        xn = x * lax.rsqrt(ss * (1.0 / B_DH) + EPS) * g
        first = (_iota(x.shape, 1) % (B_DH // 2)) < (B_DH // 4)
        swapped = jnp.where(first, pltpu.roll(xn, w - B_DH // 4, 1), pltpu.roll(xn, B_DH // 4, 1))
        reps = w // LANES
        return xn * jnp.tile(cos, (1, reps)) + swapped * jnp.tile(sin, (1, reps))

    q = (norm_rope(q_ref[...].astype(f32), qg_ref[...]) * (B_DH ** -0.5)).astype(bf16)
    for h in range(B_HEADS):
        qo_ref[h] = q[:, B_DH * h:B_DH * (h + 1)]
    kv = kv_ref[...]
    k = norm_rope(kv[:, :B_KV * B_DH].astype(f32), kg_ref[...]).astype(bf16)
    ones_col = (_iota((kv.shape[0], LANES - B_DH), 1) == 0).astype(bf16)
    for h in range(B_KV):
        ko_ref[h] = k[:, B_DH * h:B_DH * (h + 1)]
        vh = kv[:, B_KV * B_DH + B_DH * h:B_KV * B_DH + B_DH * (h + 1)]
        vo_ref[h] = jnp.concatenate([vh, ones_col], axis=1)


def _attn_prep(b_q, b_kv, cos, sin, qg, kg, tpb):
    t = b_q.shape[0]
    return pl.pallas_call(
        _attn_prep_kernel,
        out_shape=(SDS((B_HEADS, t, B_DH), bf16), SDS((B_KV, t, B_DH), bf16), SDS((B_KV, t, LANES), bf16)),
        grid=(t // TM,),
        in_specs=[pl.BlockSpec((TM, 512), lambda i: (i, 0)), pl.BlockSpec((TM, 256), lambda i: (i, 0)),
                  pl.BlockSpec((TM, LANES), lambda i: (i % tpb, 0)), pl.BlockSpec((TM, LANES), lambda i: (i % tpb, 0)),
                  pl.BlockSpec((1, 512), lambda i: (0, 0)), pl.BlockSpec((1, LANES), lambda i: (0, 0))],
        out_specs=(pl.BlockSpec((B_HEADS, TM, B_DH), lambda i: (0, i, 0)),
                   pl.BlockSpec((B_KV, TM, B_DH), lambda i: (0, i, 0)),
                   pl.BlockSpec((B_KV, TM, LANES), lambda i: (0, i, 0))),
        compiler_params=_cp(("arbitrary",)), name="attn_prep",
    )(b_q, b_kv, cos, sin, jnp.tile(qg, B_HEADS).reshape(1, -1), jnp.tile(kg, B_KV).reshape(1, -1))


def _attn_kernel(q_ref, k_ref, v_ref, o_ref, *, ctx_len):
    tb = k_ref.shape[1]
    q = q_ref[...].reshape(B_GROUP * TQ, B_DH)

    def attend(klen):
        starts = [0] + list(range(ctx_len, klen, KEY_CHUNK))
        m = acc = None
        for s0, s1 in zip(starts, starts[1:] + [klen]):
            s = _dot_nt(q, k_ref[0, s0:s1, :])
            smax = jnp.max(s, axis=-1, keepdims=True)
            if m is None:
                m = smax
                acc = _dot(jnp.exp((s - m).astype(bf16)), v_ref[0, s0:s1, :])
            else:
                m_new = jnp.maximum(m, smax)
                acc = jnp.exp(m - m_new) * acc + _dot(jnp.exp((s - m_new).astype(bf16)), v_ref[0, s0:s1, :])
                m = m_new
        o = acc[:, :B_DH] / acc[:, B_DH:B_DH + 1]
        o_ref[...] = o.reshape(B_GROUP, TQ, B_DH).astype(bf16)

    is_ctx = pl.program_id(2) < ctx_len // TQ

    @pl.when(is_ctx)
    def _():
        attend(ctx_len)

    @pl.when(jnp.logical_not(is_ctx))
    def _():
        attend(tb)


def _attn(q, k, v, nb, ctx_len):
    t = q.shape[1]
    tb = t // nb
    nq = tb // TQ
    return pl.pallas_call(
        functools.partial(_attn_kernel, ctx_len=ctx_len),
        out_shape=SDS((B_HEADS, t, B_DH), bf16), grid=(nb, B_KV, nq),
        in_specs=[pl.BlockSpec((B_GROUP, TQ, B_DH), lambda b, g, i: (g, b * nq + i, 0)),
                  pl.BlockSpec((1, tb, B_DH), lambda b, g, i: (g, b, 0)),
                  pl.BlockSpec((1, tb, LANES), lambda b, g, i: (g, b, 0))],
        out_specs=pl.BlockSpec((B_GROUP, TQ, B_DH), lambda b, g, i: (g, b * nq + i, 0)),
        compiler_params=_cp(("arbitrary", "arbitrary", "arbitrary")), name="attn",
    )(q, k, v)


def _head_rms(y, g, dv):
    parts = []
    for h in range(y.shape[1] // dv):
        yh = y[:, dv * h:dv * (h + 1)]
        parts.append(yh * lax.rsqrt(jnp.mean(yh * yh, axis=-1, keepdims=True) + EPS))
    return jnp.concatenate(parts, axis=1) * g


def _merge_kernel(ha_ref, ao_ref, att_ref, hc_ref, cr_ref, gate_ref, x_ref, g1_ref, sh2_ref, sc2_ref,
                  ag_ref, cg_ref, wb_ref, wo_ref, gn2_ref, wr_ref, br_ref,
                  xo_ref, h2_ref, ti_ref, tw_ref, rk_ref, cnt_ref, cnt_s):
    i = pl.program_id(0)

    @pl.when(i == 0)
    def _():
        cnt_s[...] = jnp.zeros_like(cnt_s)

    d = x_ref.shape[1]
    ya = _head_rms(ha_ref[...].astype(f32), ag_ref[...], A_DV) * _sigmoid(ao_ref[...].astype(f32))
    cr = cr_ref[...].astype(f32)
    yc = _head_rms(hc_ref[...].astype(f32), cg_ref[...], C_DV) * (cr * _sigmoid(cr))
    yb = jnp.concatenate([att_ref[h] for h in range(B_HEADS)], axis=1)
    merged = jnp.zeros((TM, d), f32)
    for n, y in enumerate((ya.astype(bf16), yb, yc.astype(bf16))):
        merged = merged + _sigmoid(gate_ref[:, d * n:d * (n + 1)]) * _dot(y, wb_ref[n])
    x = x_ref[...] + g1_ref[0] * _dot(merged.astype(bf16), wo_ref[...])
    xo_ref[...] = x
    h2 = _rms_mod(x, gn2_ref[...], sh2_ref[0], sc2_ref[0])
    h2_ref[...] = _pack_bf16_pairs(h2)

    logits = _dot_nt(wr_ref[...], h2, HI) + br_ref[...]
    eid = _iota((N_EXPERTS, TM), 0)
    work = logits
    onehot = jnp.zeros((N_EXPERTS, TM), f32)
    vals, sels = [], []
    for k in range(TOP_K):
        mk = jnp.max(work, axis=0, keepdims=True)
        ik = jnp.min(jnp.where(work == mk, eid, N_EXPERTS), axis=0, keepdims=True)
        sel = eid == ik
        work = jnp.where(sel, -jnp.inf, work)
        onehot = onehot + sel.astype(f32)
        ti_ref[k:k + 1, :] = ik
        vals.append(mk)
        sels.append(sel)
    ex = [jnp.exp(vk - vals[0]) for vk in vals]
    tot = ex[0] + ex[1] + ex[2] + ex[3]
    for k in range(TOP_K):
        tw_ref[k:k + 1, :] = ex[k] / tot

    ut = (_iota((TM, TM), 0) <= _iota((TM, TM), 1)).astype(bf16)
    incl = _dot(onehot.astype(bf16), ut)
    rank = cnt_s[...][:, 0:1] + incl - onehot
    for k in range(TOP_K):
        rk_ref[k:k + 1, :] = jnp.sum(jnp.where(sels[k], rank, 0.0), axis=0, keepdims=True).astype(jnp.int32)
    cnt_s[...] = cnt_s[...] + incl[:, TM - 1:TM]
    cnt_ref[...] = cnt_s[...]


def _merge(p, h_a, att, h_c, x, mods, a_norm_g, c_norm_g, w_branch, w_out, g_norm2, w_router, b_router,
           tpb, ctx_tiles, nb):
    t, d = x.shape
    row = lambda w: pl.BlockSpec((TM, w), lambda i: (i, 0))
    const = lambda shape: pl.BlockSpec(shape, lambda i: tuple(0 for _ in shape))
    return pl.pallas_call(
        _merge_kernel,
        out_shape=(SDS((t, d), f32), SDS((t, d // 2), jnp.int32), SDS((TOP_K, t), jnp.int32), SDS((TOP_K, t), f32),
                   SDS((TOP_K, t), jnp.int32), SDS((N_EXPERTS, LANES), f32)),
        grid=(t // TM,),
        in_specs=[row(512), row(512), pl.BlockSpec((B_HEADS, TM, B_DH), lambda i: (0, i, 0)), row(512), row(512),
                  row(3 * d), row(d),
                  _mod_spec(2, tpb, ctx_tiles, nb), _mod_spec(3, tpb, ctx_tiles, nb), _mod_spec(4, tpb, ctx_tiles, nb),
                  const((1, 512)), const((1, 512)), const((3, MIX_W, d)), const((d, d)), const((1, d)),
                  const((N_EXPERTS, d)), const((N_EXPERTS, 1))],
        out_specs=(row(d), row(d // 2), pl.BlockSpec((TOP_K, TM), lambda i: (0, i)), pl.BlockSpec((TOP_K, TM), lambda i: (0, i)),
                   pl.BlockSpec((TOP_K, TM), lambda i: (0, i)), const((N_EXPERTS, LANES))),
        scratch_shapes=[pltpu.VMEM((N_EXPERTS, LANES), f32)],
        compiler_params=_cp(("arbitrary",)), name="merge",
    )(h_a, p["a_o"], att, h_c, p["c_r"], p["gate"], x, mods, mods, mods,
      a_norm_g.reshape(1, -1), c_norm_g.reshape(1, -1), w_branch, w_out, g_norm2.reshape(1, -1),
      w_router.T, b_router.reshape(-1, 1))


def _sc_mesh():
    return plsc.VectorSubcoreMesh(core_axis_name="c", subcore_axis_name="s")


def _sc_scatter_rows(src, idx, n_out):
    v, d = src.shape
    n = idx.shape[0]
    per_w = n // (SC_NUM_CORES * SC_NUM_SUBCORES)
    assert per_w % SC_WINDOW == 0 and v % SC_WINDOW == 0

    @functools.partial(pl.kernel, out_type=SDS((n_out, d), src.dtype), mesh=_sc_mesh(),
                       scratch_types=[pltpu.VMEM((SC_WINDOW,), jnp.int32), pltpu.VMEM((SC_WINDOW, d), src.dtype),
                                      pltpu.SemaphoreType.DMA])
    def k(x_hbm, i_hbm, o_hbm, idx_v, rows_v, sem):
        wid = lax.axis_index("s") * SC_NUM_CORES + lax.axis_index("c")

        @pl.loop(0, per_w // SC_WINDOW)
        def _(j):
            base = wid * per_w + j * SC_WINDOW
            pltpu.sync_copy(i_hbm.at[pl.ds(base, SC_WINDOW)], idx_v)
            pltpu.sync_copy(x_hbm.at[pl.ds(lax.rem(base, v), SC_WINDOW)], rows_v)
            pltpu.async_copy(rows_v, o_hbm.at[idx_v], sem).wait()

    return k(src, idx)


def _sc_gather_rows(table, idx):
    d = table.shape[1]
    n = idx.shape[0]
    per_w = n // (SC_NUM_CORES * SC_NUM_SUBCORES)
    assert per_w % SC_WINDOW == 0

    @functools.partial(pl.kernel, out_type=SDS((n, d), table.dtype), mesh=_sc_mesh(),
                       scratch_types=[pltpu.VMEM((SC_WINDOW,), jnp.int32), pltpu.VMEM((SC_WINDOW, d), table.dtype),
                                      pltpu.SemaphoreType.DMA])
    def k(x_hbm, i_hbm, o_hbm, idx_v, rows_v, sem):
        wid = lax.axis_index("s") * SC_NUM_CORES + lax.axis_index("c")

        @pl.loop(0, per_w // SC_WINDOW)
        def _(j):
            base = wid * per_w + j * SC_WINDOW
            pltpu.sync_copy(i_hbm.at[pl.ds(base, SC_WINDOW)], idx_v)
            pltpu.async_copy(x_hbm.at[idx_v], rows_v, sem).wait()
            pltpu.sync_copy(rows_v, o_hbm.at[pl.ds(base, SC_WINDOW)])

    return k(table, idx)


GU_BLOCK = 2 * LANES


def _deinterleave_perm():
    n = np.arange(GU_BLOCK)
    src = np.where(n < LANES, 2 * n, 2 * (n - LANES) + 1)
    return jnp.asarray(np.arange(GU_BLOCK)[:, None] == src[None, :], bf16)


def _expert_kernel(te_ref, nv_ref, x_ref, w1_ref, b1_ref, w2_ref, b2_ref, perm_ref, y_ref, w1_s, w2_s):
    i = pl.program_id(0)
    valid = i < nv_ref[0]
    new_expert = jnp.logical_or(i == 0, te_ref[i] != te_ref[jnp.maximum(i - 1, 0)])
    dff2 = w1_ref.shape[2]

    @pl.when(jnp.logical_and(valid, new_expert))
    def _():
        for cb in range(dff2 // GU_BLOCK):
            cs = slice(GU_BLOCK * cb, GU_BLOCK * (cb + 1))
            w1_s[:, cs] = _dot(w1_ref[0, :, cs].astype(bf16), perm_ref[...]).astype(bf16)
        w2_s[...] = w2_ref[0].astype(bf16)

    @pl.when(valid)
    def _():
        x = jnp.concatenate(_unpack_bf16_pairs(x_ref[...]), axis=1).astype(bf16)
        gu = _dot(x, w1_s[...]) + b1_ref[0]
        nblk = dff2 // GU_BLOCK
        g = jnp.concatenate([gu[:, GU_BLOCK * cb:GU_BLOCK * cb + LANES] for cb in range(nblk)], axis=1)
        u = jnp.concatenate([gu[:, GU_BLOCK * cb + LANES:GU_BLOCK * (cb + 1)] for cb in range(nblk)], axis=1)
        gate = jnp.minimum(g, SWIGLU_LIMIT)
        up = jnp.clip(u, -SWIGLU_LIMIT, SWIGLU_LIMIT)
        a = (up + 1.0) * gate * _sigmoid(SWIGLU_ALPHA * gate)
        y_ref[...] = _pack_bf16_pairs(_dot(a.astype(bf16), w2_s[...]) + b2_ref[0])


def _experts(xs, tile_e, n_valid, layer, w1, b1, w2, b2):
    p, dw = xs.shape
    d = 2 * dw
    dff2 = w1.shape[-1]
    dff = w2.shape[2]
    row = lambda i, te, nv: (jnp.minimum(i, nv[0] - 1), 0)
    wsel = lambda i, te, nv: (layer, te[i], 0, 0)
    sq = pl.Squeezed()
    return pl.pallas_call(
        _expert_kernel, out_shape=SDS((p, dw), jnp.int32),
        grid_spec=pltpu.PrefetchScalarGridSpec(
            num_scalar_prefetch=2, grid=(p // TME,),
            in_specs=[pl.BlockSpec((TME, dw), row),
                      pl.BlockSpec((sq, 1, d, dff2), wsel), pl.BlockSpec((sq, 1, 1, dff2), wsel),
                      pl.BlockSpec((sq, 1, dff, d), wsel), pl.BlockSpec((sq, 1, 1, d), wsel),
                      pl.BlockSpec((GU_BLOCK, GU_BLOCK), lambda i, te, nv: (0, 0))],
            out_specs=pl.BlockSpec((TME, dw), row),
            scratch_shapes=[pltpu.VMEM((d, dff2), bf16), pltpu.VMEM((dff, d), bf16)]),
        compiler_params=_cp(("arbitrary",)), name="experts",
    )(tile_e, n_valid, xs, w1, b1, w2, b2, _deinterleave_perm())


def _combine_kernel(x_ref, yg_ref, w_ref, g2_ref, *rest, final):
    w = w_ref[...]
    acc_lo = acc_hi = None
    for k in range(TOP_K):
        lo, hi = _unpack_bf16_pairs(yg_ref[k])
        wk = w[:, k:k + 1]
        acc_lo = wk * lo if acc_lo is None else acc_lo + wk * lo
        acc_hi = wk * hi if acc_hi is None else acc_hi + wk * hi
    x = x_ref[...] + g2_ref[0] * jnp.concatenate([acc_lo, acc_hi], axis=1)
    if final:
        gf_ref, o_ref = rest
        x = x * lax.rsqrt(jnp.mean(x * x, axis=-1, keepdims=True) + EPS) * gf_ref[...]
    else:
        (o_ref,) = rest
    o_ref[...] = x


def _combine(x, yg, wcol, mods, tpb, ctx_tiles, nb, g_final=None):
    t, d = x.shape
    final = g_final is not None
    if final:
        lat = tpb - ctx_tiles
        rmap = lambda i: ((i // lat) * tpb + ctx_tiles + i % lat)
        grid = (nb * lat,)
        mod = pl.BlockSpec((1, 1, 1024), lambda i: (i // lat, 0, 5))
        n_out = nb * lat * TM
    else:
        rmap = lambda i: i
        grid = (t // TM,)
        mod = _mod_spec(5, tpb, ctx_tiles, nb)
        n_out = t
    in_specs = [pl.BlockSpec((TM, d), lambda i: (rmap(i), 0)),
                pl.BlockSpec((TOP_K, TM, d // 2), lambda i: (0, rmap(i), 0)),
                pl.BlockSpec((TM, TOP_K), lambda i: (rmap(i), 0)), mod]
    args = [x, yg, wcol, mods]
    if final:
        in_specs.append(pl.BlockSpec((1, d), lambda i: (0, 0)))
        args.append(g_final.reshape(1, -1))
    return pl.pallas_call(
        functools.partial(_combine_kernel, final=final), out_shape=SDS((n_out, d), f32), grid=grid,
        in_specs=in_specs, out_specs=pl.BlockSpec((TM, d), lambda i: (i, 0)),
        compiler_params=_cp(("arbitrary",)), name="combine_final" if final else "combine",
    )(*args)


def _routing_tables(top_i, rank, counts, n_tiles):
    cnt = counts[:, 0].astype(jnp.int32)
    padded = ((cnt + TME - 1) // TME) * TME
    ends = jnp.cumsum(padded)
    starts = ends - padded
    eids = jnp.arange(N_EXPERTS, dtype=jnp.int32)
    start_of = jnp.sum(jnp.where(top_i[..., None] == eids, starts, 0), axis=-1)
    pos = (start_of + rank).reshape(-1)
    n_valid = ends[-1] // TME
    tile_start = jnp.arange(n_tiles, dtype=jnp.int32) * TME
    tile_e = jnp.sum(tile_start[:, None] >= ends[None, :], axis=1).astype(jnp.int32)
    tile_e = jnp.where(jnp.arange(n_tiles) < n_valid, tile_e, tile_e[jnp.maximum(n_valid - 1, 0)])
    tile_e = jnp.minimum(tile_e, N_EXPERTS - 1)
    return pos, tile_e, n_valid.reshape(1).astype(jnp.int32)


def kernel(x, c, ctx, c_ctx, w_ada, b_ada, g_norm1, w_in, a_conv_w, a_conv_b, a_gate_b, a_norm_g, b_q_norm_g, b_k_norm_g, c_w_up, c_b_up, c_norm_g, w_branch, w_out, g_norm2, w_router, b_router, w_e1, b_e1, w_e2, b_e2, g_final):
    nb, seq, d = x.shape
    ctx_len = ctx.shape[1]
    depth = w_ada.shape[0]
    tb = ctx_len + seq
    t = nb * tb
    tpb, ctx_tiles = tb // TM, ctx_len // TM
    assert d == 1024 and nb < 16 and seq % TM == 0 and ctx_len % TM == 0 and ctx_len % TQ == 0
    n_assign = TOP_K * t
    n_sorted = n_assign + N_EXPERTS * TME
    n_tiles = n_sorted // TME

    xs = jnp.concatenate([ctx, x], axis=1).reshape(t, d)
    cc = jnp.zeros((16, d), f32).at[:nb].set(c).at[nb].set(c_ctx)
    mods_all = _adaln(cc, w_ada, b_ada)
    cos, sin = _rope_tables(seq, ctx_len)
    w_in_r = _arrange_w_in(w_in)
    w_branch_b, w_out_b = w_branch.astype(bf16), w_out.astype(bf16)
    col = np.arange(b_e1.shape[-1])
    within = col % GU_BLOCK
    src = (col // GU_BLOCK) * GU_BLOCK + np.where(within < LANES, 2 * within, 2 * (within - LANES) + 1)
    b1 = b_e1[..., src][..., None, :]
    b2 = b_e2[..., None, :]

    out = None
    for l in range(depth):
        mods = mods_all[l].reshape(16, 1, 6 * d)
        p = _inproj(xs, mods, g_norm1[l].reshape(1, -1), w_in_r[l], tpb, ctx_tiles, nb)
        h_a = _mlstm(p["a_qk"], p["a_v"], p["small"], a_conv_w[l], a_conv_b[l], a_gate_b[l], nb, ctx_len)
        h_c = _gla(p["c_qk"], p["c_v"], p["small"], c_w_up[l], c_b_up[l], nb, ctx_len)
        qn, kn, vn = _attn_prep(p["b_q"], p["b_kv"], cos, sin, b_q_norm_g[l], b_k_norm_g[l], tpb)
        att = _attn(qn, kn, vn, nb, ctx_len)
        xs, h2, top_i, top_w, rank, counts = _merge(
            p, h_a, att, h_c, xs, mods, a_norm_g[l], c_norm_g[l], w_branch_b[l], w_out_b[l], g_norm2[l],
            w_router[l], b_router[l], tpb, ctx_tiles, nb)
        pos, tile_e, n_valid = _routing_tables(top_i, rank, counts, n_tiles)
        x_sorted = _sc_scatter_rows(h2, pos, n_sorted)
        y_sorted = _experts(x_sorted, tile_e, n_valid, l, w_e1, b1, w_e2, b2)
        yg = _sc_gather_rows(y_sorted, pos).reshape(TOP_K, t, d // 2)
        if l == depth - 1:
            out = _combine(xs, yg, top_w.T, mods, tpb, ctx_tiles, nb, g_final=g_final)
        else:
            xs = _combine(xs, yg, top_w.T, mods, tpb, ctx_tiles, nb)
    return out.reshape(nb, seq, d)
```

```python
import functools

import jax
import jax.numpy as jnp
import numpy as np
from jax import lax
from jax.experimental import pallas as pl
from jax.experimental.pallas import tpu as pltpu
from jax.experimental.pallas import tpu_sc as plsc

f32 = jnp.float32
bf16 = jnp.bfloat16
HI = lax.Precision.HIGHEST
SDS = jax.ShapeDtypeStruct

EPS = 1e-6
CHUNK = 128
GRID_W = 64
ROPE_THETA = 10000.0
MIX_W = 512
A_HEADS, A_DQK, A_DV = 4, 64, 128
B_HEADS, B_KV, B_DH = 8, 2, 64
B_GROUP = B_HEADS // B_KV
C_HEADS, C_DK, C_DV, C_RANK, C_TAU = 4, 64, 128, 16, 16.0
N_EXPERTS, TOP_K = 32, 4
SWIGLU_LIMIT, SWIGLU_ALPHA = 7.0, 1.702

LANES = 128
SC_NUM_CORES = 2
SC_NUM_SUBCORES = 16
VMEM_LIMIT = 56 * 2 ** 20

TM = 256
TQ = 128
TME = 512
KEY_CHUNK = 2048
SC_WINDOW = 64


def _cp(sem, vmem=VMEM_LIMIT):
    return pltpu.CompilerParams(dimension_semantics=sem, vmem_limit_bytes=vmem)


def _dot(a, b, precision=None):
    return jnp.dot(a, b, preferred_element_type=f32, precision=precision)


def _dot_nt(a, b, precision=None):
    return lax.dot_general(a, b, (((1,), (1,)), ((), ())), preferred_element_type=f32, precision=precision)


def _dot_tn(a, b):
    return lax.dot_general(a, b, (((0,), (0,)), ((), ())), preferred_element_type=f32)


def _sigmoid(x):
    return 0.5 * jnp.tanh(0.5 * x) + 0.5


def _log_sigmoid(x):
    return jnp.minimum(x, 0.0) - jnp.log(1.0 + jnp.exp(-jnp.abs(x)))


def _iota(shape, axis):
    return lax.broadcasted_iota(jnp.int32, shape, axis)


def _pack_bf16_pairs(x):
    n = x.shape[1] // 2
    lo = pltpu.bitcast(x[:, :n].astype(bf16).astype(f32), jnp.uint32)
    hi = pltpu.bitcast(x[:, n:].astype(bf16).astype(f32), jnp.uint32)
    return pltpu.bitcast((lo >> 16) | hi, jnp.int32)


def _unpack_bf16_pairs(w):
    u = pltpu.bitcast(w, jnp.uint32)
    return pltpu.bitcast(u << 16, f32), pltpu.bitcast(u & jnp.uint32(0xFFFF0000), f32)


def _rms_mod(x, g, shift, scale):
    y = x * lax.rsqrt(jnp.mean(x * x, axis=-1, keepdims=True) + EPS) * g
    return y * (1.0 + scale) + shift


def _adaln_kernel(cc_ref, w_ref, b_ref, o_ref):
    cc = cc_ref[...]
    s = cc * _sigmoid(cc)
    o_ref[0] = _dot(s, w_ref[0], HI) + b_ref[0]


def _adaln(cc, w_ada, b_ada):
    nl, d, n = w_ada.shape
    return pl.pallas_call(
        _adaln_kernel, out_shape=SDS((nl, 16, n), f32), grid=(nl, n // 1024),
        in_specs=[pl.BlockSpec((16, d), lambda l, j: (0, 0)),
                  pl.BlockSpec((1, d, 1024), lambda l, j: (l, 0, j)),
                  pl.BlockSpec((1, 1, 1024), lambda l, j: (l, 0, j))],
        out_specs=pl.BlockSpec((1, 16, 1024), lambda l, j: (l, 0, j)),
        compiler_params=_cp(("arbitrary", "arbitrary")), name="adaln",
    )(cc, w_ada, b_ada.reshape(nl, 1, n))


def _mod_spec(col, tpb, ctx_tiles, nb):
    def imap(i):
        return (jnp.where((i % tpb) >= ctx_tiles, i // tpb, nb), 0, col)
    return pl.BlockSpec((1, 1, 1024), imap)


_K1_GROUPS = (("a_qk", 512, bf16), ("a_v", 512, bf16), ("a_o", 512, bf16), ("b_q", 512, bf16),
              ("b_kv", 256, bf16), ("c_qk", 512, bf16), ("c_v", 512, bf16), ("c_r", 512, bf16),
              ("gate", 3072, bf16), ("small", 128, f32))
_K1_WIDTH = sum(w for _, w, _ in _K1_GROUPS)


def _arrange_w_in(w_in):
    z = jnp.zeros(w_in.shape[:-1] + (LANES - 48,), w_in.dtype)
    w = jnp.concatenate([w_in[..., 0:1024], w_in[..., 1040:1552], w_in[..., 1552:2320], w_in[..., 2320:3344],
                         w_in[..., 3376:3888], w_in[..., 3888:6960], w_in[..., 1024:1040], w_in[..., 3344:3376], z],
                        axis=-1)
    assert w.shape[-1] == _K1_WIDTH
    return w.astype(bf16)


_ATTN_GROUPS = ("b_q", "b_kv")
_STORED_GROUPS = tuple(g for g in _K1_GROUPS if g[0] not in _ATTN_GROUPS)


def _weighted_expert_sum(yg_ref, w):
    acc_lo = acc_hi = None
    for k in range(TOP_K):
        lo, hi = _unpack_bf16_pairs(yg_ref[k])
        wk = w[:, k:k + 1]
        acc_lo = wk * lo if acc_lo is None else acc_lo + wk * lo
        acc_hi = wk * hi if acc_hi is None else acc_hi + wk * hi
    return jnp.concatenate([acc_lo, acc_hi], axis=1)


def _inproj_kernel(*refs, fuse_prev):
    refs = list(refs)
    x_ref = refs.pop(0)
    x = x_ref[...]
    if fuse_prev:
        yg_ref, wcol_ref, g2_ref = refs[:3]
        refs = refs[3:]
        x = x + g2_ref[0] * _weighted_expert_sum(yg_ref, wcol_ref[...])
    sh_ref, sc_ref, g_ref, w_ref, cos_ref, sin_ref, qg_ref, kg_ref = refs[:8]
    outs = refs[8:]
    if fuse_prev:
        outs.pop(0)[...] = x
    h = _rms_mod(x, g_ref[...], sh_ref[0], sc_ref[0]).astype(bf16)
    c0 = 0
    attn_in = {}
    stored = iter(outs[:len(_STORED_GROUPS)])
    for name, width, dt in _K1_GROUPS:
        if name in _ATTN_GROUPS:
            attn_in[name] = _dot(h, w_ref[:, c0:c0 + width])
        else:
            o_ref = next(stored)
            for j in range(0, width, 512):
                wj = min(512, width - j)
                o_ref[:, j:j + wj] = _dot(h, w_ref[:, c0 + j:c0 + j + wj]).astype(dt)
        c0 += width
    qo_ref, ko_ref, vo_ref = outs[len(_STORED_GROUPS):]
    _attn_prep(attn_in["b_q"], attn_in["b_kv"], cos_ref[...], sin_ref[...], qg_ref[...], kg_ref[...],
               qo_ref, ko_ref, vo_ref)


def _inproj(x, mods, g, w, cos, sin, qg, kg, tpb, ctx_tiles, nb, prev=None):
    t, d = x.shape
    row = lambda wd: pl.BlockSpec((TM, wd), lambda i: (i, 0))
    const = lambda shape: pl.BlockSpec(shape, lambda i: (0,) * len(shape))
    in_specs, args = [row(d)], [x]
    out_shape, out_specs = [], []
    if prev is not None:
        yg, wcol, mods_prev = prev
        in_specs += [pl.BlockSpec((TOP_K, TM, d // 2), lambda i: (0, i, 0)), row(TOP_K),
                     _mod_spec(5, tpb, ctx_tiles, nb)]
        args += [yg, wcol, mods_prev]
        out_shape.append(SDS((t, d), f32))
        out_specs.append(row(d))
    in_specs += [_mod_spec(0, tpb, ctx_tiles, nb), _mod_spec(1, tpb, ctx_tiles, nb), const((1, d)),
                 pl.BlockSpec((d, _K1_WIDTH), lambda i: (0, 0), pipeline_mode=pl.Buffered(1)),
                 pl.BlockSpec((TM, LANES), lambda i: (i % tpb, 0)), pl.BlockSpec((TM, LANES), lambda i: (i % tpb, 0)),
                 const((1, B_HEADS * B_DH)), const((1, B_KV * B_DH))]
    args += [mods, mods, g, w, cos, sin, jnp.tile(qg, B_HEADS).reshape(1, -1), jnp.tile(kg, B_KV).reshape(1, -1)]
    out_shape += [SDS((t, wd), dt) for _, wd, dt in _STORED_GROUPS]
    out_specs += [row(wd) for _, wd, _ in _STORED_GROUPS]
    out_shape += [SDS((B_HEADS, t, B_DH), bf16), SDS((B_KV, t, B_DH), bf16), SDS((B_KV, t, LANES), bf16)]
    out_specs += [pl.BlockSpec((B_HEADS, TM, B_DH), lambda i: (0, i, 0)),
                  pl.BlockSpec((B_KV, TM, B_DH), lambda i: (0, i, 0)),
                  pl.BlockSpec((B_KV, TM, LANES), lambda i: (0, i, 0))]
    res = list(pl.pallas_call(
        functools.partial(_inproj_kernel, fuse_prev=prev is not None),
        out_shape=tuple(out_shape), grid=(t // TM,), in_specs=in_specs, out_specs=tuple(out_specs),
        compiler_params=_cp(("arbitrary",)), name="inproj",
    )(*args))
    x_new = res.pop(0) if prev is not None else x
    p = dict(zip([n for n, _, _ in _STORED_GROUPS], res))
    return x_new, p, tuple(res[len(_STORED_GROUPS):])


def _bwd_chunk(i, ncc, nch):
    return jnp.where(i < ncc, ncc - 1 - i, nch - 1 + ncc - i)


def _split3(x, axis=0):
    hi = x.astype(bf16)
    r1 = x - hi.astype(f32)
    mid = r1.astype(bf16)
    lo = (r1 - mid.astype(f32)).astype(bf16)
    return jnp.concatenate([hi, mid, lo], axis=axis)


def _mlstm_kernel(qk_ref, v_ref, sm_ref, cw_ref, cb_ref, gb_ref, tri_ref, lmask_ref, exps_ref, expd_ref, expv_ref,
                  oseg_ref, kmask_ref, vmask_ref, cmask_ref, nmask_ref, o_ref, qk_s, hacc, c_s, n_s, *, ctx_len):
    tb = qk_ref.shape[0]
    nch = tb // CHUNK
    ncc = ctx_len // CHUNK
    nu = 2 * A_HEADS

    cw = cw_ref[...]
    cb = cb_ref[...]
    row = _iota((CHUNK, 2 * A_HEADS * A_DQK), 0)
    kscale = jnp.where(_iota((1, 2 * A_HEADS * A_DQK), 1) >= A_HEADS * A_DQK, A_DQK ** -0.5, 1.0)

    def conv_body(c, carry):
        r0 = pl.multiple_of(c * CHUNK, CHUNK)
        x = qk_ref[pl.ds(r0, CHUNK), :].astype(f32)
        rp = pl.multiple_of(jnp.maximum(r0 - 16, 0), 16)
        rn = pl.multiple_of(jnp.minimum(r0 + CHUNK, tb - 16), 16)
        prev = qk_ref[pl.ds(rp, 16), :].astype(f32)[15:16, :]
        nxt = qk_ref[pl.ds(rn, 16), :].astype(f32)[0:1, :]
        has_prev = jnp.logical_and(r0 != 0, r0 != ctx_len)
        has_next = jnp.logical_and(r0 + CHUNK != ctx_len, r0 + CHUNK != tb)
        prev = jnp.where(has_prev, prev, 0.0)
        nxt = jnp.where(has_next, nxt, 0.0)
        xm = jnp.where(row == 0, prev, pltpu.roll(x, 1, 0))
        xp = jnp.where(row == CHUNK - 1, nxt, pltpu.roll(x, CHUNK - 1, 0))
        y = cw[0:1] * xm + cw[1:2] * x + cw[2:3] * xp + cb
        y = y * _sigmoid(y) * kscale
        qk_s[pl.ds(r0, CHUNK), :] = y.astype(bf16)
        return carry

    lax.fori_loop(0, nch, conv_body, 0)

    hacc[...] = jnp.zeros_like(hacc)
    c_s[...] = jnp.zeros_like(c_s)
    n_s[...] = jnp.zeros_like(n_s)

    lane = _iota((CHUNK, LANES), 1)
    row = _iota((CHUNK, LANES), 0)
    is_f = jnp.logical_and(lane >= nu, lane < 2 * nu)
    gb = gb_ref[...]
    neg_inf = jnp.float32(-jnp.inf)
    nk = A_HEADS * A_DQK
    ones_rows = jnp.ones((CHUNK, LANES), bf16)

    def cummax(x, d):
        for sh in [1 << e for e in range(CHUNK.bit_length() - 1)]:
            if d == 0:
                y = jnp.where(row >= sh, pltpu.roll(x, sh, 0), neg_inf)
            else:
                y = jnp.where(row < CHUNK - sh, pltpu.roll(x, CHUNK - sh, 0), neg_inf)
            x = jnp.maximum(x, y)
        return x

    def body(i, ms):
        ms = list(ms)
        pending = []
        for d in range(2):
            c = i if d == 0 else _bwd_chunk(i, ncc, nch)
            r0 = pl.multiple_of(c * CHUNK, CHUNK)
            qk = qk_s[pl.ds(r0, CHUNK), :]
            q, k = qk[:, :nk], qk[:, nk:]
            v = v_ref[pl.ds(r0, CHUNK), :]
            mine = jnp.logical_and(lane >= d * A_HEADS, lane < (d + 1) * A_HEADS)
            keep = lambda x: jnp.where(mine[:x.shape[0]], x, 0.0)
            sm = sm_ref[pl.ds(r0, CHUNK), :] + gb
            pre = jnp.where(is_f, _log_sigmoid(sm), sm)
            b = pltpu.roll(_dot(tri_ref[d], _split3(pre)), LANES - nu, 1)
            g = keep(pre - b)
            cm = cummax(g, d)
            g_t = g.T
            g_row = jnp.concatenate([g_t[d * A_HEADS + h:d * A_HEADS + h + 1, :] for h in range(A_HEADS)], axis=1)
            pm = jnp.exp(jnp.where(lmask_ref[d] > 0, g_row - _dot(_split3(cm, 1), exps_ref[d]), neg_inf))
            kst = jnp.tile(k, (A_HEADS, 1)) * kmask_ref[...]
            s = (_dot_nt(q, kst) * pm).astype(bf16)
            p1 = _dot(s, jnp.tile(v, (A_HEADS, 1)) * vmask_ref[...])
            sden = _dot(s, oseg_ref[d])
            cbd, nbd = c_s[d], n_s[d]
            qc = _dot(q, cbd.astype(bf16))
            qn = _dot(q, nbd.astype(bf16))
            m = ms[d]
            rmax, li = b + cm, b + m
            mj = jnp.maximum(li, rmax)
            alpha, beta = jnp.exp(rmax - mj), jnp.exp(li - mj)
            inv = 1.0 / jnp.maximum(jnp.abs(alpha * sden + beta * qn), jnp.exp(-mj))
            ab = _dot(_split3(jnp.concatenate([keep(alpha * inv), keep(beta * inv)], axis=0), 1), expv_ref[d])
            hout = ab[:CHUNK] * p1 + ab[CHUNK:] * qc
            last = CHUNK - 1 if d == 0 else 0
            b_last, m_new, mloc = b[last:last + 1], mj[last:last + 1], rmax[last:last + 1]
            wl = keep(jnp.exp(b_last - b + pre - mloc))
            kw = (k.astype(f32) * _dot(_split3(wl, 1), expd_ref[d])).astype(bf16)
            dec, gain = keep(jnp.exp(b_last + m - m_new)), keep(jnp.exp(mloc - m_new))
            dg = _dot(_split3(jnp.concatenate([dec, gain, jnp.zeros((6, LANES), f32)], axis=0), 1), expv_ref[d])
            c_new = cbd * dg[0:1] + (_dot_tn(kw, v) * cmask_ref[...]) * dg[1:2]
            n_new = nbd * dec + (_dot_tn(kw, ones_rows) * nmask_ref[d]) * gain
            ms[d] = keep(m_new)
            pending.append((r0, hout, c_new, n_new))
        for d, (r0, hout, c_new, n_new) in enumerate(pending):
            hacc[pl.ds(r0, CHUNK), :] += hout
            c_s[d] = c_new
            n_s[d] = n_new
        return tuple(ms)

    lax.fori_loop(0, nch, body, tuple(jnp.zeros((1, LANES), f32) for _ in range(2)))
    o_ref[...] = hacc[...].astype(bf16)


def _mlstm_constants():
    t = np.arange(CHUNK)
    lane = np.arange(LANES)
    hs = np.arange(A_HEADS * CHUNK)
    hd = np.arange(A_HEADS * A_DQK)
    hv = np.arange(A_HEADS * A_DV)
    tri3, lmask, exps, expd, expv, oseg, nmask = [], [], [], [], [], [], []
    for d in range(2):
        pos = t if d == 0 else CHUNK - 1 - t
        tri = pos[None, :] <= pos[:, None]
        tri3.append(np.tile(tri, (1, 3)))
        lmask.append(np.tile(tri, (1, A_HEADS)))
        unit = lambda idx, width: lane[:, None] == d * A_HEADS + idx[None, :] // width
        exps.append(np.tile(unit(hs, CHUNK), (3, 1)))
        expd.append(np.tile(unit(hd, A_DQK), (3, 1)))
        expv.append(np.tile(unit(hv, A_DV), (3, 1)))
        oseg.append(unit(hs, CHUNK).T)
        nmask.append(unit(hd, A_DQK).T)
    kmask = hs[:, None] // CHUNK == hd[None, :] // A_DQK
    vmask = hs[:, None] // CHUNK == hv[None, :] // A_DV
    cmask = hd[:, None] // A_DQK == hv[None, :] // A_DV
    as_b = lambda x: jnp.asarray(np.stack(x) if isinstance(x, list) else x, bf16)
    return (as_b(tri3), as_b(lmask), as_b(exps), as_b(expd), as_b(expv), as_b(oseg), as_b(kmask), as_b(vmask),
            jnp.asarray(cmask, f32), jnp.asarray(np.stack(nmask), f32))


def _mlstm(a_qk, a_v, small, conv_w, conv_b, gate_b, nb, ctx_len):
    t = a_qk.shape[0]
    tb = t // nb
    gb = jnp.zeros((1, LANES), f32).at[0, :4 * A_HEADS].set(gate_b.reshape(-1))
    consts = _mlstm_constants()
    const_spec = lambda a: pl.BlockSpec(a.shape, lambda b, n=a.ndim: (0,) * n, pipeline_mode=pl.Buffered(1))
    return pl.pallas_call(
        functools.partial(_mlstm_kernel, ctx_len=ctx_len),
        out_shape=SDS((t, MIX_W), bf16), grid=(nb,),
        in_specs=[pl.BlockSpec((tb, 512), lambda b: (b, 0)), pl.BlockSpec((tb, 512), lambda b: (b, 0)),
                  pl.BlockSpec((tb, LANES), lambda b: (b, 0)),
                  pl.BlockSpec((3, 512), lambda b: (0, 0)), pl.BlockSpec((1, 512), lambda b: (0, 0)),
                  pl.BlockSpec((1, LANES), lambda b: (0, 0))] + [const_spec(a) for a in consts],
        out_specs=pl.BlockSpec((tb, MIX_W), lambda b: (b, 0)),
        scratch_shapes=[pltpu.VMEM((tb, 512), bf16), pltpu.VMEM((tb, MIX_W), f32),
                        pltpu.VMEM((2, A_HEADS * A_DQK, A_HEADS * A_DV), f32),
                        pltpu.VMEM((2, A_HEADS * A_DQK, LANES), f32)],
        compiler_params=_cp(("arbitrary",)), name="mlstm",
    )(a_qk, a_v, small, conv_w, conv_b.reshape(1, -1), gb, *consts)


GLA_LEVELS = CHUNK.bit_length() - 1


def _gla_constants():
    t = np.arange(CHUNK)
    tri, gref, lmask = [], [], []
    for d in range(2):
        pos = t if d == 0 else CHUNK - 1 - t
        row_of = np.argsort(pos)
        tri.append(pos[None, :] <= pos[:, None])
        g, m = [], []
        for lvl in range(GLA_LEVELS):
            half = CHUNK >> (lvl + 1)
            ref_pos = (pos // (2 * half)) * (2 * half) + half
            g.append(t[None, :] == row_of[ref_pos][:, None])
            late, early = (pos % (2 * half)) >= half, (pos % (2 * half)) < half
            same = (pos[:, None] // (2 * half)) == (pos[None, :] // (2 * half))
            m.append(same & late[:, None] & early[None, :])
        m.append(t[:, None] == t[None, :])
        gref.append(np.concatenate(g, axis=0))
        lmask.append(np.stack([np.tile(x, (1, C_HEADS)) for x in m]))
    tri3 = np.stack([np.tile(x, (1, 3)) for x in tri])
    gref3 = np.stack([np.tile(x, (1, 3)) for x in gref])
    hs = np.arange(C_HEADS * CHUNK)
    kmask = (hs[:, None] // CHUNK) == (np.arange(C_HEADS * C_DK)[None, :] // C_DK)
    vmask = (hs[:, None] // CHUNK) == (np.arange(C_HEADS * C_DV)[None, :] // C_DV)
    bdm = (np.arange(2 * C_DV)[:, None] // C_DV) == (np.arange(2 * C_DK)[None, :] // C_DK)
    as_b = lambda x: jnp.asarray(x, bf16)
    return as_b(tri3), as_b(gref3), as_b(np.stack(lmask)), as_b(kmask), as_b(vmask), jnp.asarray(bdm, f32)


def _gla_kernel(qk_ref, v_ref, sm_ref, wup_ref, bup_ref, tri_ref, gref_ref, lmask_ref, kmask_ref, vmask_ref, bdm_ref,
                o_ref, hacc, s_s, *, ctx_len):
    tb = qk_ref.shape[0]
    nch = tb // CHUNK
    ncc = ctx_len // CHUNK
    nk = C_HEADS * C_DK
    pw = 2 * C_DK
    pv = 2 * C_DV
    npair = C_HEADS // 2

    hacc[...] = jnp.zeros_like(hacc)
    s_s[...] = jnp.zeros_like(s_s)
    qscale = C_DK ** -0.5

    def body(i, carry):
        for d in range(2):
            c = i if d == 0 else _bwd_chunk(i, ncc, nch)
            r0 = pl.multiple_of(c * CHUNK, CHUNK)
            sm = sm_ref[pl.ds(r0, CHUNK), :]
            la = _log_sigmoid(_dot(sm, wup_ref[d], HI) + bup_ref[d]) * (1.0 / C_TAU)
            qk = qk_ref[pl.ds(r0, CHUNK), :]
            q = qk[:, :nk].astype(f32) * qscale
            k = qk[:, nk:].astype(f32)
            v = v_ref[pl.ds(r0, CHUNK), :]
            b = _dot(tri_ref[d], _split3(la))
            bref = _dot(gref_ref[d], _split3(b))

            def scores(qe, ke):
                kst = jnp.tile(ke.astype(bf16), (C_HEADS, 1)) * kmask_ref[...]
                return _dot_nt(qe.astype(bf16), kst)

            a = scores(q, k) * lmask_ref[d, GLA_LEVELS]
            for lvl in range(GLA_LEVELS):
                rel = b - bref[CHUNK * lvl:CHUNK * (lvl + 1)]
                s = scores(q * jnp.exp(jnp.minimum(rel, 0.0)), k * jnp.exp(jnp.minimum(-rel, 0.0)))
                a = a + s * lmask_ref[d, lvl]
            vst = jnp.tile(v, (C_HEADS, 1)) * vmask_ref[...]
            o = _dot(a.astype(bf16), vst)

            last = CHUNK - 1 if d == 0 else 0
            bl = b[last:last + 1]
            qd = (q * jnp.exp(b)).astype(bf16)
            kd = (k * jnp.exp(bl - b)).astype(bf16)
            dec = jnp.exp(bl)
            o_int = []
            for p in range(npair):
                u = d * npair + p
                st = s_s[u]
                o_int.append(_dot_nt(qd[:, pw * p:pw * (p + 1)], st.astype(bf16)))
                upd = _dot_tn(v[:, pv * p:pv * (p + 1)], kd[:, pw * p:pw * (p + 1)])
                s_s[u] = st * dec[:, pw * p:pw * (p + 1)] + bdm_ref[...] * upd
            hacc[pl.ds(r0, CHUNK), :] += o + jnp.concatenate(o_int, axis=1)
        return carry

    lax.fori_loop(0, nch, body, 0)
    o_ref[...] = hacc[...].astype(bf16)


def _gla(c_qk, c_v, small, w_up, b_up, nb, ctx_len):
    t = c_qk.shape[0]
    tb = t // nb
    nk = C_HEADS * C_DK
    wz = jnp.zeros((2, LANES, nk), f32)
    wz = wz.at[0, 16:16 + C_RANK].set(w_up[0]).at[1, 16 + C_RANK:16 + 2 * C_RANK].set(w_up[1])
    consts = _gla_constants()
    const_spec = lambda a: pl.BlockSpec(a.shape, lambda b, n=a.ndim: (0,) * n, pipeline_mode=pl.Buffered(1))
    return pl.pallas_call(
        functools.partial(_gla_kernel, ctx_len=ctx_len),
        out_shape=SDS((t, MIX_W), bf16), grid=(nb,),
        in_specs=[pl.BlockSpec((tb, 512), lambda b: (b, 0)), pl.BlockSpec((tb, 512), lambda b: (b, 0)),
                  pl.BlockSpec((tb, LANES), lambda b: (b, 0)),
                  pl.BlockSpec((2, LANES, nk), lambda b: (0, 0, 0)), pl.BlockSpec((2, 1, nk), lambda b: (0, 0, 0))]
                 + [const_spec(a) for a in consts],
        out_specs=pl.BlockSpec((tb, MIX_W), lambda b: (b, 0)),
        scratch_shapes=[pltpu.VMEM((tb, MIX_W), f32), pltpu.VMEM((C_HEADS, 2 * C_DV, 2 * C_DK), f32)],
        compiler_params=_cp(("arbitrary",)), name="gla",
    )(c_qk, c_v, small, wz, b_up.reshape(2, 1, nk), *consts)


def _rope_tables(seq, ctx_len):
    n_f = B_DH // 4
    t = np.arange(seq)
    freqs = ROPE_THETA ** (-np.arange(n_f, dtype=np.float32) / n_f)
    hd = np.arange(B_DH)
    pos = np.where(hd[None, :] < B_DH // 2, (t // GRID_W)[:, None], (t % GRID_W)[:, None]).astype(np.float32)
    ang = jnp.asarray(pos * freqs[hd % n_f][None, :], f32)
    sign = np.where((hd % (2 * n_f)) < n_f, -1.0, 1.0).astype(np.float32)
    cos = jnp.concatenate([jnp.ones((ctx_len, B_DH), f32), jnp.cos(ang)], axis=0)
    sin = jnp.concatenate([jnp.zeros((ctx_len, B_DH), f32), jnp.sin(ang) * sign[None, :]], axis=0)
    return jnp.tile(cos, (1, 2)), jnp.tile(sin, (1, 2))


def _attn_prep(bq, bkv, cos, sin, qg, kg, qo_ref, ko_ref, vo_ref):
    def norm_rope(x, g):
        w = x.shape[1]
        bd = (_iota((w, w), 0) // B_DH == _iota((w, w), 1) // B_DH).astype(bf16)
        ss = _dot((x * x).astype(bf16), bd)
        xn = x * lax.rsqrt(ss * (1.0 / B_DH) + EPS) * g
        first = (_iota(x.shape, 1) % (B_DH // 2)) < (B_DH // 4)
        swapped = jnp.where(first, pltpu.roll(xn, w - B_DH // 4, 1), pltpu.roll(xn, B_DH // 4, 1))
        reps = w // LANES
        return xn * jnp.tile(cos, (1, reps)) + swapped * jnp.tile(sin, (1, reps))

    q = (norm_rope(bq, qg) * (B_DH ** -0.5)).astype(bf16)
    for h in range(B_HEADS):
        qo_ref[h] = q[:, B_DH * h:B_DH * (h + 1)]
    k = norm_rope(bkv[:, :B_KV * B_DH], kg).astype(bf16)
    v = bkv[:, B_KV * B_DH:].astype(bf16)
    ones_col = (_iota((bkv.shape[0], LANES - B_DH), 1) == 0).astype(bf16)
    for h in range(B_KV):
        ko_ref[h] = k[:, B_DH * h:B_DH * (h + 1)]
        vo_ref[h] = jnp.concatenate([v[:, B_DH * h:B_DH * (h + 1)], ones_col], axis=1)


def _attn_kernel(q_ref, k_ref, v_ref, o_ref, *, ctx_len):
    tb = k_ref.shape[1]
    q = q_ref[...].reshape(B_GROUP * TQ, B_DH)

    def attend(klen):
        starts = [0] + list(range(ctx_len, klen, KEY_CHUNK))
        m = acc = None
        for s0, s1 in zip(starts, starts[1:] + [klen]):
            s = _dot_nt(q, k_ref[0, s0:s1, :])
            smax = jnp.max(s, axis=-1, keepdims=True)
            if m is None:
                m = smax
                acc = _dot(jnp.exp((s - m).astype(bf16)), v_ref[0, s0:s1, :])
            else:
                m_new = jnp.maximum(m, smax)
                acc = jnp.exp(m - m_new) * acc + _dot(jnp.exp((s - m_new).astype(bf16)), v_ref[0, s0:s1, :])
                m = m_new
        o = acc[:, :B_DH] / acc[:, B_DH:B_DH + 1]
        o_ref[...] = o.reshape(B_GROUP, TQ, B_DH).astype(bf16)

    is_ctx = pl.program_id(2) < ctx_len // TQ

    @pl.when(is_ctx)
    def _():
        attend(ctx_len)

    @pl.when(jnp.logical_not(is_ctx))
    def _():
        attend(tb)


def _attn(q, k, v, nb, ctx_len):
    t = q.shape[1]
    tb = t // nb
    nq = tb // TQ
    return pl.pallas_call(
        functools.partial(_attn_kernel, ctx_len=ctx_len),
        out_shape=SDS((B_HEADS, t, B_DH), bf16), grid=(nb, B_KV, nq),
        in_specs=[pl.BlockSpec((B_GROUP, TQ, B_DH), lambda b, g, i: (g, b * nq + i, 0)),
                  pl.BlockSpec((1, tb, B_DH), lambda b, g, i: (g, b, 0)),
                  pl.BlockSpec((1, tb, LANES), lambda b, g, i: (g, b, 0))],
        out_specs=pl.BlockSpec((B_GROUP, TQ, B_DH), lambda b, g, i: (g, b * nq + i, 0)),
        compiler_params=_cp(("arbitrary", "arbitrary", "arbitrary")), name="attn",
    )(q, k, v)


def _head_rms(y, g, dv):
    parts = []
    for h in range(y.shape[1] // dv):
        yh = y[:, dv * h:dv * (h + 1)]
        parts.append(yh * lax.rsqrt(jnp.mean(yh * yh, axis=-1, keepdims=True) + EPS))
    return jnp.concatenate(parts, axis=1) * g


def _merge_kernel(ha_ref, ao_ref, att_ref, hc_ref, cr_ref, gate_ref, x_ref, g1_ref, sh2_ref, sc2_ref,
                  ag_ref, cg_ref, wb_ref, wo_ref, gn2_ref, wr_ref, br_ref,
                  xo_ref, h2_ref, ti_ref, tw_ref, rk_ref, cnt_ref, cnt_s):
    i = pl.program_id(0)

    @pl.when(i == 0)
    def _():
        cnt_s[...] = jnp.zeros_like(cnt_s)

    d = x_ref.shape[1]
    ya = _head_rms(ha_ref[...].astype(f32), ag_ref[...], A_DV) * _sigmoid(ao_ref[...].astype(f32))
    cr = cr_ref[...].astype(f32)
    yc = _head_rms(hc_ref[...].astype(f32), cg_ref[...], C_DV) * (cr * _sigmoid(cr))
    yb = jnp.concatenate([att_ref[h] for h in range(B_HEADS)], axis=1)
    merged = jnp.zeros((TM, d), f32)
    for n, y in enumerate((ya.astype(bf16), yb, yc.astype(bf16))):
        merged = merged + _sigmoid(gate_ref[:, d * n:d * (n + 1)]) * _dot(y, wb_ref[n])
    x = x_ref[...] + g1_ref[0] * _dot(merged.astype(bf16), wo_ref[...])
    xo_ref[...] = x
    h2 = _rms_mod(x, gn2_ref[...], sh2_ref[0], sc2_ref[0])
    h2_ref[...] = _pack_bf16_pairs(h2)

    logits = _dot_nt(wr_ref[...], h2, HI) + br_ref[...]
    eid = _iota((N_EXPERTS, TM), 0)
    work = logits
    onehot = jnp.zeros((N_EXPERTS, TM), f32)
    vals, sels = [], []
    for k in range(TOP_K):
        mk = jnp.max(work, axis=0, keepdims=True)
        ik = jnp.min(jnp.where(work == mk, eid, N_EXPERTS), axis=0, keepdims=True)
        sel = eid == ik
        work = jnp.where(sel, -jnp.inf, work)
        onehot = onehot + sel.astype(f32)
        ti_ref[k:k + 1, :] = ik
        vals.append(mk)
        sels.append(sel)
    ex = [jnp.exp(vk - vals[0]) for vk in vals]
    tot = ex[0] + ex[1] + ex[2] + ex[3]
    for k in range(TOP_K):
        tw_ref[k:k + 1, :] = ex[k] / tot

    ut = (_iota((TM, TM), 0) <= _iota((TM, TM), 1)).astype(bf16)
    incl = _dot(onehot.astype(bf16), ut)
    rank = cnt_s[...][:, 0:1] + incl - onehot
    for k in range(TOP_K):
        rk_ref[k:k + 1, :] = jnp.sum(jnp.where(sels[k], rank, 0.0), axis=0, keepdims=True).astype(jnp.int32)
    cnt_s[...] = cnt_s[...] + incl[:, TM - 1:TM]
    cnt_ref[...] = cnt_s[...]


def _merge(p, h_a, att, h_c, x, mods, a_norm_g, c_norm_g, w_branch, w_out, g_norm2, w_router, b_router,
           tpb, ctx_tiles, nb):
    t, d = x.shape
    row = lambda w: pl.BlockSpec((TM, w), lambda i: (i, 0))
    const = lambda shape: pl.BlockSpec(shape, lambda i: tuple(0 for _ in shape))
    return pl.pallas_call(
        _merge_kernel,
        out_shape=(SDS((t, d), f32), SDS((t, d // 2), jnp.int32), SDS((TOP_K, t), jnp.int32), SDS((TOP_K, t), f32),
                   SDS((TOP_K, t), jnp.int32), SDS((N_EXPERTS, LANES), f32)),
        grid=(t // TM,),
        in_specs=[row(512), row(512), pl.BlockSpec((B_HEADS, TM, B_DH), lambda i: (0, i, 0)), row(512), row(512),
                  row(3 * d), row(d),
                  _mod_spec(2, tpb, ctx_tiles, nb), _mod_spec(3, tpb, ctx_tiles, nb), _mod_spec(4, tpb, ctx_tiles, nb),
                  const((1, 512)), const((1, 512)), const((3, MIX_W, d)), const((d, d)), const((1, d)),
                  const((N_EXPERTS, d)), const((N_EXPERTS, 1))],
        out_specs=(row(d), row(d // 2), pl.BlockSpec((TOP_K, TM), lambda i: (0, i)), pl.BlockSpec((TOP_K, TM), lambda i: (0, i)),
                   pl.BlockSpec((TOP_K, TM), lambda i: (0, i)), const((N_EXPERTS, LANES))),
        scratch_shapes=[pltpu.VMEM((N_EXPERTS, LANES), f32)],
        compiler_params=_cp(("arbitrary",)), name="merge",
    )(h_a, p["a_o"], att, h_c, p["c_r"], p["gate"], x, mods, mods, mods,
      a_norm_g.reshape(1, -1), c_norm_g.reshape(1, -1), w_branch, w_out, g_norm2.reshape(1, -1),
      w_router.T, b_router.reshape(-1, 1))


def _sc_mesh():
    return plsc.VectorSubcoreMesh(core_axis_name="c", subcore_axis_name="s")


def _sc_scatter_rows(src, idx, n_out):
    v, d = src.shape
    reps = idx.shape[0] // v
    per_w = v // (SC_NUM_CORES * SC_NUM_SUBCORES)
    assert per_w % SC_WINDOW == 0 and idx.shape[0] == reps * v

    @functools.partial(pl.kernel, out_type=SDS((n_out, d), src.dtype), mesh=_sc_mesh(),
                       scratch_types=[pltpu.VMEM((SC_WINDOW,), jnp.int32), pltpu.VMEM((SC_WINDOW, d), src.dtype),
                                      pltpu.SemaphoreType.DMA])
    def k(x_hbm, i_hbm, o_hbm, idx_v, rows_v, sem):
        wid = lax.axis_index("s") * SC_NUM_CORES + lax.axis_index("c")

        @pl.loop(0, per_w // SC_WINDOW)
        def _(j):
            base = wid * per_w + j * SC_WINDOW
            pltpu.sync_copy(x_hbm.at[pl.ds(base, SC_WINDOW)], rows_v)
            for r in range(reps):
                pltpu.sync_copy(i_hbm.at[pl.ds(r * v + base, SC_WINDOW)], idx_v)
                pltpu.async_copy(rows_v, o_hbm.at[idx_v], sem).wait()

    return k(src, idx)


def _sc_gather_rows(table, idx):
    d = table.shape[1]
    n = idx.shape[0]
    per_w = n // (SC_NUM_CORES * SC_NUM_SUBCORES)
    assert per_w % SC_WINDOW == 0

    @functools.partial(pl.kernel, out_type=SDS((n, d), table.dtype), mesh=_sc_mesh(),
                       scratch_types=[pltpu.VMEM((SC_WINDOW,), jnp.int32), pltpu.VMEM((SC_WINDOW, d), table.dtype),
                                      pltpu.SemaphoreType.DMA])
    def k(x_hbm, i_hbm, o_hbm, idx_v, rows_v, sem):
        wid = lax.axis_index("s") * SC_NUM_CORES + lax.axis_index("c")

        @pl.loop(0, per_w // SC_WINDOW)
        def _(j):
            base = wid * per_w + j * SC_WINDOW
            pltpu.sync_copy(i_hbm.at[pl.ds(base, SC_WINDOW)], idx_v)
            pltpu.async_copy(x_hbm.at[idx_v], rows_v, sem).wait()
            pltpu.sync_copy(rows_v, o_hbm.at[pl.ds(base, SC_WINDOW)])

    return k(table, idx)


GU_BLOCK = 2 * LANES


def _deinterleave_perm():
    n = np.arange(GU_BLOCK)
    src = np.where(n < LANES, 2 * n, 2 * (n - LANES) + 1)
    return jnp.asarray(np.arange(GU_BLOCK)[:, None] == src[None, :], bf16)


def _expert_kernel(te_ref, nv_ref, x_ref, w1_ref, b1_ref, w2_ref, b2_ref, perm_ref, y_ref, w1_s, w2_s):
    i = pl.program_id(0)
    valid = i < nv_ref[0]
    new_expert = jnp.logical_or(i == 0, te_ref[i] != te_ref[jnp.maximum(i - 1, 0)])
    dff2 = w1_ref.shape[2]

    @pl.when(jnp.logical_and(valid, new_expert))
    def _():
        for cb in range(dff2 // GU_BLOCK):
            cs = slice(GU_BLOCK * cb, GU_BLOCK * (cb + 1))
            w1_s[:, cs] = _dot(w1_ref[0, :, cs].astype(bf16), perm_ref[...]).astype(bf16)
        w2_s[...] = w2_ref[0].astype(bf16)

    @pl.when(valid)
    def _():
        x = jnp.concatenate(_unpack_bf16_pairs(x_ref[...]), axis=1).astype(bf16)
        gu = _dot(x, w1_s[...]) + b1_ref[0]
        nblk = dff2 // GU_BLOCK
        g = jnp.concatenate([gu[:, GU_BLOCK * cb:GU_BLOCK * cb + LANES] for cb in range(nblk)], axis=1)
        u = jnp.concatenate([gu[:, GU_BLOCK * cb + LANES:GU_BLOCK * (cb + 1)] for cb in range(nblk)], axis=1)
        gate = jnp.minimum(g, SWIGLU_LIMIT)
        up = jnp.clip(u, -SWIGLU_LIMIT, SWIGLU_LIMIT)
        a = (up + 1.0) * gate * _sigmoid(SWIGLU_ALPHA * gate)
        y_ref[...] = _pack_bf16_pairs(_dot(a.astype(bf16), w2_s[...]) + b2_ref[0])


def _experts(xs, tile_e, n_valid, layer, w1, b1, w2, b2):
    p, dw = xs.shape
    d = 2 * dw
    dff2 = w1.shape[-1]
    dff = w2.shape[2]
    row = lambda i, te, nv: (jnp.minimum(i, nv[0] - 1), 0)
    wsel = lambda i, te, nv: (layer, te[i], 0, 0)
    sq = pl.Squeezed()
    return pl.pallas_call(
        _expert_kernel, out_shape=SDS((p, dw), jnp.int32),
        grid_spec=pltpu.PrefetchScalarGridSpec(
            num_scalar_prefetch=2, grid=(p // TME,),
            in_specs=[pl.BlockSpec((TME, dw), row),
                      pl.BlockSpec((sq, 1, d, dff2), wsel), pl.BlockSpec((sq, 1, 1, dff2), wsel),
                      pl.BlockSpec((sq, 1, dff, d), wsel), pl.BlockSpec((sq, 1, 1, d), wsel),
                      pl.BlockSpec((GU_BLOCK, GU_BLOCK), lambda i, te, nv: (0, 0))],
            out_specs=pl.BlockSpec((TME, dw), row),
            scratch_shapes=[pltpu.VMEM((d, dff2), bf16), pltpu.VMEM((dff, d), bf16)]),
        compiler_params=_cp(("arbitrary",)), name="experts",
    )(tile_e, n_valid, xs, w1, b1, w2, b2, _deinterleave_perm())


def _combine_kernel(x_ref, yg_ref, w_ref, g2_ref, *rest, final):
    x = x_ref[...] + g2_ref[0] * _weighted_expert_sum(yg_ref, w_ref[...])
    if final:
        gf_ref, o_ref = rest
        x = x * lax.rsqrt(jnp.mean(x * x, axis=-1, keepdims=True) + EPS) * gf_ref[...]
    else:
        (o_ref,) = rest
    o_ref[...] = x


def _combine(x, yg, wcol, mods, tpb, ctx_tiles, nb, g_final=None):
    t, d = x.shape
    final = g_final is not None
    if final:
        lat = tpb - ctx_tiles
        rmap = lambda i: ((i // lat) * tpb + ctx_tiles + i % lat)
        grid = (nb * lat,)
        mod = pl.BlockSpec((1, 1, 1024), lambda i: (i // lat, 0, 5))
        n_out = nb * lat * TM
    else:
        rmap = lambda i: i
        grid = (t // TM,)
        mod = _mod_spec(5, tpb, ctx_tiles, nb)
        n_out = t
    in_specs = [pl.BlockSpec((TM, d), lambda i: (rmap(i), 0)),
                pl.BlockSpec((TOP_K, TM, d // 2), lambda i: (0, rmap(i), 0)),
                pl.BlockSpec((TM, TOP_K), lambda i: (rmap(i), 0)), mod]
    args = [x, yg, wcol, mods]
    if final:
        in_specs.append(pl.BlockSpec((1, d), lambda i: (0, 0)))
        args.append(g_final.reshape(1, -1))
    return pl.pallas_call(
        functools.partial(_combine_kernel, final=final), out_shape=SDS((n_out, d), f32), grid=grid,
        in_specs=in_specs, out_specs=pl.BlockSpec((TM, d), lambda i: (i, 0)),
        compiler_params=_cp(("arbitrary",)), name="combine_final" if final else "combine",
    )(*args)


def _routing_tables(top_i, rank, counts, n_tiles):
    cnt = counts[:, 0].astype(jnp.int32)
    padded = ((cnt + TME - 1) // TME) * TME
    ends = jnp.cumsum(padded)
    starts = ends - padded
    eids = jnp.arange(N_EXPERTS, dtype=jnp.int32)
    start_of = jnp.sum(jnp.where(top_i[..., None] == eids, starts, 0), axis=-1)
    pos = (start_of + rank).reshape(-1)
    n_valid = ends[-1] // TME
    tile_start = jnp.arange(n_tiles, dtype=jnp.int32) * TME
    tile_e = jnp.sum(tile_start[:, None] >= ends[None, :], axis=1).astype(jnp.int32)
    tile_e = jnp.where(jnp.arange(n_tiles) < n_valid, tile_e, tile_e[jnp.maximum(n_valid - 1, 0)])
    tile_e = jnp.minimum(tile_e, N_EXPERTS - 1)
    return pos, tile_e, n_valid.reshape(1).astype(jnp.int32)


def kernel(x, c, ctx, c_ctx, w_ada, b_ada, g_norm1, w_in, a_conv_w, a_conv_b, a_gate_b, a_norm_g, b_q_norm_g, b_k_norm_g, c_w_up, c_b_up, c_norm_g, w_branch, w_out, g_norm2, w_router, b_router, w_e1, b_e1, w_e2, b_e2, g_final):
    nb, seq, d = x.shape
    ctx_len = ctx.shape[1]
    depth = w_ada.shape[0]
    tb = ctx_len + seq
    t = nb * tb
    tpb, ctx_tiles = tb // TM, ctx_len // TM
    assert d == 1024 and nb < 16 and seq % TM == 0 and ctx_len % TM == 0 and ctx_len % TQ == 0
    n_assign = TOP_K * t
    n_sorted = n_assign + N_EXPERTS * TME
    n_tiles = n_sorted // TME

    xs = jnp.concatenate([ctx, x], axis=1).reshape(t, d)
    cc = jnp.zeros((16, d), f32).at[:nb].set(c).at[nb].set(c_ctx)
    mods_all = _adaln(cc, w_ada, b_ada)
    cos, sin = _rope_tables(seq, ctx_len)
    w_in_r = _arrange_w_in(w_in)
    w_branch_b, w_out_b = w_branch.astype(bf16), w_out.astype(bf16)
    col = np.arange(b_e1.shape[-1])
    within = col % GU_BLOCK
    src = (col // GU_BLOCK) * GU_BLOCK + np.where(within < LANES, 2 * within, 2 * (within - LANES) + 1)
    b1 = b_e1[..., src][..., None, :]
    b2 = b_e2[..., None, :]

    out = None
    prev = None
    for l in range(depth):
        mods = mods_all[l].reshape(16, 1, 6 * d)
        xs, p, (qn, kn, vn) = _inproj(xs, mods, g_norm1[l].reshape(1, -1), w_in_r[l], cos, sin,
                                      b_q_norm_g[l], b_k_norm_g[l], tpb, ctx_tiles, nb, prev=prev)
        h_a = _mlstm(p["a_qk"], p["a_v"], p["small"], a_conv_w[l], a_conv_b[l], a_gate_b[l], nb, ctx_len)
        h_c = _gla(p["c_qk"], p["c_v"], p["small"], c_w_up[l], c_b_up[l], nb, ctx_len)
        att = _attn(qn, kn, vn, nb, ctx_len)
        xs, h2, top_i, top_w, rank, counts = _merge(
            p, h_a, att, h_c, xs, mods, a_norm_g[l], c_norm_g[l], w_branch_b[l], w_out_b[l], g_norm2[l],
            w_router[l], b_router[l], tpb, ctx_tiles, nb)
        pos, tile_e, n_valid = _routing_tables(top_i, rank, counts, n_tiles)
        x_sorted = _sc_scatter_rows(h2, pos, n_sorted)
        y_sorted = _experts(x_sorted, tile_e, n_valid, l, w_e1, b1, w_e2, b2)
        yg = _sc_gather_rows(y_sorted, pos).reshape(TOP_K, t, d // 2)
        if l == depth - 1:
            out = _combine(xs, yg, top_w.T, mods, tpb, ctx_tiles, nb, g_final=g_final)
        else:
            prev = (yg, top_w.T, mods)
    return out.reshape(nb, seq, d)
```

```python
import functools

import jax
import jax.numpy as jnp
import numpy as np
from jax import lax
from jax.experimental import pallas as pl
from jax.experimental.pallas import tpu as pltpu
from jax.experimental.pallas import tpu_sc as plsc

f32 = jnp.float32
bf16 = jnp.bfloat16
HI = lax.Precision.HIGHEST
SDS = jax.ShapeDtypeStruct

EPS = 1e-6
CHUNK = 128
GRID_W = 64
ROPE_THETA = 10000.0
MIX_W = 512
A_HEADS, A_DQK, A_DV = 4, 64, 128
B_HEADS, B_KV, B_DH = 8, 2, 64
B_GROUP = B_HEADS // B_KV
C_HEADS, C_DK, C_DV, C_RANK, C_TAU = 4, 64, 128, 16, 16.0
N_EXPERTS, TOP_K = 32, 4
SWIGLU_LIMIT, SWIGLU_ALPHA = 7.0, 1.702

LANES = 128
SC_NUM_CORES = 2
SC_NUM_SUBCORES = 16
VMEM_LIMIT = 56 * 2 ** 20

TMH = 256
TM = 2 * TMH
TQ = 128
TME = 512
KEY_CHUNK = 2048
SC_WINDOW = 64


def _cp(sem, vmem=VMEM_LIMIT):
    return pltpu.CompilerParams(dimension_semantics=sem, vmem_limit_bytes=vmem)


def _dot(a, b, precision=None):
    return jnp.dot(a, b, preferred_element_type=f32, precision=precision)


def _dot_nt(a, b, precision=None):
    return lax.dot_general(a, b, (((1,), (1,)), ((), ())), preferred_element_type=f32, precision=precision)


def _dot_tn(a, b):
    return lax.dot_general(a, b, (((0,), (0,)), ((), ())), preferred_element_type=f32)


def _sigmoid(x):
    return 0.5 * jnp.tanh(0.5 * x) + 0.5


def _log_sigmoid(x):
    return jnp.minimum(x, 0.0) - jnp.log(1.0 + jnp.exp(-jnp.abs(x)))


def _iota(shape, axis):
    return lax.broadcasted_iota(jnp.int32, shape, axis)


def _pack_bf16_pairs(x):
    n = x.shape[1] // 2
    lo = pltpu.bitcast(x[:, :n].astype(bf16).astype(f32), jnp.uint32)
    hi = pltpu.bitcast(x[:, n:].astype(bf16).astype(f32), jnp.uint32)
    return pltpu.bitcast((lo >> 16) | hi, jnp.int32)


def _unpack_bf16_pairs(w):
    u = pltpu.bitcast(w, jnp.uint32)
    return pltpu.bitcast(u << 16, f32), pltpu.bitcast(u & jnp.uint32(0xFFFF0000), f32)


def _rms_mod(x, g, shift, scale):
    y = x * lax.rsqrt(jnp.mean(x * x, axis=-1, keepdims=True) + EPS) * g
    return y * (1.0 + scale) + shift


def _adaln_kernel(cc_ref, w_ref, b_ref, o_ref):
    cc = cc_ref[...]
    s = cc * _sigmoid(cc)
    o_ref[0] = _dot(s, w_ref[0], HI) + b_ref[0]


def _adaln(cc, w_ada, b_ada):
    nl, d, n = w_ada.shape
    return pl.pallas_call(
        _adaln_kernel, out_shape=SDS((nl, 16, n), f32), grid=(nl, n // 1024),
        in_specs=[pl.BlockSpec((16, d), lambda l, j: (0, 0)),
                  pl.BlockSpec((1, d, 1024), lambda l, j: (l, 0, j)),
                  pl.BlockSpec((1, 1, 1024), lambda l, j: (l, 0, j))],
        out_specs=pl.BlockSpec((1, 16, 1024), lambda l, j: (l, 0, j)),
        compiler_params=_cp(("arbitrary", "arbitrary")), name="adaln",
    )(cc, w_ada, b_ada.reshape(nl, 1, n))


def _mod_specs(col, tpb, ctx_tiles, nb):
    def spec(half):
        def imap(i):
            u = 2 * i + half
            return (jnp.where((u % tpb) >= ctx_tiles, u // tpb, nb), 0, col)
        return pl.BlockSpec((1, 1, 1024), imap)
    return [spec(0), spec(1)]


def _rows(a_ref, b_ref):
    a, b = a_ref[0], b_ref[0]
    return jnp.concatenate([jnp.broadcast_to(a, (TMH, a.shape[1])), jnp.broadcast_to(b, (TMH, b.shape[1]))], axis=0)


_K1_GROUPS = (("a_qk", 512, bf16), ("a_v", 512, bf16), ("a_o", 512, bf16), ("b_q", 512, bf16),
              ("b_kv", 256, bf16), ("c_qk", 512, bf16), ("c_v", 512, bf16), ("c_r", 512, bf16),
              ("gate", 3072, bf16), ("small", 128, f32))
_K1_WIDTH = sum(w for _, w, _ in _K1_GROUPS)


def _arrange_w_in(w_in):
    z = jnp.zeros(w_in.shape[:-1] + (LANES - 48,), w_in.dtype)
    w = jnp.concatenate([w_in[..., 0:1024], w_in[..., 1040:1552], w_in[..., 1552:2320], w_in[..., 2320:3344],
                         w_in[..., 3376:3888], w_in[..., 3888:6960], w_in[..., 1024:1040], w_in[..., 3344:3376], z],
                        axis=-1)
    assert w.shape[-1] == _K1_WIDTH
    return w.astype(bf16)


_ATTN_GROUPS = ("b_q", "b_kv")
_STORED_GROUPS = tuple(g for g in _K1_GROUPS if g[0] not in _ATTN_GROUPS)


def _weighted_expert_sum(yg_ref, w):
    acc_lo = acc_hi = None
    for k in range(TOP_K):
        lo, hi = _unpack_bf16_pairs(yg_ref[k])
        wk = w[:, k:k + 1]
        acc_lo = wk * lo if acc_lo is None else acc_lo + wk * lo
        acc_hi = wk * hi if acc_hi is None else acc_hi + wk * hi
    return jnp.concatenate([acc_lo, acc_hi], axis=1)


def _inproj_kernel(*refs, fuse_prev):
    refs = list(refs)
    x_ref = refs.pop(0)
    x = x_ref[...]
    if fuse_prev:
        yg_ref, wcol_ref, g2a_ref, g2b_ref = refs[:4]
        refs = refs[4:]
        x = x + _rows(g2a_ref, g2b_ref) * _weighted_expert_sum(yg_ref, wcol_ref[...])
    sha_ref, shb_ref, sca_ref, scb_ref, g_ref, w_ref, cosa_ref, cosb_ref, sina_ref, sinb_ref, qg_ref, kg_ref = refs[:12]
    outs = refs[12:]
    if fuse_prev:
        outs.pop(0)[...] = x
    h = _rms_mod(x, g_ref[...], _rows(sha_ref, shb_ref), _rows(sca_ref, scb_ref)).astype(bf16)
    c0 = 0
    attn_in = {}
    stored = iter(outs[:len(_STORED_GROUPS)])
    for name, width, dt in _K1_GROUPS:
        if name in _ATTN_GROUPS:
            attn_in[name] = _dot(h, w_ref[:, c0:c0 + width])
        else:
            o_ref = next(stored)
            for j in range(0, width, 512):
                wj = min(512, width - j)
                o_ref[:, j:j + wj] = _dot(h, w_ref[:, c0 + j:c0 + j + wj]).astype(dt)
        c0 += width
    qo_ref, ko_ref, vo_ref = outs[len(_STORED_GROUPS):]
    cos = jnp.concatenate([cosa_ref[...], cosb_ref[...]], axis=0)
    sin = jnp.concatenate([sina_ref[...], sinb_ref[...]], axis=0)
    _attn_prep(attn_in["b_q"], attn_in["b_kv"], cos, sin, qg_ref[...], kg_ref[...], qo_ref, ko_ref, vo_ref)


def _inproj(x, mods, g, w, cos, sin, qg, kg, tpb, ctx_tiles, nb, prev=None):
    t, d = x.shape
    row = lambda wd: pl.BlockSpec((TM, wd), lambda i: (i, 0))
    const = lambda shape: pl.BlockSpec(shape, lambda i: (0,) * len(shape))
    in_specs, args = [row(d)], [x]
    out_shape, out_specs = [], []
    if prev is not None:
        yg, wcol, mods_prev = prev
        in_specs += [pl.BlockSpec((TOP_K, TM, d // 2), lambda i: (0, i, 0)), row(TOP_K)]
        in_specs += _mod_specs(5, tpb, ctx_tiles, nb)
        args += [yg, wcol, mods_prev, mods_prev]
        out_shape.append(SDS((t, d), f32))
        out_specs.append(row(d))
    table = lambda half: pl.BlockSpec((TMH, LANES), lambda i: ((2 * i + half) % tpb, 0))
    in_specs += _mod_specs(0, tpb, ctx_tiles, nb) + _mod_specs(1, tpb, ctx_tiles, nb)
    in_specs += [const((1, d)), pl.BlockSpec((d, _K1_WIDTH), lambda i: (0, 0), pipeline_mode=pl.Buffered(1)),
                 table(0), table(1), table(0), table(1), const((1, B_HEADS * B_DH)), const((1, B_KV * B_DH))]
    args += [mods, mods, mods, mods, g, w, cos, cos, sin, sin,
             jnp.tile(qg, B_HEADS).reshape(1, -1), jnp.tile(kg, B_KV).reshape(1, -1)]
    out_shape += [SDS((t, wd), dt) for _, wd, dt in _STORED_GROUPS]
    out_specs += [row(wd) for _, wd, _ in _STORED_GROUPS]
    out_shape += [SDS((B_HEADS, t, B_DH), bf16), SDS((B_KV, t, B_DH), bf16), SDS((B_KV, t, LANES), bf16)]
    out_specs += [pl.BlockSpec((B_HEADS, TM, B_DH), lambda i: (0, i, 0)),
                  pl.BlockSpec((B_KV, TM, B_DH), lambda i: (0, i, 0)),
                  pl.BlockSpec((B_KV, TM, LANES), lambda i: (0, i, 0))]
    res = list(pl.pallas_call(
        functools.partial(_inproj_kernel, fuse_prev=prev is not None),
        out_shape=tuple(out_shape), grid=(t // TM,), in_specs=in_specs, out_specs=tuple(out_specs),
        compiler_params=_cp(("arbitrary",)), name="inproj",
    )(*args))
    x_new = res.pop(0) if prev is not None else x
    p = dict(zip([n for n, _, _ in _STORED_GROUPS], res))
    return x_new, p, tuple(res[len(_STORED_GROUPS):])


def _bwd_chunk(i, ncc, nch):
    return jnp.where(i < ncc, ncc - 1 - i, nch - 1 + ncc - i)


def _split3(x, axis=0):
    hi = x.astype(bf16)
    r1 = x - hi.astype(f32)
    mid = r1.astype(bf16)
    lo = (r1 - mid.astype(f32)).astype(bf16)
    return jnp.concatenate([hi, mid, lo], axis=axis)


def _mlstm_kernel(qk_ref, v_ref, sm_ref, cw_ref, cb_ref, gb_ref, tri_ref, lmask_ref, exps_ref, expd_ref, expv_ref,
                  oseg_ref, kmask_ref, vmask_ref, cmask_ref, nmask_ref, o_ref, qk_s, hacc, c_s, n_s, *, ctx_len):
    tb = qk_ref.shape[0]
    nch = tb // CHUNK
    ncc = ctx_len // CHUNK
    nu = 2 * A_HEADS

    cw = cw_ref[...]
    cb = cb_ref[...]
    row = _iota((CHUNK, 2 * A_HEADS * A_DQK), 0)
    kscale = jnp.where(_iota((1, 2 * A_HEADS * A_DQK), 1) >= A_HEADS * A_DQK, A_DQK ** -0.5, 1.0)

    def conv_body(c, carry):
        r0 = pl.multiple_of(c * CHUNK, CHUNK)
        x = qk_ref[pl.ds(r0, CHUNK), :].astype(f32)
        rp = pl.multiple_of(jnp.maximum(r0 - 16, 0), 16)
        rn = pl.multiple_of(jnp.minimum(r0 + CHUNK, tb - 16), 16)
        prev = qk_ref[pl.ds(rp, 16), :].astype(f32)[15:16, :]
        nxt = qk_ref[pl.ds(rn, 16), :].astype(f32)[0:1, :]
        has_prev = jnp.logical_and(r0 != 0, r0 != ctx_len)
        has_next = jnp.logical_and(r0 + CHUNK != ctx_len, r0 + CHUNK != tb)
        prev = jnp.where(has_prev, prev, 0.0)
        nxt = jnp.where(has_next, nxt, 0.0)
        xm = jnp.where(row == 0, prev, pltpu.roll(x, 1, 0))
        xp = jnp.where(row == CHUNK - 1, nxt, pltpu.roll(x, CHUNK - 1, 0))
        y = cw[0:1] * xm + cw[1:2] * x + cw[2:3] * xp + cb
        y = y * _sigmoid(y) * kscale
        qk_s[pl.ds(r0, CHUNK), :] = y.astype(bf16)
        return carry

    lax.fori_loop(0, nch, conv_body, 0)

    hacc[...] = jnp.zeros_like(hacc)
    c_s[...] = jnp.zeros_like(c_s)
    n_s[...] = jnp.zeros_like(n_s)

    lane = _iota((CHUNK, LANES), 1)
    row = _iota((CHUNK, LANES), 0)
    is_f = jnp.logical_and(lane >= nu, lane < 2 * nu)
    gb = gb_ref[...]
    neg_inf = jnp.float32(-jnp.inf)
    nk = A_HEADS * A_DQK
    ones_rows = jnp.ones((CHUNK, LANES), bf16)

    def cummax(x, d):
        for sh in [1 << e for e in range(CHUNK.bit_length() - 1)]:
            if d == 0:
                y = jnp.where(row >= sh, pltpu.roll(x, sh, 0), neg_inf)
            else:
                y = jnp.where(row < CHUNK - sh, pltpu.roll(x, CHUNK - sh, 0), neg_inf)
            x = jnp.maximum(x, y)
        return x

    def body(i, ms):
        ms = list(ms)
        pending = []
        for d in range(2):
            c = i if d == 0 else _bwd_chunk(i, ncc, nch)
            r0 = pl.multiple_of(c * CHUNK, CHUNK)
            qk = qk_s[pl.ds(r0, CHUNK), :]
            q, k = qk[:, :nk], qk[:, nk:]
            v = v_ref[pl.ds(r0, CHUNK), :]
            mine = jnp.logical_and(lane >= d * A_HEADS, lane < (d + 1) * A_HEADS)
            keep = lambda x: jnp.where(mine[:x.shape[0]], x, 0.0)
            sm = sm_ref[pl.ds(r0, CHUNK), :] + gb
            pre = jnp.where(is_f, _log_sigmoid(sm), sm)
            b = pltpu.roll(_dot(tri_ref[d], _split3(pre)), LANES - nu, 1)
            g = keep(pre - b)
            cm = cummax(g, d)
            g_t = g.T
            g_row = jnp.concatenate([g_t[d * A_HEADS + h:d * A_HEADS + h + 1, :] for h in range(A_HEADS)], axis=1)
            pm = jnp.exp(jnp.where(lmask_ref[d] > 0, g_row - _dot(_split3(cm, 1), exps_ref[d]), neg_inf))
            kst = jnp.tile(k, (A_HEADS, 1)) * kmask_ref[...]
            s = (_dot_nt(q, kst) * pm).astype(bf16)
            p1 = _dot(s, jnp.tile(v, (A_HEADS, 1)) * vmask_ref[...])
            sden = _dot(s, oseg_ref[d])
            cbd, nbd = c_s[d], n_s[d]
            qc = _dot(q, cbd.astype(bf16))
            qn = _dot(q, nbd.astype(bf16))
            m = ms[d]
            rmax, li = b + cm, b + m
            mj = jnp.maximum(li, rmax)
            alpha, beta = jnp.exp(rmax - mj), jnp.exp(li - mj)
            inv = 1.0 / jnp.maximum(jnp.abs(alpha * sden + beta * qn), jnp.exp(-mj))
            ab = _dot(_split3(jnp.concatenate([keep(alpha * inv), keep(beta * inv)], axis=0), 1), expv_ref[d])
            hout = ab[:CHUNK] * p1 + ab[CHUNK:] * qc
            last = CHUNK - 1 if d == 0 else 0
            b_last, m_new, mloc = b[last:last + 1], mj[last:last + 1], rmax[last:last + 1]
            wl = keep(jnp.exp(b_last - b + pre - mloc))
            kw = (k.astype(f32) * _dot(_split3(wl, 1), expd_ref[d])).astype(bf16)
            dec, gain = keep(jnp.exp(b_last + m - m_new)), keep(jnp.exp(mloc - m_new))
            dg = _dot(_split3(jnp.concatenate([dec, gain, jnp.zeros((6, LANES), f32)], axis=0), 1), expv_ref[d])
            c_new = cbd * dg[0:1] + (_dot_tn(kw, v) * cmask_ref[...]) * dg[1:2]
            n_new = nbd * dec + (_dot_tn(kw, ones_rows) * nmask_ref[d]) * gain
            ms[d] = keep(m_new)
            pending.append((r0, hout, c_new, n_new))
        for d, (r0, hout, c_new, n_new) in enumerate(pending):
            hacc[pl.ds(r0, CHUNK), :] += hout
            c_s[d] = c_new
            n_s[d] = n_new
        return tuple(ms)

    lax.fori_loop(0, nch, body, tuple(jnp.zeros((1, LANES), f32) for _ in range(2)))
    o_ref[...] = hacc[...].astype(bf16)


def _mlstm_constants():
    t = np.arange(CHUNK)
    lane = np.arange(LANES)
    hs = np.arange(A_HEADS * CHUNK)
    hd = np.arange(A_HEADS * A_DQK)
    hv = np.arange(A_HEADS * A_DV)
    tri3, lmask, exps, expd, expv, oseg, nmask = [], [], [], [], [], [], []
    for d in range(2):
        pos = t if d == 0 else CHUNK - 1 - t
        tri = pos[None, :] <= pos[:, None]
        tri3.append(np.tile(tri, (1, 3)))
        lmask.append(np.tile(tri, (1, A_HEADS)))
        unit = lambda idx, width: lane[:, None] == d * A_HEADS + idx[None, :] // width
        exps.append(np.tile(unit(hs, CHUNK), (3, 1)))
        expd.append(np.tile(unit(hd, A_DQK), (3, 1)))
        expv.append(np.tile(unit(hv, A_DV), (3, 1)))
        oseg.append(unit(hs, CHUNK).T)
        nmask.append(unit(hd, A_DQK).T)
    kmask = hs[:, None] // CHUNK == hd[None, :] // A_DQK
    vmask = hs[:, None] // CHUNK == hv[None, :] // A_DV
    cmask = hd[:, None] // A_DQK == hv[None, :] // A_DV
    as_b = lambda x: jnp.asarray(np.stack(x) if isinstance(x, list) else x, bf16)
    return (as_b(tri3), as_b(lmask), as_b(exps), as_b(expd), as_b(expv), as_b(oseg), as_b(kmask), as_b(vmask),
            jnp.asarray(cmask, f32), jnp.asarray(np.stack(nmask), f32))


def _mlstm(a_qk, a_v, small, conv_w, conv_b, gate_b, nb, ctx_len):
    t = a_qk.shape[0]
    tb = t // nb
    gb = jnp.zeros((1, LANES), f32).at[0, :4 * A_HEADS].set(gate_b.reshape(-1))
    consts = _mlstm_constants()
    const_spec = lambda a: pl.BlockSpec(a.shape, lambda b, n=a.ndim: (0,) * n, pipeline_mode=pl.Buffered(1))
    return pl.pallas_call(
        functools.partial(_mlstm_kernel, ctx_len=ctx_len),
        out_shape=SDS((t, MIX_W), bf16), grid=(nb,),
        in_specs=[pl.BlockSpec((tb, 512), lambda b: (b, 0)), pl.BlockSpec((tb, 512), lambda b: (b, 0)),
                  pl.BlockSpec((tb, LANES), lambda b: (b, 0)),
                  pl.BlockSpec((3, 512), lambda b: (0, 0)), pl.BlockSpec((1, 512), lambda b: (0, 0)),
                  pl.BlockSpec((1, LANES), lambda b: (0, 0))] + [const_spec(a) for a in consts],
        out_specs=pl.BlockSpec((tb, MIX_W), lambda b: (b, 0)),
        scratch_shapes=[pltpu.VMEM((tb, 512), bf16), pltpu.VMEM((tb, MIX_W), f32),
                        pltpu.VMEM((2, A_HEADS * A_DQK, A_HEADS * A_DV), f32),
                        pltpu.VMEM((2, A_HEADS * A_DQK, LANES), f32)],
        compiler_params=_cp(("arbitrary",)), name="mlstm",
    )(a_qk, a_v, small, conv_w, conv_b.reshape(1, -1), gb, *consts)


GLA_LEVELS = CHUNK.bit_length() - 1


def _gla_constants():
    t = np.arange(CHUNK)
    tri, gref, lmask = [], [], []
    for d in range(2):
        pos = t if d == 0 else CHUNK - 1 - t
        row_of = np.argsort(pos)
        tri.append(pos[None, :] <= pos[:, None])
        g, m = [], []
        for lvl in range(GLA_LEVELS):
            half = CHUNK >> (lvl + 1)
            ref_pos = (pos // (2 * half)) * (2 * half) + half
            g.append(t[None, :] == row_of[ref_pos][:, None])
            late, early = (pos % (2 * half)) >= half, (pos % (2 * half)) < half
            same = (pos[:, None] // (2 * half)) == (pos[None, :] // (2 * half))
            m.append(same & late[:, None] & early[None, :])
        m.append(t[:, None] == t[None, :])
        gref.append(np.concatenate(g, axis=0))
        lmask.append(np.stack([np.tile(x, (1, C_HEADS)) for x in m]))
    tri3 = np.stack([np.tile(x, (1, 3)) for x in tri])
    gref3 = np.stack([np.tile(x, (1, 3)) for x in gref])
    hs = np.arange(C_HEADS * CHUNK)
    kmask = (hs[:, None] // CHUNK) == (np.arange(C_HEADS * C_DK)[None, :] // C_DK)
    vmask = (hs[:, None] // CHUNK) == (np.arange(C_HEADS * C_DV)[None, :] // C_DV)
    bdm = (np.arange(2 * C_DV)[:, None] // C_DV) == (np.arange(2 * C_DK)[None, :] // C_DK)
    as_b = lambda x: jnp.asarray(x, bf16)
    return as_b(tri3), as_b(gref3), as_b(np.stack(lmask)), as_b(kmask), as_b(vmask), jnp.asarray(bdm, f32)


def _gla_kernel(qk_ref, v_ref, sm_ref, wup_ref, bup_ref, tri_ref, gref_ref, lmask_ref, kmask_ref, vmask_ref, bdm_ref,
                o_ref, hacc, s_s, *, ctx_len):
    tb = qk_ref.shape[0]
    nch = tb // CHUNK
    ncc = ctx_len // CHUNK
    nk = C_HEADS * C_DK
    pw = 2 * C_DK
    pv = 2 * C_DV
    npair = C_HEADS // 2

    hacc[...] = jnp.zeros_like(hacc)
    s_s[...] = jnp.zeros_like(s_s)
    qscale = C_DK ** -0.5

    def body(i, carry):
        for d in range(2):
            c = i if d == 0 else _bwd_chunk(i, ncc, nch)
            r0 = pl.multiple_of(c * CHUNK, CHUNK)
            sm = sm_ref[pl.ds(r0, CHUNK), :]
            la = _log_sigmoid(_dot(sm, wup_ref[d], HI) + bup_ref[d]) * (1.0 / C_TAU)
            qk = qk_ref[pl.ds(r0, CHUNK), :]
            q = qk[:, :nk].astype(f32) * qscale
            k = qk[:, nk:].astype(f32)
            v = v_ref[pl.ds(r0, CHUNK), :]
            b = _dot(tri_ref[d], _split3(la))
            bref = _dot(gref_ref[d], _split3(b))

            def scores(qe, ke):
                kst = jnp.tile(ke.astype(bf16), (C_HEADS, 1)) * kmask_ref[...]
                return _dot_nt(qe.astype(bf16), kst)

            a = scores(q, k) * lmask_ref[d, GLA_LEVELS]
            for lvl in range(GLA_LEVELS):
                rel = b - bref[CHUNK * lvl:CHUNK * (lvl + 1)]
                s = scores(q * jnp.exp(jnp.minimum(rel, 0.0)), k * jnp.exp(jnp.minimum(-rel, 0.0)))
                a = a + s * lmask_ref[d, lvl]
            vst = jnp.tile(v, (C_HEADS, 1)) * vmask_ref[...]
            o = _dot(a.astype(bf16), vst)

            last = CHUNK - 1 if d == 0 else 0
            bl = b[last:last + 1]
            qd = (q * jnp.exp(b)).astype(bf16)
            kd = (k * jnp.exp(bl - b)).astype(bf16)
            dec = jnp.exp(bl)
            o_int = []
            for p in range(npair):
                u = d * npair + p
                st = s_s[u]
                o_int.append(_dot_nt(qd[:, pw * p:pw * (p + 1)], st.astype(bf16)))
                upd = _dot_tn(v[:, pv * p:pv * (p + 1)], kd[:, pw * p:pw * (p + 1)])
                s_s[u] = st * dec[:, pw * p:pw * (p + 1)] + bdm_ref[...] * upd
            hacc[pl.ds(r0, CHUNK), :] += o + jnp.concatenate(o_int, axis=1)
        return carry

    lax.fori_loop(0, nch, body, 0)
    o_ref[...] = hacc[...].astype(bf16)


def _gla(c_qk, c_v, small, w_up, b_up, nb, ctx_len):
    t = c_qk.shape[0]
    tb = t // nb
    nk = C_HEADS * C_DK
    wz = jnp.zeros((2, LANES, nk), f32)
    wz = wz.at[0, 16:16 + C_RANK].set(w_up[0]).at[1, 16 + C_RANK:16 + 2 * C_RANK].set(w_up[1])
    consts = _gla_constants()
    const_spec = lambda a: pl.BlockSpec(a.shape, lambda b, n=a.ndim: (0,) * n, pipeline_mode=pl.Buffered(1))
    return pl.pallas_call(
        functools.partial(_gla_kernel, ctx_len=ctx_len),
        out_shape=SDS((t, MIX_W), bf16), grid=(nb,),
        in_specs=[pl.BlockSpec((tb, 512), lambda b: (b, 0)), pl.BlockSpec((tb, 512), lambda b: (b, 0)),
                  pl.BlockSpec((tb, LANES), lambda b: (b, 0)),
                  pl.BlockSpec((2, LANES, nk), lambda b: (0, 0, 0)), pl.BlockSpec((2, 1, nk), lambda b: (0, 0, 0))]
                 + [const_spec(a) for a in consts],
        out_specs=pl.BlockSpec((tb, MIX_W), lambda b: (b, 0)),
        scratch_shapes=[pltpu.VMEM((tb, MIX_W), f32), pltpu.VMEM((C_HEADS, 2 * C_DV, 2 * C_DK), f32)],
        compiler_params=_cp(("arbitrary",)), name="gla",
    )(c_qk, c_v, small, wz, b_up.reshape(2, 1, nk), *consts)


def _rope_tables(seq, ctx_len):
    n_f = B_DH // 4
    t = np.arange(seq)
    freqs = ROPE_THETA ** (-np.arange(n_f, dtype=np.float32) / n_f)
    hd = np.arange(B_DH)
    pos = np.where(hd[None, :] < B_DH // 2, (t // GRID_W)[:, None], (t % GRID_W)[:, None]).astype(np.float32)
    ang = jnp.asarray(pos * freqs[hd % n_f][None, :], f32)
    sign = np.where((hd % (2 * n_f)) < n_f, -1.0, 1.0).astype(np.float32)
    cos = jnp.concatenate([jnp.ones((ctx_len, B_DH), f32), jnp.cos(ang)], axis=0)
    sin = jnp.concatenate([jnp.zeros((ctx_len, B_DH), f32), jnp.sin(ang) * sign[None, :]], axis=0)
    return jnp.tile(cos, (1, 2)), jnp.tile(sin, (1, 2))


def _attn_prep(bq, bkv, cos, sin, qg, kg, qo_ref, ko_ref, vo_ref):
    def norm_rope(x, g):
        w = x.shape[1]
        bd = (_iota((w, w), 0) // B_DH == _iota((w, w), 1) // B_DH).astype(bf16)
        ss = _dot((x * x).astype(bf16), bd)
        xn = x * lax.rsqrt(ss * (1.0 / B_DH) + EPS) * g
        first = (_iota(x.shape, 1) % (B_DH // 2)) < (B_DH // 4)
        swapped = jnp.where(first, pltpu.roll(xn, w - B_DH // 4, 1), pltpu.roll(xn, B_DH // 4, 1))
        reps = w // LANES
        return xn * jnp.tile(cos, (1, reps)) + swapped * jnp.tile(sin, (1, reps))

    q = (norm_rope(bq, qg) * (B_DH ** -0.5)).astype(bf16)
    for h in range(B_HEADS):
        qo_ref[h] = q[:, B_DH * h:B_DH * (h + 1)]
    k = norm_rope(bkv[:, :B_KV * B_DH], kg).astype(bf16)
    v = bkv[:, B_KV * B_DH:].astype(bf16)
    ones_col = (_iota((bkv.shape[0], LANES - B_DH), 1) == 0).astype(bf16)
    for h in range(B_KV):
        ko_ref[h] = k[:, B_DH * h:B_DH * (h + 1)]
        vo_ref[h] = jnp.concatenate([v[:, B_DH * h:B_DH * (h + 1)], ones_col], axis=1)


def _attn_kernel(q_ref, k_ref, v_ref, o_ref, *, ctx_len):
    tb = k_ref.shape[1]
    q = q_ref[...].reshape(B_GROUP * TQ, B_DH)

    def attend(klen):
        starts = [0] + list(range(ctx_len, klen, KEY_CHUNK))
        m = acc = None
        for s0, s1 in zip(starts, starts[1:] + [klen]):
            s = _dot_nt(q, k_ref[0, s0:s1, :])
            smax = jnp.max(s, axis=-1, keepdims=True)
            if m is None:
                m = smax
                acc = _dot(jnp.exp((s - m).astype(bf16)), v_ref[0, s0:s1, :])
            else:
                m_new = jnp.maximum(m, smax)
                acc = jnp.exp(m - m_new) * acc + _dot(jnp.exp((s - m_new).astype(bf16)), v_ref[0, s0:s1, :])
                m = m_new
        o = acc[:, :B_DH] / acc[:, B_DH:B_DH + 1]
        o_ref[...] = o.reshape(B_GROUP, TQ, B_DH).astype(bf16)

    is_ctx = pl.program_id(2) < ctx_len // TQ

    @pl.when(is_ctx)
    def _():
        attend(ctx_len)

    @pl.when(jnp.logical_not(is_ctx))
    def _():
        attend(tb)


def _attn(q, k, v, nb, ctx_len):
    t = q.shape[1]
    tb = t // nb
    nq = tb // TQ
    return pl.pallas_call(
        functools.partial(_attn_kernel, ctx_len=ctx_len),
        out_shape=SDS((B_HEADS, t, B_DH), bf16), grid=(nb, B_KV, nq),
        in_specs=[pl.BlockSpec((B_GROUP, TQ, B_DH), lambda b, g, i: (g, b * nq + i, 0)),
                  pl.BlockSpec((1, tb, B_DH), lambda b, g, i: (g, b, 0)),
                  pl.BlockSpec((1, tb, LANES), lambda b, g, i: (g, b, 0))],
        out_specs=pl.BlockSpec((B_GROUP, TQ, B_DH), lambda b, g, i: (g, b * nq + i, 0)),
        compiler_params=_cp(("arbitrary", "arbitrary", "arbitrary")), name="attn",
    )(q, k, v)


def _head_rms(y, g, dv):
    parts = []
    for h in range(y.shape[1] // dv):
        yh = y[:, dv * h:dv * (h + 1)]
        parts.append(yh * lax.rsqrt(jnp.mean(yh * yh, axis=-1, keepdims=True) + EPS))
    return jnp.concatenate(parts, axis=1) * g


def _merge_kernel(ha_ref, ao_ref, att_ref, hc_ref, cr_ref, gate_ref, x_ref, g1a_ref, g1b_ref, sh2a_ref, sh2b_ref,
                  sc2a_ref, sc2b_ref, ag_ref, cg_ref, wb_ref, wo_ref, gn2_ref, wr_ref, br_ref,
                  xo_ref, h2_ref, ti_ref, tw_ref, rk_ref, cnt_ref, cnt_s):
    i = pl.program_id(0)

    @pl.when(i == 0)
    def _():
        cnt_s[...] = jnp.zeros_like(cnt_s)

    d = x_ref.shape[1]
    ya = _head_rms(ha_ref[...].astype(f32), ag_ref[...], A_DV) * _sigmoid(ao_ref[...].astype(f32))
    cr = cr_ref[...].astype(f32)
    yc = _head_rms(hc_ref[...].astype(f32), cg_ref[...], C_DV) * (cr * _sigmoid(cr))
    yb = jnp.concatenate([att_ref[h] for h in range(B_HEADS)], axis=1)
    merged = jnp.zeros((TM, d), f32)
    for n, y in enumerate((ya.astype(bf16), yb, yc.astype(bf16))):
        merged = merged + _sigmoid(gate_ref[:, d * n:d * (n + 1)]) * _dot(y, wb_ref[n])
    x = x_ref[...] + _rows(g1a_ref, g1b_ref) * _dot(merged.astype(bf16), wo_ref[...])
    xo_ref[...] = x
    h2 = _rms_mod(x, gn2_ref[...], _rows(sh2a_ref, sh2b_ref), _rows(sc2a_ref, sc2b_ref))
    h2_ref[...] = _pack_bf16_pairs(h2)

    logits = _dot_nt(wr_ref[...], h2, HI) + br_ref[...]
    eid = _iota((N_EXPERTS, TM), 0)
    work = logits
    onehot = jnp.zeros((N_EXPERTS, TM), f32)
    vals, sels = [], []
    for k in range(TOP_K):
        mk = jnp.max(work, axis=0, keepdims=True)
        ik = jnp.min(jnp.where(work == mk, eid, N_EXPERTS), axis=0, keepdims=True)
        sel = eid == ik
        work = jnp.where(sel, -jnp.inf, work)
        onehot = onehot + sel.astype(f32)
        ti_ref[k:k + 1, :] = ik
        vals.append(mk)
        sels.append(sel)
    ex = [jnp.exp(vk - vals[0]) for vk in vals]
    tot = ex[0] + ex[1] + ex[2] + ex[3]
    for k in range(TOP_K):
        tw_ref[k:k + 1, :] = ex[k] / tot

    ut = (_iota((TM, TM), 0) <= _iota((TM, TM), 1)).astype(bf16)
    incl = _dot(onehot.astype(bf16), ut)
    rank = cnt_s[...][:, 0:1] + incl - onehot
    for k in range(TOP_K):
        rk_ref[k:k + 1, :] = jnp.sum(jnp.where(sels[k], rank, 0.0), axis=0, keepdims=True).astype(jnp.int32)
    cnt_s[...] = cnt_s[...] + incl[:, TM - 1:TM]
    cnt_ref[...] = cnt_s[...]


def _merge(p, h_a, att, h_c, x, mods, a_norm_g, c_norm_g, w_branch, w_out, g_norm2, w_router, b_router,
           tpb, ctx_tiles, nb):
    t, d = x.shape
    row = lambda w: pl.BlockSpec((TM, w), lambda i: (i, 0))
    const = lambda shape: pl.BlockSpec(shape, lambda i: tuple(0 for _ in shape))
    return pl.pallas_call(
        _merge_kernel,
        out_shape=(SDS((t, d), f32), SDS((t, d // 2), jnp.int32), SDS((TOP_K, t), jnp.int32), SDS((TOP_K, t), f32),
                   SDS((TOP_K, t), jnp.int32), SDS((N_EXPERTS, LANES), f32)),
        grid=(t // TM,),
        in_specs=[row(512), row(512), pl.BlockSpec((B_HEADS, TM, B_DH), lambda i: (0, i, 0)), row(512), row(512),
                  row(3 * d), row(d)]
                 + _mod_specs(2, tpb, ctx_tiles, nb) + _mod_specs(3, tpb, ctx_tiles, nb) + _mod_specs(4, tpb, ctx_tiles, nb)
                 + [const((1, 512)), const((1, 512)), const((3, MIX_W, d)), const((d, d)), const((1, d)),
                  const((N_EXPERTS, d)), const((N_EXPERTS, 1))],
        out_specs=(row(d), row(d // 2), pl.BlockSpec((TOP_K, TM), lambda i: (0, i)), pl.BlockSpec((TOP_K, TM), lambda i: (0, i)),
                   pl.BlockSpec((TOP_K, TM), lambda i: (0, i)), const((N_EXPERTS, LANES))),
        scratch_shapes=[pltpu.VMEM((N_EXPERTS, LANES), f32)],
        compiler_params=_cp(("arbitrary",)), name="merge",
    )(h_a, p["a_o"], att, h_c, p["c_r"], p["gate"], x, mods, mods, mods, mods, mods, mods,
      a_norm_g.reshape(1, -1), c_norm_g.reshape(1, -1), w_branch, w_out, g_norm2.reshape(1, -1),
      w_router.T, b_router.reshape(-1, 1))


def _sc_mesh():
    return plsc.VectorSubcoreMesh(core_axis_name="c", subcore_axis_name="s")


def _sc_scatter_rows(src, idx, n_out):
    v, d = src.shape
    reps = idx.shape[0] // v
    per_w = v // (SC_NUM_CORES * SC_NUM_SUBCORES)
    assert per_w % SC_WINDOW == 0 and idx.shape[0] == reps * v

    @functools.partial(pl.kernel, out_type=SDS((n_out, d), src.dtype), mesh=_sc_mesh(),
                       scratch_types=[pltpu.VMEM((SC_WINDOW,), jnp.int32), pltpu.VMEM((SC_WINDOW, d), src.dtype),
                                      pltpu.SemaphoreType.DMA])
    def k(x_hbm, i_hbm, o_hbm, idx_v, rows_v, sem):
        wid = lax.axis_index("s") * SC_NUM_CORES + lax.axis_index("c")

        @pl.loop(0, per_w // SC_WINDOW)
        def _(j):
            base = wid * per_w + j * SC_WINDOW
            pltpu.sync_copy(x_hbm.at[pl.ds(base, SC_WINDOW)], rows_v)
            for r in range(reps):
                pltpu.sync_copy(i_hbm.at[pl.ds(r * v + base, SC_WINDOW)], idx_v)
                pltpu.async_copy(rows_v, o_hbm.at[idx_v], sem).wait()

    return k(src, idx)


def _sc_gather_rows(table, idx):
    d = table.shape[1]
    n = idx.shape[0]
    per_w = n // (SC_NUM_CORES * SC_NUM_SUBCORES)
    assert per_w % SC_WINDOW == 0

    @functools.partial(pl.kernel, out_type=SDS((n, d), table.dtype), mesh=_sc_mesh(),
                       scratch_types=[pltpu.VMEM((SC_WINDOW,), jnp.int32), pltpu.VMEM((SC_WINDOW, d), table.dtype),
                                      pltpu.SemaphoreType.DMA])
    def k(x_hbm, i_hbm, o_hbm, idx_v, rows_v, sem):
        wid = lax.axis_index("s") * SC_NUM_CORES + lax.axis_index("c")

        @pl.loop(0, per_w // SC_WINDOW)
        def _(j):
            base = wid * per_w + j * SC_WINDOW
            pltpu.sync_copy(i_hbm.at[pl.ds(base, SC_WINDOW)], idx_v)
            pltpu.async_copy(x_hbm.at[idx_v], rows_v, sem).wait()
            pltpu.sync_copy(rows_v, o_hbm.at[pl.ds(base, SC_WINDOW)])

    return k(table, idx)


GU_BLOCK = 2 * LANES


def _deinterleave_perm():
    n = np.arange(GU_BLOCK)
    src = np.where(n < LANES, 2 * n, 2 * (n - LANES) + 1)
    return jnp.asarray(np.arange(GU_BLOCK)[:, None] == src[None, :], bf16)


def _expert_kernel(te_ref, nv_ref, x_ref, w1_ref, b1_ref, w2_ref, b2_ref, perm_ref, y_ref, w1_s, w2_s):
    i = pl.program_id(0)
    valid = i < nv_ref[0]
    new_expert = jnp.logical_or(i == 0, te_ref[i] != te_ref[jnp.maximum(i - 1, 0)])
    dff2 = w1_ref.shape[2]

    @pl.when(jnp.logical_and(valid, new_expert))
    def _():
        for cb in range(dff2 // GU_BLOCK):
            cs = slice(GU_BLOCK * cb, GU_BLOCK * (cb + 1))
            w1_s[:, cs] = _dot(w1_ref[0, :, cs].astype(bf16), perm_ref[...]).astype(bf16)
        w2_s[...] = w2_ref[0].astype(bf16)

    @pl.when(valid)
    def _():
        x = jnp.concatenate(_unpack_bf16_pairs(x_ref[...]), axis=1).astype(bf16)
        gu = _dot(x, w1_s[...]) + b1_ref[0]
        nblk = dff2 // GU_BLOCK
        g = jnp.concatenate([gu[:, GU_BLOCK * cb:GU_BLOCK * cb + LANES] for cb in range(nblk)], axis=1)
        u = jnp.concatenate([gu[:, GU_BLOCK * cb + LANES:GU_BLOCK * (cb + 1)] for cb in range(nblk)], axis=1)
        gate = jnp.minimum(g, SWIGLU_LIMIT)
        up = jnp.clip(u, -SWIGLU_LIMIT, SWIGLU_LIMIT)
        a = (up + 1.0) * gate * _sigmoid(SWIGLU_ALPHA * gate)
        y_ref[...] = _pack_bf16_pairs(_dot(a.astype(bf16), w2_s[...]) + b2_ref[0])


def _experts(xs, tile_e, n_valid, layer, w1, b1, w2, b2):
    p, dw = xs.shape
    d = 2 * dw
    dff2 = w1.shape[-1]
    dff = w2.shape[2]
    row = lambda i, te, nv: (jnp.minimum(i, nv[0] - 1), 0)
    wsel = lambda i, te, nv: (layer, te[i], 0, 0)
    sq = pl.Squeezed()
    return pl.pallas_call(
        _expert_kernel, out_shape=SDS((p, dw), jnp.int32),
        grid_spec=pltpu.PrefetchScalarGridSpec(
            num_scalar_prefetch=2, grid=(p // TME,),
            in_specs=[pl.BlockSpec((TME, dw), row),
                      pl.BlockSpec((sq, 1, d, dff2), wsel), pl.BlockSpec((sq, 1, 1, dff2), wsel),
                      pl.BlockSpec((sq, 1, dff, d), wsel), pl.BlockSpec((sq, 1, 1, d), wsel),
                      pl.BlockSpec((GU_BLOCK, GU_BLOCK), lambda i, te, nv: (0, 0))],
            out_specs=pl.BlockSpec((TME, dw), row),
            scratch_shapes=[pltpu.VMEM((d, dff2), bf16), pltpu.VMEM((dff, d), bf16)]),
        compiler_params=_cp(("arbitrary",)), name="experts",
    )(tile_e, n_valid, xs, w1, b1, w2, b2, _deinterleave_perm())


def _final_kernel(x_ref, yg_ref, w_ref, g2_ref, gf_ref, o_ref):
    x = x_ref[...] + g2_ref[0] * _weighted_expert_sum(yg_ref, w_ref[...])
    o_ref[...] = x * lax.rsqrt(jnp.mean(x * x, axis=-1, keepdims=True) + EPS) * gf_ref[...]


def _final(x, yg, wcol, mods, tpb, ctx_tiles, nb, g_final):
    d = x.shape[1]
    lat = tpb - ctx_tiles
    rmap = lambda i: ((i // lat) * tpb + ctx_tiles + i % lat)
    return pl.pallas_call(
        _final_kernel, out_shape=SDS((nb * lat * TMH, d), f32), grid=(nb * lat,),
        in_specs=[pl.BlockSpec((TMH, d), lambda i: (rmap(i), 0)),
                  pl.BlockSpec((TOP_K, TMH, d // 2), lambda i: (0, rmap(i), 0)),
                  pl.BlockSpec((TMH, TOP_K), lambda i: (rmap(i), 0)),
                  pl.BlockSpec((1, 1, 1024), lambda i: (i // lat, 0, 5)),
                  pl.BlockSpec((1, d), lambda i: (0, 0))],
        out_specs=pl.BlockSpec((TMH, d), lambda i: (i, 0)),
        compiler_params=_cp(("arbitrary",)), name="final",
    )(x, yg, wcol, mods, g_final.reshape(1, -1))


def _routing_tables(top_i, rank, counts, n_tiles):
    cnt = counts[:, 0].astype(jnp.int32)
    padded = ((cnt + TME - 1) // TME) * TME
    ends = jnp.cumsum(padded)
    starts = ends - padded
    eids = jnp.arange(N_EXPERTS, dtype=jnp.int32)
    start_of = jnp.sum(jnp.where(top_i[..., None] == eids, starts, 0), axis=-1)
    pos = (start_of + rank).reshape(-1)
    n_valid = ends[-1] // TME
    tile_start = jnp.arange(n_tiles, dtype=jnp.int32) * TME
    tile_e = jnp.sum(tile_start[:, None] >= ends[None, :], axis=1).astype(jnp.int32)
    tile_e = jnp.where(jnp.arange(n_tiles) < n_valid, tile_e, tile_e[jnp.maximum(n_valid - 1, 0)])
    tile_e = jnp.minimum(tile_e, N_EXPERTS - 1)
    return pos, tile_e, n_valid.reshape(1).astype(jnp.int32)


def kernel(x, c, ctx, c_ctx, w_ada, b_ada, g_norm1, w_in, a_conv_w, a_conv_b, a_gate_b, a_norm_g, b_q_norm_g, b_k_norm_g, c_w_up, c_b_up, c_norm_g, w_branch, w_out, g_norm2, w_router, b_router, w_e1, b_e1, w_e2, b_e2, g_final):
    nb, seq, d = x.shape
    ctx_len = ctx.shape[1]
    depth = w_ada.shape[0]
    tb = ctx_len + seq
    t = nb * tb
    tpb, ctx_tiles = tb // TMH, ctx_len // TMH
    assert d == 1024 and nb < 16 and seq % TMH == 0 and ctx_len % TMH == 0 and ctx_len % TQ == 0 and t % TM == 0
    n_assign = TOP_K * t
    n_sorted = n_assign + N_EXPERTS * TME
    n_tiles = n_sorted // TME

    xs = jnp.concatenate([ctx, x], axis=1).reshape(t, d)
    cc = jnp.zeros((16, d), f32).at[:nb].set(c).at[nb].set(c_ctx)
    mods_all = _adaln(cc, w_ada, b_ada)
    cos, sin = _rope_tables(seq, ctx_len)
    w_in_r = _arrange_w_in(w_in)
    w_branch_b, w_out_b = w_branch.astype(bf16), w_out.astype(bf16)
    col = np.arange(b_e1.shape[-1])
    within = col % GU_BLOCK
    src = (col // GU_BLOCK) * GU_BLOCK + np.where(within < LANES, 2 * within, 2 * (within - LANES) + 1)
    b1 = b_e1[..., src][..., None, :]
    b2 = b_e2[..., None, :]

    out = None
    prev = None
    for l in range(depth):
        mods = mods_all[l].reshape(16, 1, 6 * d)
        xs, p, (qn, kn, vn) = _inproj(xs, mods, g_norm1[l].reshape(1, -1), w_in_r[l], cos, sin,
                                      b_q_norm_g[l], b_k_norm_g[l], tpb, ctx_tiles, nb, prev=prev)
        h_a = _mlstm(p["a_qk"], p["a_v"], p["small"], a_conv_w[l], a_conv_b[l], a_gate_b[l], nb, ctx_len)
        h_c = _gla(p["c_qk"], p["c_v"], p["small"], c_w_up[l], c_b_up[l], nb, ctx_len)
        att = _attn(qn, kn, vn, nb, ctx_len)
        xs, h2, top_i, top_w, rank, counts = _merge(
            p, h_a, att, h_c, xs, mods, a_norm_g[l], c_norm_g[l], w_branch_b[l], w_out_b[l], g_norm2[l],
            w_router[l], b_router[l], tpb, ctx_tiles, nb)
        pos, tile_e, n_valid = _routing_tables(top_i, rank, counts, n_tiles)
        x_sorted = _sc_scatter_rows(h2, pos, n_sorted)
        y_sorted = _experts(x_sorted, tile_e, n_valid, l, w_e1, b1, w_e2, b2)
        yg = _sc_gather_rows(y_sorted, pos).reshape(TOP_K, t, d // 2)
        if l == depth - 1:
            out = _final(xs, yg, top_w.T, mods, tpb, ctx_tiles, nb, g_final)
        else:
            prev = (yg, top_w.T, mods)
    return out.reshape(nb, seq, d)
```

```python
import functools

import jax
import jax.numpy as jnp
import numpy as np
from jax import lax
from jax.experimental import pallas as pl
from jax.experimental.pallas import tpu as pltpu
from jax.experimental.pallas import tpu_sc as plsc

f32 = jnp.float32
bf16 = jnp.bfloat16
HI = lax.Precision.HIGHEST
SDS = jax.ShapeDtypeStruct

EPS = 1e-6
CHUNK = 128
GRID_W = 64
ROPE_THETA = 10000.0
MIX_W = 512
A_HEADS, A_DQK, A_DV = 4, 64, 128
B_HEADS, B_KV, B_DH = 8, 2, 64
B_GROUP = B_HEADS // B_KV
C_HEADS, C_DK, C_DV, C_RANK, C_TAU = 4, 64, 128, 16, 16.0
N_EXPERTS, TOP_K = 32, 4
SWIGLU_LIMIT, SWIGLU_ALPHA = 7.0, 1.702

LANES = 128
SC_NUM_CORES = 2
SC_NUM_SUBCORES = 16
VMEM_LIMIT = 56 * 2 ** 20

TMH = 256
TM = 2 * TMH
TQ = 256
TME = 512
KEY_CHUNK = 2048
SC_WINDOW = 64


def _cp(sem, vmem=VMEM_LIMIT):
    return pltpu.CompilerParams(dimension_semantics=sem, vmem_limit_bytes=vmem)


def _dot(a, b, precision=None):
    return jnp.dot(a, b, preferred_element_type=f32, precision=precision)


def _dot_nt(a, b, precision=None):
    return lax.dot_general(a, b, (((1,), (1,)), ((), ())), preferred_element_type=f32, precision=precision)


def _dot_tn(a, b):
    return lax.dot_general(a, b, (((0,), (0,)), ((), ())), preferred_element_type=f32)


def _sigmoid(x):
    return 0.5 * jnp.tanh(0.5 * x) + 0.5


def _log_sigmoid(x):
    return jnp.minimum(x, 0.0) - jnp.log(1.0 + jnp.exp(-jnp.abs(x)))


def _iota(shape, axis):
    return lax.broadcasted_iota(jnp.int32, shape, axis)


def _pack_bf16_pairs(x):
    n = x.shape[1] // 2
    lo = pltpu.bitcast(x[:, :n].astype(bf16).astype(f32), jnp.uint32)
    hi = pltpu.bitcast(x[:, n:].astype(bf16).astype(f32), jnp.uint32)
    return pltpu.bitcast((lo >> 16) | hi, jnp.int32)


def _unpack_bf16_pairs(w):
    u = pltpu.bitcast(w, jnp.uint32)
    return pltpu.bitcast(u << 16, f32), pltpu.bitcast(u & jnp.uint32(0xFFFF0000), f32)


def _rms_mod(x, g, shift, scale):
    y = x * lax.rsqrt(jnp.mean(x * x, axis=-1, keepdims=True) + EPS) * g
    return y * (1.0 + scale) + shift


def _adaln_kernel(cc_ref, w_ref, b_ref, o_ref):
    cc = cc_ref[...]
    s = cc * _sigmoid(cc)
    o_ref[0] = _dot(s, w_ref[0], HI) + b_ref[0]


def _adaln(cc, w_ada, b_ada):
    nl, d, n = w_ada.shape
    return pl.pallas_call(
        _adaln_kernel, out_shape=SDS((nl, 16, n), f32), grid=(nl, n // 1024),
        in_specs=[pl.BlockSpec((16, d), lambda l, j: (0, 0)),
                  pl.BlockSpec((1, d, 1024), lambda l, j: (l, 0, j)),
                  pl.BlockSpec((1, 1, 1024), lambda l, j: (l, 0, j))],
        out_specs=pl.BlockSpec((1, 16, 1024), lambda l, j: (l, 0, j)),
        compiler_params=_cp(("arbitrary", "arbitrary")), name="adaln",
    )(cc, w_ada, b_ada.reshape(nl, 1, n))


def _mod_specs(col, tpb, ctx_tiles, nb):
    def spec(half):
        def imap(i):
            u = 2 * i + half
            return (jnp.where((u % tpb) >= ctx_tiles, u // tpb, nb), 0, col)
        return pl.BlockSpec((1, 1, 1024), imap)
    return [spec(0), spec(1)]


def _rows(a_ref, b_ref):
    a, b = a_ref[0], b_ref[0]
    return jnp.concatenate([jnp.broadcast_to(a, (TMH, a.shape[1])), jnp.broadcast_to(b, (TMH, b.shape[1]))], axis=0)


_K1_GROUPS = (("a_qk", 512, bf16), ("a_v", 512, bf16), ("a_o", 512, bf16), ("b_q", 512, bf16),
              ("b_kv", 256, bf16), ("c_qk", 512, bf16), ("c_v", 512, bf16), ("c_r", 512, bf16),
              ("gate", 3072, bf16), ("small", 128, f32))
_K1_WIDTH = sum(w for _, w, _ in _K1_GROUPS)


def _arrange_w_in(w_in):
    z = jnp.zeros(w_in.shape[:-1] + (LANES - 48,), w_in.dtype)
    w = jnp.concatenate([w_in[..., 0:1024], w_in[..., 1040:1552], w_in[..., 1552:2320], w_in[..., 2320:3344],
                         w_in[..., 3376:3888], w_in[..., 3888:6960], w_in[..., 1024:1040], w_in[..., 3344:3376], z],
                        axis=-1)
    assert w.shape[-1] == _K1_WIDTH
    return w.astype(bf16)


_ATTN_GROUPS = ("b_q", "b_kv")
_STORED_GROUPS = tuple(g for g in _K1_GROUPS if g[0] not in _ATTN_GROUPS)


def _weighted_expert_sum(yg_ref, w):
    acc_lo = acc_hi = None
    for k in range(TOP_K):
        lo, hi = _unpack_bf16_pairs(yg_ref[k])
        wk = w[:, k:k + 1]
        acc_lo = wk * lo if acc_lo is None else acc_lo + wk * lo
        acc_hi = wk * hi if acc_hi is None else acc_hi + wk * hi
    return jnp.concatenate([acc_lo, acc_hi], axis=1)


def _inproj_kernel(*refs, fuse_prev):
    refs = list(refs)
    x_ref = refs.pop(0)
    x = x_ref[...]
    if fuse_prev:
        yg_ref, wcol_ref, g2a_ref, g2b_ref = refs[:4]
        refs = refs[4:]
        x = x + _rows(g2a_ref, g2b_ref) * _weighted_expert_sum(yg_ref, wcol_ref[...])
    (sha_ref, shb_ref, sca_ref, scb_ref, g_ref, w_ref, cosa_ref, cosb_ref, sina_ref, sinb_ref, qg_ref, kg_ref,
     gb_ref, wup_ref, bup_ref) = refs[:15]
    outs = refs[15:]
    if fuse_prev:
        outs.pop(0)[...] = x
    h = _rms_mod(x, g_ref[...], _rows(sha_ref, shb_ref), _rows(sca_ref, scb_ref)).astype(bf16)
    c0 = 0
    attn_in = {}
    n_stored = len(_STORED_GROUPS) + 1
    stored = iter(outs[:n_stored])
    for name, width, dt in _K1_GROUPS:
        if name in _ATTN_GROUPS:
            attn_in[name] = _dot(h, w_ref[:, c0:c0 + width])
        elif name == "small":
            sm = _dot(h, w_ref[:, c0:c0 + width]) + gb_ref[...]
            lane = _iota(sm.shape, 1)
            is_f = jnp.logical_and(lane >= 2 * A_HEADS, lane < 4 * A_HEADS)
            next(stored)[...] = jnp.where(is_f, _log_sigmoid(sm), sm)
            la_ref = next(stored)
            sm_hi = sm.astype(bf16)
            sm_lo = (sm - sm_hi.astype(f32)).astype(bf16)
            sm3 = jnp.concatenate([sm_hi, sm_lo, sm_hi], axis=1)
            for dr in range(2):
                nk = C_HEADS * C_DK
                la_ref[:, nk * dr:nk * (dr + 1)] = _log_sigmoid(_dot(sm3, wup_ref[dr]) + bup_ref[dr]) * (1.0 / C_TAU)
        else:
            o_ref = next(stored)
            for j in range(0, width, 512):
                wj = min(512, width - j)
                o_ref[:, j:j + wj] = _dot(h, w_ref[:, c0 + j:c0 + j + wj]).astype(dt)
        c0 += width
    qo_ref, ko_ref, vo_ref = outs[n_stored:]
    cos = jnp.concatenate([cosa_ref[...], cosb_ref[...]], axis=0)
    sin = jnp.concatenate([sina_ref[...], sinb_ref[...]], axis=0)
    _attn_prep(attn_in["b_q"], attn_in["b_kv"], cos, sin, qg_ref[...], kg_ref[...], qo_ref, ko_ref, vo_ref)


def _inproj(x, mods, g, w, cos, sin, qg, kg, gate_b, w_up, b_up, tpb, ctx_tiles, nb, prev=None):
    t, d = x.shape
    row = lambda wd: pl.BlockSpec((TM, wd), lambda i: (i, 0))
    const = lambda shape: pl.BlockSpec(shape, lambda i: (0,) * len(shape))
    in_specs, args = [row(d)], [x]
    out_shape, out_specs = [], []
    nk = C_HEADS * C_DK
    gb = jnp.zeros((1, LANES), f32).at[0, :4 * A_HEADS].set(gate_b.reshape(-1))
    wz = jnp.zeros((2, LANES, nk), f32)
    wz = wz.at[0, 16:16 + C_RANK].set(w_up[0]).at[1, 16 + C_RANK:16 + 2 * C_RANK].set(w_up[1])
    wz_hi = wz.astype(bf16)
    wz_lo = (wz - wz_hi.astype(f32)).astype(bf16)
    wz = jnp.concatenate([wz_hi, wz_hi, wz_lo], axis=1)
    if prev is not None:
        yg, wcol, mods_prev = prev
        in_specs += [pl.BlockSpec((TOP_K, TM, d // 2), lambda i: (0, i, 0)), row(TOP_K)]
        in_specs += _mod_specs(5, tpb, ctx_tiles, nb)
        args += [yg, wcol, mods_prev, mods_prev]
        out_shape.append(SDS((t, d), f32))
        out_specs.append(row(d))
    table = lambda half: pl.BlockSpec((TMH, LANES), lambda i: ((2 * i + half) % tpb, 0))
    in_specs += _mod_specs(0, tpb, ctx_tiles, nb) + _mod_specs(1, tpb, ctx_tiles, nb)
    in_specs += [const((1, d)), pl.BlockSpec((d, _K1_WIDTH), lambda i: (0, 0), pipeline_mode=pl.Buffered(1)),
                 table(0), table(1), table(0), table(1), const((1, B_HEADS * B_DH)), const((1, B_KV * B_DH)),
                 const((1, LANES)), const((2, 3 * LANES, nk)), const((2, 1, nk))]
    args += [mods, mods, mods, mods, g, w, cos, cos, sin, sin,
             jnp.tile(qg, B_HEADS).reshape(1, -1), jnp.tile(kg, B_KV).reshape(1, -1), gb, wz, b_up.reshape(2, 1, nk)]
    out_shape += [SDS((t, wd), dt) for _, wd, dt in _STORED_GROUPS] + [SDS((t, 2 * nk), f32)]
    out_specs += [row(wd) for _, wd, _ in _STORED_GROUPS] + [row(2 * nk)]
    out_shape += [SDS((B_HEADS, t, B_DH), bf16), SDS((B_KV, t, B_DH), bf16), SDS((B_KV, t, LANES), bf16)]
    out_specs += [pl.BlockSpec((B_HEADS, TM, B_DH), lambda i: (0, i, 0)),
                  pl.BlockSpec((B_KV, TM, B_DH), lambda i: (0, i, 0)),
                  pl.BlockSpec((B_KV, TM, LANES), lambda i: (0, i, 0))]
    res = list(pl.pallas_call(
        functools.partial(_inproj_kernel, fuse_prev=prev is not None),
        out_shape=tuple(out_shape), grid=(t // TM,), in_specs=in_specs, out_specs=tuple(out_specs),
        compiler_params=_cp(("arbitrary",)), name="inproj",
    )(*args))
    x_new = res.pop(0) if prev is not None else x
    p = dict(zip([n for n, _, _ in _STORED_GROUPS] + ["c_la"], res))
    return x_new, p, tuple(res[len(_STORED_GROUPS) + 1:])


def _bwd_chunk(i, ncc, nch):
    return jnp.where(i < ncc, ncc - 1 - i, nch - 1 + ncc - i)


def _split3(x, axis=0):
    hi = x.astype(bf16)
    r1 = x - hi.astype(f32)
    mid = r1.astype(bf16)
    lo = (r1 - mid.astype(f32)).astype(bf16)
    return jnp.concatenate([hi, mid, lo], axis=axis)


def _mlstm_kernel(qk_ref, v_ref, sm_ref, cw_ref, cb_ref, tri_ref, lmask_ref, exps_ref, expd_ref, expv_ref,
                  oseg_ref, kmask_ref, vmask_ref, cmask_ref, nmask_ref, o_ref, qk_s, hacc, c_s, n_s, *, ctx_len):
    tb = qk_ref.shape[0]
    nch = tb // CHUNK
    ncc = ctx_len // CHUNK
    nu = 2 * A_HEADS

    cw = cw_ref[...]
    cb = cb_ref[...]
    row = _iota((CHUNK, 2 * A_HEADS * A_DQK), 0)
    kscale = jnp.where(_iota((1, 2 * A_HEADS * A_DQK), 1) >= A_HEADS * A_DQK, A_DQK ** -0.5, 1.0)

    def conv_body(c, carry):
        r0 = pl.multiple_of(c * CHUNK, CHUNK)
        x = qk_ref[pl.ds(r0, CHUNK), :].astype(f32)
        rp = pl.multiple_of(jnp.maximum(r0 - 16, 0), 16)
        rn = pl.multiple_of(jnp.minimum(r0 + CHUNK, tb - 16), 16)
        prev = qk_ref[pl.ds(rp, 16), :].astype(f32)[15:16, :]
        nxt = qk_ref[pl.ds(rn, 16), :].astype(f32)[0:1, :]
        has_prev = jnp.logical_and(r0 != 0, r0 != ctx_len)
        has_next = jnp.logical_and(r0 + CHUNK != ctx_len, r0 + CHUNK != tb)
        prev = jnp.where(has_prev, prev, 0.0)
        nxt = jnp.where(has_next, nxt, 0.0)
        xm = jnp.where(row == 0, prev, pltpu.roll(x, 1, 0))
        xp = jnp.where(row == CHUNK - 1, nxt, pltpu.roll(x, CHUNK - 1, 0))
        y = cw[0:1] * xm + cw[1:2] * x + cw[2:3] * xp + cb
        y = y * _sigmoid(y) * kscale
        qk_s[pl.ds(r0, CHUNK), :] = y.astype(bf16)
        return carry

    lax.fori_loop(0, nch, conv_body, 0)

    hacc[...] = jnp.zeros_like(hacc)
    c_s[...] = jnp.zeros_like(c_s)
    n_s[...] = jnp.zeros_like(n_s)

    lane = _iota((CHUNK, LANES), 1)
    row = _iota((CHUNK, LANES), 0)
    neg_inf = jnp.float32(-jnp.inf)
    nk = A_HEADS * A_DQK
    ones_rows = jnp.ones((CHUNK, LANES), bf16)

    def cummax(x, d):
        for sh in [1 << e for e in range(CHUNK.bit_length() - 1)]:
            if d == 0:
                y = jnp.where(row >= sh, pltpu.roll(x, sh, 0), neg_inf)
            else:
                y = jnp.where(row < CHUNK - sh, pltpu.roll(x, CHUNK - sh, 0), neg_inf)
            x = jnp.maximum(x, y)
        return x

    def body(i, ms):
        ms = list(ms)
        pending = []
        for d in range(2):
            c = i if d == 0 else _bwd_chunk(i, ncc, nch)
            r0 = pl.multiple_of(c * CHUNK, CHUNK)
            qk = qk_s[pl.ds(r0, CHUNK), :]
            q, k = qk[:, :nk], qk[:, nk:]
            v = v_ref[pl.ds(r0, CHUNK), :]
            mine = jnp.logical_and(lane >= d * A_HEADS, lane < (d + 1) * A_HEADS)
            keep = lambda x: jnp.where(mine[:x.shape[0]], x, 0.0)
            pre = sm_ref[pl.ds(r0, CHUNK), :]
            b = pltpu.roll(_dot(tri_ref[d], _split3(pre)), LANES - nu, 1)
            g = keep(pre - b)
            cm = cummax(g, d)
            g_t = g.T
            g_row = jnp.concatenate([g_t[d * A_HEADS + h:d * A_HEADS + h + 1, :] for h in range(A_HEADS)], axis=1)
            pm = jnp.exp(jnp.where(lmask_ref[d] > 0, g_row - _dot(_split3(cm, 1), exps_ref[d]), neg_inf))
            kst = jnp.tile(k, (A_HEADS, 1)) * kmask_ref[...]
            s = (_dot_nt(q, kst) * pm).astype(bf16)
            p1 = _dot(s, jnp.tile(v, (A_HEADS, 1)) * vmask_ref[...])
            sden = _dot(s, oseg_ref[d])
            cbd, nbd = c_s[d], n_s[d]
            qc = _dot(q, cbd.astype(bf16))
            qn = _dot(q, nbd.astype(bf16))
            m = ms[d]
            rmax, li = b + cm, b + m
            mj = jnp.maximum(li, rmax)
            alpha, beta = jnp.exp(rmax - mj), jnp.exp(li - mj)
            inv = 1.0 / jnp.maximum(jnp.abs(alpha * sden + beta * qn), jnp.exp(-mj))
            ab = _dot(_split3(jnp.concatenate([keep(alpha * inv), keep(beta * inv)], axis=0), 1), expv_ref[d])
            hout = ab[:CHUNK] * p1 + ab[CHUNK:] * qc
            last = CHUNK - 1 if d == 0 else 0
            b_last, m_new, mloc = b[last:last + 1], mj[last:last + 1], rmax[last:last + 1]
            wl = keep(jnp.exp(b_last - b + pre - mloc))
            kw = (k.astype(f32) * _dot(_split3(wl, 1), expd_ref[d])).astype(bf16)
            dec, gain = keep(jnp.exp(b_last + m - m_new)), keep(jnp.exp(mloc - m_new))
            dg = _dot(_split3(jnp.concatenate([dec, gain, jnp.zeros((6, LANES), f32)], axis=0), 1), expv_ref[d])
            c_new = cbd * dg[0:1] + (_dot_tn(kw, v) * cmask_ref[...]) * dg[1:2]
            n_new = nbd * dec + (_dot_tn(kw, ones_rows) * nmask_ref[d]) * gain
            ms[d] = keep(m_new)
            pending.append((r0, hout, c_new, n_new))
        for d, (r0, hout, c_new, n_new) in enumerate(pending):
            hacc[pl.ds(r0, CHUNK), :] += hout
            c_s[d] = c_new
            n_s[d] = n_new
        return tuple(ms)

    lax.fori_loop(0, nch, body, tuple(jnp.zeros((1, LANES), f32) for _ in range(2)))
    o_ref[...] = hacc[...].astype(bf16)


def _mlstm_constants():
    t = np.arange(CHUNK)
    lane = np.arange(LANES)
    hs = np.arange(A_HEADS * CHUNK)
    hd = np.arange(A_HEADS * A_DQK)
    hv = np.arange(A_HEADS * A_DV)
    tri3, lmask, exps, expd, expv, oseg, nmask = [], [], [], [], [], [], []
    for d in range(2):
        pos = t if d == 0 else CHUNK - 1 - t
        tri = pos[None, :] <= pos[:, None]
        tri3.append(np.tile(tri, (1, 3)))
        lmask.append(np.tile(tri, (1, A_HEADS)))
        unit = lambda idx, width: lane[:, None] == d * A_HEADS + idx[None, :] // width
        exps.append(np.tile(unit(hs, CHUNK), (3, 1)))
        expd.append(np.tile(unit(hd, A_DQK), (3, 1)))
        expv.append(np.tile(unit(hv, A_DV), (3, 1)))
        oseg.append(unit(hs, CHUNK).T)
        nmask.append(unit(hd, A_DQK).T)
    kmask = hs[:, None] // CHUNK == hd[None, :] // A_DQK
    vmask = hs[:, None] // CHUNK == hv[None, :] // A_DV
    cmask = hd[:, None] // A_DQK == hv[None, :] // A_DV
    as_b = lambda x: jnp.asarray(np.stack(x) if isinstance(x, list) else x, bf16)
    return (as_b(tri3), as_b(lmask), as_b(exps), as_b(expd), as_b(expv), as_b(oseg), as_b(kmask), as_b(vmask),
            jnp.asarray(cmask, f32), jnp.asarray(np.stack(nmask), f32))


def _mlstm(a_qk, a_v, small, conv_w, conv_b, nb, ctx_len):
    t = a_qk.shape[0]
    tb = t // nb
    consts = _mlstm_constants()
    const_spec = lambda a: pl.BlockSpec(a.shape, lambda b, n=a.ndim: (0,) * n, pipeline_mode=pl.Buffered(1))
    return pl.pallas_call(
        functools.partial(_mlstm_kernel, ctx_len=ctx_len),
        out_shape=SDS((t, MIX_W), bf16), grid=(nb,),
        in_specs=[pl.BlockSpec((tb, 512), lambda b: (b, 0)), pl.BlockSpec((tb, 512), lambda b: (b, 0)),
                  pl.BlockSpec((tb, LANES), lambda b: (b, 0)),
                  pl.BlockSpec((3, 512), lambda b: (0, 0)), pl.BlockSpec((1, 512), lambda b: (0, 0))]
                 + [const_spec(a) for a in consts],
        out_specs=pl.BlockSpec((tb, MIX_W), lambda b: (b, 0)),
        scratch_shapes=[pltpu.VMEM((tb, 512), bf16), pltpu.VMEM((tb, MIX_W), f32),
                        pltpu.VMEM((2, A_HEADS * A_DQK, A_HEADS * A_DV), f32),
                        pltpu.VMEM((2, A_HEADS * A_DQK, LANES), f32)],
        compiler_params=_cp(("arbitrary",)), name="mlstm",
    )(a_qk, a_v, small, conv_w, conv_b.reshape(1, -1), *consts)


GLA_LEVELS = CHUNK.bit_length() - 1


def _gla_constants():
    t = np.arange(CHUNK)
    tri, gref, lmask = [], [], []
    for d in range(2):
        pos = t if d == 0 else CHUNK - 1 - t
        row_of = np.argsort(pos)
        tri.append(pos[None, :] <= pos[:, None])
        g, m = [], []
        for lvl in range(GLA_LEVELS):
            half = CHUNK >> (lvl + 1)
            ref_pos = (pos // (2 * half)) * (2 * half) + half
            g.append(t[None, :] == row_of[ref_pos][:, None])
            late, early = (pos % (2 * half)) >= half, (pos % (2 * half)) < half
            same = (pos[:, None] // (2 * half)) == (pos[None, :] // (2 * half))
            m.append(same & late[:, None] & early[None, :])
        m.append(t[:, None] == t[None, :])
        gref.append(np.concatenate(g, axis=0))
        lmask.append(np.stack([np.tile(x, (1, C_HEADS)) for x in m]))
    tri3 = np.stack([np.tile(x, (1, 3)) for x in tri])
    gref3 = np.stack([np.tile(x, (1, 3)) for x in gref])
    hs = np.arange(C_HEADS * CHUNK)
    kmask = (hs[:, None] // CHUNK) == (np.arange(C_HEADS * C_DK)[None, :] // C_DK)
    vmask = (hs[:, None] // CHUNK) == (np.arange(C_HEADS * C_DV)[None, :] // C_DV)
    bdm = (np.arange(2 * C_DV)[:, None] // C_DV) == (np.arange(2 * C_DK)[None, :] // C_DK)
    as_b = lambda x: jnp.asarray(x, bf16)
    return as_b(tri3), as_b(gref3), jnp.asarray(np.stack(lmask), f32), as_b(kmask), as_b(vmask), jnp.asarray(bdm, f32)


def _gla_kernel(qk_ref, v_ref, la_ref, tri_ref, gref_ref, lmask_ref, kmask_ref, vmask_ref, bdm_ref,
                o_ref, hacc, s_s, *, ctx_len):
    tb = qk_ref.shape[0]
    nch = tb // CHUNK
    ncc = ctx_len // CHUNK
    nk = C_HEADS * C_DK
    pw = 2 * C_DK
    pv = 2 * C_DV
    npair = C_HEADS // 2

    hacc[...] = jnp.zeros_like(hacc)
    s_s[...] = jnp.zeros_like(s_s)
    qscale = C_DK ** -0.5

    def body(i, carry):
        for d in range(2):
            c = i if d == 0 else _bwd_chunk(i, ncc, nch)
            r0 = pl.multiple_of(c * CHUNK, CHUNK)
            la = la_ref[pl.ds(r0, CHUNK), nk * d:nk * (d + 1)]
            qk = qk_ref[pl.ds(r0, CHUNK), :]
            q = qk[:, :nk].astype(f32) * qscale
            k = qk[:, nk:].astype(f32)
            v = v_ref[pl.ds(r0, CHUNK), :]
            b = _dot(tri_ref[d], _split3(la))
            bref = _dot(gref_ref[d], _split3(b))

            def scores(qe, ke):
                kst = jnp.tile(ke, (C_HEADS, 1)) * kmask_ref[...]
                return _dot_nt(qe, kst)

            qb, kb = q.astype(bf16), k.astype(bf16)
            a = scores(qb, kb) * lmask_ref[d, GLA_LEVELS]
            for lvl in range(GLA_LEVELS):
                rel = b - bref[CHUNK * lvl:CHUNK * (lvl + 1)]
                s = scores(qb * jnp.exp(jnp.minimum(rel, 0.0)).astype(bf16),
                           kb * jnp.exp(jnp.minimum(-rel, 0.0)).astype(bf16))
                a = a + s * lmask_ref[d, lvl]
            vst = jnp.tile(v, (C_HEADS, 1)) * vmask_ref[...]
            o = _dot(a.astype(bf16), vst)

            last = CHUNK - 1 if d == 0 else 0
            bl = b[last:last + 1]
            qd = (q * jnp.exp(b)).astype(bf16)
            kd = (k * jnp.exp(bl - b)).astype(bf16)
            dec = jnp.exp(bl)
            o_int = []
            for p in range(npair):
                u = d * npair + p
                st = s_s[u]
                o_int.append(_dot_nt(qd[:, pw * p:pw * (p + 1)], st.astype(bf16)))
                upd = _dot_tn(v[:, pv * p:pv * (p + 1)], kd[:, pw * p:pw * (p + 1)])
                s_s[u] = st * dec[:, pw * p:pw * (p + 1)] + bdm_ref[...] * upd
            hacc[pl.ds(r0, CHUNK), :] += o + jnp.concatenate(o_int, axis=1)
        return carry

    lax.fori_loop(0, nch, body, 0)
    o_ref[...] = hacc[...].astype(bf16)


def _gla(c_qk, c_v, c_la, nb, ctx_len):
    t = c_qk.shape[0]
    tb = t // nb
    consts = _gla_constants()
    const_spec = lambda a: pl.BlockSpec(a.shape, lambda b, n=a.ndim: (0,) * n, pipeline_mode=pl.Buffered(1))
    return pl.pallas_call(
        functools.partial(_gla_kernel, ctx_len=ctx_len),
        out_shape=SDS((t, MIX_W), bf16), grid=(nb,),
        in_specs=[pl.BlockSpec((tb, 512), lambda b: (b, 0)), pl.BlockSpec((tb, 512), lambda b: (b, 0)),
                  pl.BlockSpec((tb, c_la.shape[1]), lambda b: (b, 0), pipeline_mode=pl.Buffered(1))]
                 + [const_spec(a) for a in consts],
        out_specs=pl.BlockSpec((tb, MIX_W), lambda b: (b, 0)),
        scratch_shapes=[pltpu.VMEM((tb, MIX_W), f32), pltpu.VMEM((C_HEADS, 2 * C_DV, 2 * C_DK), f32)],
        compiler_params=_cp(("arbitrary",)), name="gla",
    )(c_qk, c_v, c_la, *consts)


def _rope_tables(seq, ctx_len):
    n_f = B_DH // 4
    t = np.arange(seq)
    freqs = ROPE_THETA ** (-np.arange(n_f, dtype=np.float32) / n_f)
    hd = np.arange(B_DH)
    pos = np.where(hd[None, :] < B_DH // 2, (t // GRID_W)[:, None], (t % GRID_W)[:, None]).astype(np.float32)
    ang = jnp.asarray(pos * freqs[hd % n_f][None, :], f32)
    sign = np.where((hd % (2 * n_f)) < n_f, -1.0, 1.0).astype(np.float32)
    cos = jnp.concatenate([jnp.ones((ctx_len, B_DH), f32), jnp.cos(ang)], axis=0)
    sin = jnp.concatenate([jnp.zeros((ctx_len, B_DH), f32), jnp.sin(ang) * sign[None, :]], axis=0)
    return jnp.tile(cos, (1, 2)), jnp.tile(sin, (1, 2))


def _attn_prep(bq, bkv, cos, sin, qg, kg, qo_ref, ko_ref, vo_ref):
    def norm_rope(x, g):
        w = x.shape[1]
        bd = (_iota((w, w), 0) // B_DH == _iota((w, w), 1) // B_DH).astype(bf16)
        ss = _dot((x * x).astype(bf16), bd)
        xn = x * lax.rsqrt(ss * (1.0 / B_DH) + EPS) * g
        first = (_iota(x.shape, 1) % (B_DH // 2)) < (B_DH // 4)
        swapped = jnp.where(first, pltpu.roll(xn, w - B_DH // 4, 1), pltpu.roll(xn, B_DH // 4, 1))
        reps = w // LANES
        return xn * jnp.tile(cos, (1, reps)) + swapped * jnp.tile(sin, (1, reps))

    q = (norm_rope(bq, qg) * (B_DH ** -0.5)).astype(bf16)
    for h in range(B_HEADS):
        qo_ref[h] = q[:, B_DH * h:B_DH * (h + 1)]
    k = norm_rope(bkv[:, :B_KV * B_DH], kg).astype(bf16)
    v = bkv[:, B_KV * B_DH:].astype(bf16)
    ones_col = (_iota((bkv.shape[0], LANES - B_DH), 1) == 0).astype(bf16)
    for h in range(B_KV):
        ko_ref[h] = k[:, B_DH * h:B_DH * (h + 1)]
        vo_ref[h] = jnp.concatenate([v[:, B_DH * h:B_DH * (h + 1)], ones_col], axis=1)


def _attn_kernel(q_ref, k_ref, v_ref, o_ref, *, ctx_len):
    tb = k_ref.shape[1]
    q = q_ref[...].reshape(B_GROUP * TQ, B_DH)

    def attend(klen):
        starts = [0] + list(range(ctx_len, klen, KEY_CHUNK))
        m = acc = None
        for s0, s1 in zip(starts, starts[1:] + [klen]):
            s = _dot_nt(q, k_ref[0, s0:s1, :])
            smax = jnp.max(s, axis=-1, keepdims=True)
            if m is None:
                m = smax
                acc = _dot(jnp.exp((s - m).astype(bf16)), v_ref[0, s0:s1, :])
            else:
                m_new = jnp.maximum(m, smax)
                acc = jnp.exp(m - m_new) * acc + _dot(jnp.exp((s - m_new).astype(bf16)), v_ref[0, s0:s1, :])
                m = m_new
        o = acc[:, :B_DH] / acc[:, B_DH:B_DH + 1]
        o_ref[...] = o.reshape(B_GROUP, TQ, B_DH).astype(bf16)

    is_ctx = pl.program_id(2) < ctx_len // TQ

    @pl.when(is_ctx)
    def _():
        attend(ctx_len)

    @pl.when(jnp.logical_not(is_ctx))
    def _():
        attend(tb)


def _attn(q, k, v, nb, ctx_len):
    t = q.shape[1]
    tb = t // nb
    nq = tb // TQ
    return pl.pallas_call(
        functools.partial(_attn_kernel, ctx_len=ctx_len),
        out_shape=SDS((B_HEADS, t, B_DH), bf16), grid=(nb, B_KV, nq),
        in_specs=[pl.BlockSpec((B_GROUP, TQ, B_DH), lambda b, g, i: (g, b * nq + i, 0)),
                  pl.BlockSpec((1, tb, B_DH), lambda b, g, i: (g, b, 0)),
                  pl.BlockSpec((1, tb, LANES), lambda b, g, i: (g, b, 0))],
        out_specs=pl.BlockSpec((B_GROUP, TQ, B_DH), lambda b, g, i: (g, b * nq + i, 0)),
        compiler_params=_cp(("arbitrary", "arbitrary", "arbitrary")), name="attn",
    )(q, k, v)


def _head_rms(y, g, dv):
    parts = []
    for h in range(y.shape[1] // dv):
        yh = y[:, dv * h:dv * (h + 1)]
        parts.append(yh * lax.rsqrt(jnp.mean(yh * yh, axis=-1, keepdims=True) + EPS))
    return jnp.concatenate(parts, axis=1) * g


def _merge_kernel(ha_ref, ao_ref, att_ref, hc_ref, cr_ref, gate_ref, x_ref, g1a_ref, g1b_ref, sh2a_ref, sh2b_ref,
                  sc2a_ref, sc2b_ref, ag_ref, cg_ref, wb_ref, wo_ref, gn2_ref, wr_ref, br_ref,
                  xo_ref, h2_ref, ti_ref, tw_ref, rk_ref, cnt_ref, cnt_s):
    i = pl.program_id(0)

    @pl.when(i == 0)
    def _():
        cnt_s[...] = jnp.zeros_like(cnt_s)

    d = x_ref.shape[1]
    ya = _head_rms(ha_ref[...].astype(f32), ag_ref[...], A_DV) * _sigmoid(ao_ref[...].astype(f32))
    cr = cr_ref[...].astype(f32)
    yc = _head_rms(hc_ref[...].astype(f32), cg_ref[...], C_DV) * (cr * _sigmoid(cr))
    yb = jnp.concatenate([att_ref[h] for h in range(B_HEADS)], axis=1)
    merged = jnp.zeros((TM, d), f32)
    for n, y in enumerate((ya.astype(bf16), yb, yc.astype(bf16))):
        merged = merged + _sigmoid(gate_ref[:, d * n:d * (n + 1)]) * _dot(y, wb_ref[n])
    x = x_ref[...] + _rows(g1a_ref, g1b_ref) * _dot(merged.astype(bf16), wo_ref[...])
    xo_ref[...] = x
    h2 = _rms_mod(x, gn2_ref[...], _rows(sh2a_ref, sh2b_ref), _rows(sc2a_ref, sc2b_ref))
    h2_ref[...] = _pack_bf16_pairs(h2)

    logits = _dot_nt(wr_ref[...], h2, HI) + br_ref[...]
    eid = _iota((N_EXPERTS, TM), 0)
    work = logits
    onehot = jnp.zeros((N_EXPERTS, TM), f32)
    vals, sels = [], []
    for k in range(TOP_K):
        mk = jnp.max(work, axis=0, keepdims=True)
        ik = jnp.min(jnp.where(work == mk, eid, N_EXPERTS), axis=0, keepdims=True)
        sel = eid == ik
        work = jnp.where(sel, -jnp.inf, work)
        onehot = onehot + sel.astype(f32)
        ti_ref[k:k + 1, :] = ik
        vals.append(mk)
        sels.append(sel)
    ex = [jnp.exp(vk - vals[0]) for vk in vals]
    tot = ex[0] + ex[1] + ex[2] + ex[3]
    for k in range(TOP_K):
        tw_ref[k:k + 1, :] = ex[k] / tot

    ut = (_iota((TM, TM), 0) <= _iota((TM, TM), 1)).astype(bf16)
    incl = _dot(onehot.astype(bf16), ut)
    rank = cnt_s[...][:, 0:1] + incl - onehot
    for k in range(TOP_K):
        rk_ref[k:k + 1, :] = jnp.sum(jnp.where(sels[k], rank, 0.0), axis=0, keepdims=True).astype(jnp.int32)
    cnt_s[...] = cnt_s[...] + incl[:, TM - 1:TM]
    cnt_ref[...] = cnt_s[...]


def _merge(p, h_a, att, h_c, x, mods, a_norm_g, c_norm_g, w_branch, w_out, g_norm2, w_router, b_router,
           tpb, ctx_tiles, nb):
    t, d = x.shape
    row = lambda w: pl.BlockSpec((TM, w), lambda i: (i, 0))
    const = lambda shape: pl.BlockSpec(shape, lambda i: tuple(0 for _ in shape))
    return pl.pallas_call(
        _merge_kernel,
        out_shape=(SDS((t, d), f32), SDS((t, d // 2), jnp.int32), SDS((TOP_K, t), jnp.int32), SDS((TOP_K, t), f32),
                   SDS((TOP_K, t), jnp.int32), SDS((N_EXPERTS, LANES), f32)),
        grid=(t // TM,),
        in_specs=[row(512), row(512), pl.BlockSpec((B_HEADS, TM, B_DH), lambda i: (0, i, 0)), row(512), row(512),
                  row(3 * d), row(d)]
                 + _mod_specs(2, tpb, ctx_tiles, nb) + _mod_specs(3, tpb, ctx_tiles, nb) + _mod_specs(4, tpb, ctx_tiles, nb)
                 + [const((1, 512)), const((1, 512)), const((3, MIX_W, d)), const((d, d)), const((1, d)),
                  const((N_EXPERTS, d)), const((N_EXPERTS, 1))],
        out_specs=(row(d), row(d // 2), pl.BlockSpec((TOP_K, TM), lambda i: (0, i)), pl.BlockSpec((TOP_K, TM), lambda i: (0, i)),
                   pl.BlockSpec((TOP_K, TM), lambda i: (0, i)), const((N_EXPERTS, LANES))),
        scratch_shapes=[pltpu.VMEM((N_EXPERTS, LANES), f32)],
        compiler_params=_cp(("arbitrary",)), name="merge",
    )(h_a, p["a_o"], att, h_c, p["c_r"], p["gate"], x, mods, mods, mods, mods, mods, mods,
      a_norm_g.reshape(1, -1), c_norm_g.reshape(1, -1), w_branch, w_out, g_norm2.reshape(1, -1),
      w_router.T, b_router.reshape(-1, 1))


def _sc_mesh():
    return plsc.VectorSubcoreMesh(core_axis_name="c", subcore_axis_name="s")


def _sc_scatter_rows(src, idx, n_out):
    v, d = src.shape
    reps = idx.shape[0] // v
    per_w = v // (SC_NUM_CORES * SC_NUM_SUBCORES)
    assert per_w % SC_WINDOW == 0 and idx.shape[0] == reps * v

    @functools.partial(pl.kernel, out_type=SDS((n_out, d), src.dtype), mesh=_sc_mesh(),
                       scratch_types=[pltpu.VMEM((SC_WINDOW,), jnp.int32), pltpu.VMEM((SC_WINDOW, d), src.dtype),
                                      pltpu.SemaphoreType.DMA])
    def k(x_hbm, i_hbm, o_hbm, idx_v, rows_v, sem):
        wid = lax.axis_index("s") * SC_NUM_CORES + lax.axis_index("c")

        @pl.loop(0, per_w // SC_WINDOW)
        def _(j):
            base = wid * per_w + j * SC_WINDOW
            pltpu.sync_copy(x_hbm.at[pl.ds(base, SC_WINDOW)], rows_v)
            for r in range(reps):
                pltpu.sync_copy(i_hbm.at[pl.ds(r * v + base, SC_WINDOW)], idx_v)
                pltpu.async_copy(rows_v, o_hbm.at[idx_v], sem).wait()

    return k(src, idx)


def _sc_gather_rows(table, idx):
    d = table.shape[1]
    n = idx.shape[0]
    per_w = n // (SC_NUM_CORES * SC_NUM_SUBCORES)
    assert per_w % SC_WINDOW == 0

    @functools.partial(pl.kernel, out_type=SDS((n, d), table.dtype), mesh=_sc_mesh(),
                       scratch_types=[pltpu.VMEM((SC_WINDOW,), jnp.int32), pltpu.VMEM((SC_WINDOW, d), table.dtype),
                                      pltpu.SemaphoreType.DMA])
    def k(x_hbm, i_hbm, o_hbm, idx_v, rows_v, sem):
        wid = lax.axis_index("s") * SC_NUM_CORES + lax.axis_index("c")

        @pl.loop(0, per_w // SC_WINDOW)
        def _(j):
            base = wid * per_w + j * SC_WINDOW
            pltpu.sync_copy(i_hbm.at[pl.ds(base, SC_WINDOW)], idx_v)
            pltpu.async_copy(x_hbm.at[idx_v], rows_v, sem).wait()
            pltpu.sync_copy(rows_v, o_hbm.at[pl.ds(base, SC_WINDOW)])

    return k(table, idx)


GU_BLOCK = 2 * LANES


def _deinterleave_perm():
    n = np.arange(GU_BLOCK)
    src = np.where(n < LANES, 2 * n, 2 * (n - LANES) + 1)
    return jnp.asarray(np.arange(GU_BLOCK)[:, None] == src[None, :], bf16)


def _expert_kernel(te_ref, nv_ref, x_ref, w1_ref, b1_ref, w2_ref, b2_ref, perm_ref, y_ref, w1_s, w2_s):
    i = pl.program_id(0)
    valid = i < nv_ref[0]
    new_expert = jnp.logical_or(i == 0, te_ref[i] != te_ref[jnp.maximum(i - 1, 0)])
    dff2 = w1_ref.shape[2]

    @pl.when(jnp.logical_and(valid, new_expert))
    def _():
        for cb in range(dff2 // GU_BLOCK):
            cs = slice(GU_BLOCK * cb, GU_BLOCK * (cb + 1))
            w1_s[:, cs] = _dot(w1_ref[0, :, cs].astype(bf16), perm_ref[...]).astype(bf16)
        w2_s[...] = w2_ref[0].astype(bf16)

    @pl.when(valid)
    def _():
        x = jnp.concatenate(_unpack_bf16_pairs(x_ref[...]), axis=1).astype(bf16)
        gu = _dot(x, w1_s[...]) + b1_ref[0]
        nblk = dff2 // GU_BLOCK
        g = jnp.concatenate([gu[:, GU_BLOCK * cb:GU_BLOCK * cb + LANES] for cb in range(nblk)], axis=1)
        u = jnp.concatenate([gu[:, GU_BLOCK * cb + LANES:GU_BLOCK * (cb + 1)] for cb in range(nblk)], axis=1)
        gate = jnp.minimum(g, SWIGLU_LIMIT)
        up = jnp.clip(u, -SWIGLU_LIMIT, SWIGLU_LIMIT)
        a = (up + 1.0) * gate * _sigmoid(SWIGLU_ALPHA * gate)
        y_ref[...] = _pack_bf16_pairs(_dot(a.astype(bf16), w2_s[...]) + b2_ref[0])


def _experts(xs, tile_e, n_valid, layer, w1, b1, w2, b2):
    p, dw = xs.shape
    d = 2 * dw
    dff2 = w1.shape[-1]
    dff = w2.shape[2]
    row = lambda i, te, nv: (jnp.minimum(i, nv[0] - 1), 0)
    wsel = lambda i, te, nv: (layer, te[i], 0, 0)
    sq = pl.Squeezed()
    return pl.pallas_call(
        _expert_kernel, out_shape=SDS((p, dw), jnp.int32),
        grid_spec=pltpu.PrefetchScalarGridSpec(
            num_scalar_prefetch=2, grid=(p // TME,),
            in_specs=[pl.BlockSpec((TME, dw), row),
                      pl.BlockSpec((sq, 1, d, dff2), wsel), pl.BlockSpec((sq, 1, 1, dff2), wsel),
                      pl.BlockSpec((sq, 1, dff, d), wsel), pl.BlockSpec((sq, 1, 1, d), wsel),
                      pl.BlockSpec((GU_BLOCK, GU_BLOCK), lambda i, te, nv: (0, 0))],
            out_specs=pl.BlockSpec((TME, dw), row),
            scratch_shapes=[pltpu.VMEM((d, dff2), bf16), pltpu.VMEM((dff, d), bf16)]),
        compiler_params=_cp(("arbitrary",)), name="experts",
    )(tile_e, n_valid, xs, w1, b1, w2, b2, _deinterleave_perm())


def _final_kernel(x_ref, yg_ref, w_ref, g2_ref, gf_ref, o_ref):
    x = x_ref[...] + g2_ref[0] * _weighted_expert_sum(yg_ref, w_ref[...])
    o_ref[...] = x * lax.rsqrt(jnp.mean(x * x, axis=-1, keepdims=True) + EPS) * gf_ref[...]


def _final(x, yg, wcol, mods, tpb, ctx_tiles, nb, g_final):
    d = x.shape[1]
    lat = tpb - ctx_tiles
    rmap = lambda i: ((i // lat) * tpb + ctx_tiles + i % lat)
    return pl.pallas_call(
        _final_kernel, out_shape=SDS((nb * lat * TMH, d), f32), grid=(nb * lat,),
        in_specs=[pl.BlockSpec((TMH, d), lambda i: (rmap(i), 0)),
                  pl.BlockSpec((TOP_K, TMH, d // 2), lambda i: (0, rmap(i), 0)),
                  pl.BlockSpec((TMH, TOP_K), lambda i: (rmap(i), 0)),
                  pl.BlockSpec((1, 1, 1024), lambda i: (i // lat, 0, 5)),
                  pl.BlockSpec((1, d), lambda i: (0, 0))],
        out_specs=pl.BlockSpec((TMH, d), lambda i: (i, 0)),
        compiler_params=_cp(("arbitrary",)), name="final",
    )(x, yg, wcol, mods, g_final.reshape(1, -1))


def _routing_tables(top_i, rank, counts, n_tiles):
    cnt = counts[:, 0].astype(jnp.int32)
    padded = ((cnt + TME - 1) // TME) * TME
    ends = jnp.cumsum(padded)
    starts = ends - padded
    eids = jnp.arange(N_EXPERTS, dtype=jnp.int32)
    start_of = jnp.sum(jnp.where(top_i[..., None] == eids, starts, 0), axis=-1)
    pos = (start_of + rank).reshape(-1)
    n_valid = ends[-1] // TME
    tile_start = jnp.arange(n_tiles, dtype=jnp.int32) * TME
    tile_e = jnp.sum(tile_start[:, None] >= ends[None, :], axis=1).astype(jnp.int32)
    tile_e = jnp.where(jnp.arange(n_tiles) < n_valid, tile_e, tile_e[jnp.maximum(n_valid - 1, 0)])
    tile_e = jnp.minimum(tile_e, N_EXPERTS - 1)
    return pos, tile_e, n_valid.reshape(1).astype(jnp.int32)


def kernel(x, c, ctx, c_ctx, w_ada, b_ada, g_norm1, w_in, a_conv_w, a_conv_b, a_gate_b, a_norm_g, b_q_norm_g, b_k_norm_g, c_w_up, c_b_up, c_norm_g, w_branch, w_out, g_norm2, w_router, b_router, w_e1, b_e1, w_e2, b_e2, g_final):
    nb, seq, d = x.shape
    ctx_len = ctx.shape[1]
    depth = w_ada.shape[0]
    tb = ctx_len + seq
    t = nb * tb
    tpb, ctx_tiles = tb // TMH, ctx_len // TMH
    assert d == 1024 and nb < 16 and seq % TMH == 0 and ctx_len % TMH == 0 and ctx_len % TQ == 0 and t % TM == 0
    n_assign = TOP_K * t
    n_sorted = n_assign + N_EXPERTS * TME
    n_tiles = n_sorted // TME

    xs = jnp.concatenate([ctx, x], axis=1).reshape(t, d)
    cc = jnp.zeros((16, d), f32).at[:nb].set(c).at[nb].set(c_ctx)
    mods_all = _adaln(cc, w_ada, b_ada)
    cos, sin = _rope_tables(seq, ctx_len)
    w_in_r = _arrange_w_in(w_in)
    w_branch_b, w_out_b = w_branch.astype(bf16), w_out.astype(bf16)
    col = np.arange(b_e1.shape[-1])
    within = col % GU_BLOCK
    src = (col // GU_BLOCK) * GU_BLOCK + np.where(within < LANES, 2 * within, 2 * (within - LANES) + 1)
    b1 = b_e1[..., src][..., None, :]
    b2 = b_e2[..., None, :]

    out = None
    prev = None
    for l in range(depth):
        mods = mods_all[l].reshape(16, 1, 6 * d)
        xs, p, (qn, kn, vn) = _inproj(xs, mods, g_norm1[l].reshape(1, -1), w_in_r[l], cos, sin,
                                      b_q_norm_g[l], b_k_norm_g[l], a_gate_b[l], c_w_up[l], c_b_up[l],
                                      tpb, ctx_tiles, nb, prev=prev)
        h_a = _mlstm(p["a_qk"], p["a_v"], p["small"], a_conv_w[l], a_conv_b[l], nb, ctx_len)
        h_c = _gla(p["c_qk"], p["c_v"], p["c_la"], nb, ctx_len)
        att = _attn(qn, kn, vn, nb, ctx_len)
        xs, h2, top_i, top_w, rank, counts = _merge(
            p, h_a, att, h_c, xs, mods, a_norm_g[l], c_norm_g[l], w_branch_b[l], w_out_b[l], g_norm2[l],
            w_router[l], b_router[l], tpb, ctx_tiles, nb)
        pos, tile_e, n_valid = _routing_tables(top_i, rank, counts, n_tiles)
        x_sorted = _sc_scatter_rows(h2, pos, n_sorted)
        y_sorted = _experts(x_sorted, tile_e, n_valid, l, w_e1, b1, w_e2, b2)
        yg = _sc_gather_rows(y_sorted, pos).reshape(TOP_K, t, d // 2)
        if l == depth - 1:
            out = _final(xs, yg, top_w.T, mods, tpb, ctx_tiles, nb, g_final)
        else:
            prev = (yg, top_w.T, mods)
    return out.reshape(nb, seq, d)
```

```python
import functools

import jax
import jax.numpy as jnp
import numpy as np
from jax import lax
from jax.experimental import pallas as pl
from jax.experimental.pallas import tpu as pltpu
from jax.experimental.pallas import tpu_sc as plsc

f32 = jnp.float32
bf16 = jnp.bfloat16
HI = lax.Precision.HIGHEST
SDS = jax.ShapeDtypeStruct

EPS = 1e-6
CHUNK = 128
GRID_W = 64
ROPE_THETA = 10000.0
MIX_W = 512
A_HEADS, A_DQK, A_DV = 4, 64, 128
B_HEADS, B_KV, B_DH = 8, 2, 64
B_GROUP = B_HEADS // B_KV
C_HEADS, C_DK, C_DV, C_RANK, C_TAU = 4, 64, 128, 16, 16.0
N_EXPERTS, TOP_K = 32, 4
SWIGLU_LIMIT, SWIGLU_ALPHA = 7.0, 1.702

LANES = 128
SC_NUM_CORES = 2
SC_NUM_SUBCORES = 16
VMEM_LIMIT = 56 * 2 ** 20

TMH = 256
TM = 2 * TMH
TQ = 256
TME = 512
KEY_CHUNK = 2048
SC_WINDOW = 64


def _cp(sem, vmem=VMEM_LIMIT):
    return pltpu.CompilerParams(dimension_semantics=sem, vmem_limit_bytes=vmem)


def _dot(a, b, precision=None):
    return jnp.dot(a, b, preferred_element_type=f32, precision=precision)


def _dot_nt(a, b, precision=None):
    return lax.dot_general(a, b, (((1,), (1,)), ((), ())), preferred_element_type=f32, precision=precision)


def _dot_tn(a, b):
    return lax.dot_general(a, b, (((0,), (0,)), ((), ())), preferred_element_type=f32)


def _sigmoid(x):
    return 0.5 * jnp.tanh(0.5 * x) + 0.5


def _log_sigmoid(x):
    return jnp.minimum(x, 0.0) - jnp.log(1.0 + jnp.exp(-jnp.abs(x)))


def _iota(shape, axis):
    return lax.broadcasted_iota(jnp.int32, shape, axis)


def _pack_bf16_pairs(x):
    n = x.shape[1] // 2
    lo = pltpu.bitcast(x[:, :n].astype(bf16).astype(f32), jnp.uint32)
    hi = pltpu.bitcast(x[:, n:].astype(bf16).astype(f32), jnp.uint32)
    return pltpu.bitcast((lo >> 16) | hi, jnp.int32)


def _unpack_bf16_pairs(w):
    u = pltpu.bitcast(w, jnp.uint32)
    return pltpu.bitcast(u << 16, f32), pltpu.bitcast(u & jnp.uint32(0xFFFF0000), f32)


def _rms_mod(x, g, shift, scale):
    y = x * lax.rsqrt(jnp.mean(x * x, axis=-1, keepdims=True) + EPS) * g
    return y * (1.0 + scale) + shift


def _adaln_kernel(cc_ref, w_ref, b_ref, o_ref):
    cc = cc_ref[...]
    s = cc * _sigmoid(cc)
    o_ref[0] = _dot(s, w_ref[0], HI) + b_ref[0]


def _adaln(cc, w_ada, b_ada):
    nl, d, n = w_ada.shape
    return pl.pallas_call(
        _adaln_kernel, out_shape=SDS((nl, 16, n), f32), grid=(nl, n // 1024),
        in_specs=[pl.BlockSpec((16, d), lambda l, j: (0, 0)),
                  pl.BlockSpec((1, d, 1024), lambda l, j: (l, 0, j)),
                  pl.BlockSpec((1, 1, 1024), lambda l, j: (l, 0, j))],
        out_specs=pl.BlockSpec((1, 16, 1024), lambda l, j: (l, 0, j)),
        compiler_params=_cp(("arbitrary", "arbitrary")), name="adaln",
    )(cc, w_ada, b_ada.reshape(nl, 1, n))


def _mod_specs(col, tpb, ctx_tiles, nb):
    def spec(half):
        def imap(i):
            u = 2 * i + half
            return (jnp.where((u % tpb) >= ctx_tiles, u // tpb, nb), 0, col)
        return pl.BlockSpec((1, 1, 1024), imap)
    return [spec(0), spec(1)]


def _rows(a_ref, b_ref):
    a, b = a_ref[0], b_ref[0]
    return jnp.concatenate([jnp.broadcast_to(a, (TMH, a.shape[1])), jnp.broadcast_to(b, (TMH, b.shape[1]))], axis=0)


_K1_GROUPS = (("a_qk", 512, bf16), ("a_v", 512, bf16), ("a_o", 512, bf16), ("b_q", 512, bf16),
              ("b_kv", 256, bf16), ("c_qk", 512, bf16), ("c_v", 512, bf16), ("c_r", 512, bf16),
              ("gate", 3072, bf16), ("small", 128, f32))
_K1_WIDTH = sum(w for _, w, _ in _K1_GROUPS)


def _arrange_w_in(w_in):
    z = jnp.zeros(w_in.shape[:-1] + (LANES - 48,), w_in.dtype)
    w = jnp.concatenate([w_in[..., 0:1024], w_in[..., 1040:1552], w_in[..., 1552:2320], w_in[..., 2320:3344],
                         w_in[..., 3376:3888], w_in[..., 3888:6960], w_in[..., 1024:1040], w_in[..., 3344:3376], z],
                        axis=-1)
    assert w.shape[-1] == _K1_WIDTH
    return w.astype(bf16)


_ATTN_GROUPS = ("b_q", "b_kv")
_STORED_GROUPS = tuple(g for g in _K1_GROUPS if g[0] not in _ATTN_GROUPS)


def _weighted_expert_sum(yg_ref, w):
    acc_lo = acc_hi = None
    for k in range(TOP_K):
        lo, hi = _unpack_bf16_pairs(yg_ref[k])
        wk = w[:, k:k + 1]
        acc_lo = wk * lo if acc_lo is None else acc_lo + wk * lo
        acc_hi = wk * hi if acc_hi is None else acc_hi + wk * hi
    return jnp.concatenate([acc_lo, acc_hi], axis=1)


def _inproj_kernel(*refs, fuse_prev):
    refs = list(refs)
    x_ref = refs.pop(0)
    x = x_ref[...]
    if fuse_prev:
        yg_ref, wcol_ref, g2a_ref, g2b_ref = refs[:4]
        refs = refs[4:]
        x = x + _rows(g2a_ref, g2b_ref) * _weighted_expert_sum(yg_ref, wcol_ref[...])
    (sha_ref, shb_ref, sca_ref, scb_ref, g_ref, w_ref, cosa_ref, cosb_ref, sina_ref, sinb_ref, qg_ref, kg_ref,
     gb_ref, wup_ref, bup_ref) = refs[:15]
    outs = refs[15:]
    if fuse_prev:
        outs.pop(0)[...] = x
    h = _rms_mod(x, g_ref[...], _rows(sha_ref, shb_ref), _rows(sca_ref, scb_ref)).astype(bf16)
    c0 = 0
    attn_in = {}
    n_stored = len(_STORED_GROUPS) + 1
    stored = iter(outs[:n_stored])
    for name, width, dt in _K1_GROUPS:
        if name in _ATTN_GROUPS:
            attn_in[name] = _dot(h, w_ref[:, c0:c0 + width])
        elif name == "small":
            sm = _dot(h, w_ref[:, c0:c0 + width]) + gb_ref[...]
            lane = _iota(sm.shape, 1)
            is_f = jnp.logical_and(lane >= 2 * A_HEADS, lane < 4 * A_HEADS)
            next(stored)[...] = jnp.where(is_f, _log_sigmoid(sm), sm)
            la_ref = next(stored)
            sm_hi = sm.astype(bf16)
            sm_lo = (sm - sm_hi.astype(f32)).astype(bf16)
            sm3 = jnp.concatenate([sm_hi, sm_lo, sm_hi], axis=1)
            for dr in range(2):
                nk = C_HEADS * C_DK
                la_ref[:, nk * dr:nk * (dr + 1)] = _log_sigmoid(_dot(sm3, wup_ref[dr]) + bup_ref[dr]) * (1.0 / C_TAU)
        else:
            o_ref = next(stored)
            for j in range(0, width, 512):
                wj = min(512, width - j)
                o_ref[:, j:j + wj] = _dot(h, w_ref[:, c0 + j:c0 + j + wj]).astype(dt)
        c0 += width
    qo_ref, ko_ref, vo_ref = outs[n_stored:]
    cos = jnp.concatenate([cosa_ref[...], cosb_ref[...]], axis=0)
    sin = jnp.concatenate([sina_ref[...], sinb_ref[...]], axis=0)
    _attn_prep(attn_in["b_q"], attn_in["b_kv"], cos, sin, qg_ref[...], kg_ref[...], qo_ref, ko_ref, vo_ref)


def _inproj(x, mods, g, w, cos, sin, qg, kg, gate_b, w_up, b_up, tpb, ctx_tiles, nb, prev=None):
    t, d = x.shape
    row = lambda wd: pl.BlockSpec((TM, wd), lambda i: (i, 0))
    const = lambda shape: pl.BlockSpec(shape, lambda i: (0,) * len(shape))
    in_specs, args = [row(d)], [x]
    out_shape, out_specs = [], []
    nk = C_HEADS * C_DK
    gb = jnp.zeros((1, LANES), f32).at[0, :4 * A_HEADS].set(gate_b.reshape(-1))
    wz = jnp.zeros((2, LANES, nk), f32)
    wz = wz.at[0, 16:16 + C_RANK].set(w_up[0]).at[1, 16 + C_RANK:16 + 2 * C_RANK].set(w_up[1])
    wz_hi = wz.astype(bf16)
    wz_lo = (wz - wz_hi.astype(f32)).astype(bf16)
    wz = jnp.concatenate([wz_hi, wz_hi, wz_lo], axis=1)
    if prev is not None:
        yg, wcol, mods_prev = prev
        in_specs += [pl.BlockSpec((TOP_K, TM, d // 2), lambda i: (0, i, 0)), row(TOP_K)]
        in_specs += _mod_specs(5, tpb, ctx_tiles, nb)
        args += [yg, wcol, mods_prev, mods_prev]
        out_shape.append(SDS((t, d), f32))
        out_specs.append(row(d))
    table = lambda half: pl.BlockSpec((TMH, LANES), lambda i: ((2 * i + half) % tpb, 0))
    in_specs += _mod_specs(0, tpb, ctx_tiles, nb) + _mod_specs(1, tpb, ctx_tiles, nb)
    in_specs += [const((1, d)), pl.BlockSpec((d, _K1_WIDTH), lambda i: (0, 0), pipeline_mode=pl.Buffered(1)),
                 table(0), table(1), table(0), table(1), const((1, B_HEADS * B_DH)), const((1, B_KV * B_DH)),
                 const((1, LANES)), const((2, 3 * LANES, nk)), const((2, 1, nk))]
    args += [mods, mods, mods, mods, g, w, cos, cos, sin, sin,
             jnp.tile(qg, B_HEADS).reshape(1, -1), jnp.tile(kg, B_KV).reshape(1, -1), gb, wz, b_up.reshape(2, 1, nk)]
    out_shape += [SDS((t, wd), dt) for _, wd, dt in _STORED_GROUPS] + [SDS((t, 2 * nk), f32)]
    out_specs += [row(wd) for _, wd, _ in _STORED_GROUPS] + [row(2 * nk)]
    out_shape += [SDS((B_HEADS, t, B_DH), bf16), SDS((B_KV, t, B_DH), bf16), SDS((B_KV, t, LANES), bf16)]
    out_specs += [pl.BlockSpec((B_HEADS, TM, B_DH), lambda i: (0, i, 0)),
                  pl.BlockSpec((B_KV, TM, B_DH), lambda i: (0, i, 0)),
                  pl.BlockSpec((B_KV, TM, LANES), lambda i: (0, i, 0))]
    res = list(pl.pallas_call(
        functools.partial(_inproj_kernel, fuse_prev=prev is not None),
        out_shape=tuple(out_shape), grid=(t // TM,), in_specs=in_specs, out_specs=tuple(out_specs),
        compiler_params=_cp(("arbitrary",)), name="inproj",
    )(*args))
    x_new = res.pop(0) if prev is not None else x
    p = dict(zip([n for n, _, _ in _STORED_GROUPS] + ["c_la"], res))
    return x_new, p, tuple(res[len(_STORED_GROUPS) + 1:])


def _bwd_chunk(i, ncc, nch):
    return jnp.where(i < ncc, ncc - 1 - i, nch - 1 + ncc - i)


def _split2(x, axis=0):
    hi = x.astype(bf16)
    return jnp.concatenate([hi, (x - hi.astype(f32)).astype(bf16)], axis=axis)


def _mlstm_kernel(qk_ref, v_ref, sm_ref, cw_ref, cb_ref, tri_ref, lmask_ref, exps_ref, expd_ref, expv_ref,
                  oseg_ref, kmask_ref, vmask_ref, cmask_ref, nmask_ref, o_ref, qk_s, hacc, c_s, n_s, *, ctx_len):
    tb = qk_ref.shape[0]
    nch = tb // CHUNK
    ncc = ctx_len // CHUNK
    nu = 2 * A_HEADS

    cw = cw_ref[...]
    cb = cb_ref[...]
    row = _iota((CHUNK, 2 * A_HEADS * A_DQK), 0)
    kscale = jnp.where(_iota((1, 2 * A_HEADS * A_DQK), 1) >= A_HEADS * A_DQK, A_DQK ** -0.5, 1.0)

    def conv_body(c, carry):
        r0 = pl.multiple_of(c * CHUNK, CHUNK)
        x = qk_ref[pl.ds(r0, CHUNK), :].astype(f32)
        rp = pl.multiple_of(jnp.maximum(r0 - 16, 0), 16)
        rn = pl.multiple_of(jnp.minimum(r0 + CHUNK, tb - 16), 16)
        prev = qk_ref[pl.ds(rp, 16), :].astype(f32)[15:16, :]
        nxt = qk_ref[pl.ds(rn, 16), :].astype(f32)[0:1, :]
        has_prev = jnp.logical_and(r0 != 0, r0 != ctx_len)
        has_next = jnp.logical_and(r0 + CHUNK != ctx_len, r0 + CHUNK != tb)
        prev = jnp.where(has_prev, prev, 0.0)
        nxt = jnp.where(has_next, nxt, 0.0)
        xm = jnp.where(row == 0, prev, pltpu.roll(x, 1, 0))
        xp = jnp.where(row == CHUNK - 1, nxt, pltpu.roll(x, CHUNK - 1, 0))
        y = cw[0:1] * xm + cw[1:2] * x + cw[2:3] * xp + cb
        y = y * _sigmoid(y) * kscale
        qk_s[pl.ds(r0, CHUNK), :] = y.astype(bf16)
        return carry

    lax.fori_loop(0, nch, conv_body, 0)

    hacc[...] = jnp.zeros_like(hacc)
    c_s[...] = jnp.zeros_like(c_s)
    n_s[...] = jnp.zeros_like(n_s)

    lane = _iota((CHUNK, LANES), 1)
    row = _iota((CHUNK, LANES), 0)
    neg_inf = jnp.float32(-jnp.inf)
    nk = A_HEADS * A_DQK
    ones_rows = jnp.ones((CHUNK, LANES), bf16)

    def cummax(x, d):
        for sh in [1 << e for e in range(CHUNK.bit_length() - 1)]:
            if d == 0:
                y = jnp.where(row >= sh, pltpu.roll(x, sh, 0), neg_inf)
            else:
                y = jnp.where(row < CHUNK - sh, pltpu.roll(x, CHUNK - sh, 0), neg_inf)
            x = jnp.maximum(x, y)
        return x

    def body(i, ms):
        ms = list(ms)
        pending = []
        for d in range(2):
            c = i if d == 0 else _bwd_chunk(i, ncc, nch)
            r0 = pl.multiple_of(c * CHUNK, CHUNK)
            qk = qk_s[pl.ds(r0, CHUNK), :]
            q, k = qk[:, :nk], qk[:, nk:]
            v = v_ref[pl.ds(r0, CHUNK), :]
            mine = jnp.logical_and(lane >= d * A_HEADS, lane < (d + 1) * A_HEADS)
            keep = lambda x: jnp.where(mine[:x.shape[0]], x, 0.0)
            pre = sm_ref[pl.ds(r0, CHUNK), :]
            b = pltpu.roll(_dot(tri_ref[d], _split2(pre)), LANES - nu, 1)
            g = keep(pre - b)
            cm = cummax(g, d)
            g_t = g.T
            g_row = jnp.concatenate([g_t[d * A_HEADS + h:d * A_HEADS + h + 1, :] for h in range(A_HEADS)], axis=1)
            pm = jnp.exp(jnp.where(lmask_ref[d] > 0, g_row - _dot(_split2(cm, 1), exps_ref[d]), neg_inf))
            kst = jnp.tile(k, (A_HEADS, 1)) * kmask_ref[...]
            s = (_dot_nt(q, kst) * pm).astype(bf16)
            p1 = _dot(s, jnp.tile(v, (A_HEADS, 1)) * vmask_ref[...])
            sden = _dot(s, oseg_ref[d])
            cbd, nbd = c_s[d], n_s[d]
            qc = _dot(q, cbd.astype(bf16))
            qn = _dot(q, nbd.astype(bf16))
            m = ms[d]
            rmax, li = b + cm, b + m
            mj = jnp.maximum(li, rmax)
            alpha, beta = jnp.exp(rmax - mj), jnp.exp(li - mj)
            inv = 1.0 / jnp.maximum(jnp.abs(alpha * sden + beta * qn), jnp.exp(-mj))
            last = CHUNK - 1 if d == 0 else 0
            b_last, m_new, mloc = b[last:last + 1], mj[last:last + 1], rmax[last:last + 1]
            dec, gain = keep(jnp.exp(b_last + m - m_new)), keep(jnp.exp(mloc - m_new))
            sc = jnp.concatenate([keep(alpha * inv), keep(beta * inv), dec, gain, jnp.zeros((6, LANES), f32)], axis=0)
            scx = _dot(_split2(sc, 1), expv_ref[d])
            hout = scx[:CHUNK] * p1 + scx[CHUNK:2 * CHUNK] * qc
            wl = keep(jnp.exp(b_last - b + pre - mloc))
            kw = (k.astype(f32) * _dot(_split2(wl, 1), expd_ref[d])).astype(bf16)
            c_new = cbd * scx[2 * CHUNK:2 * CHUNK + 1] + (_dot_tn(kw, v) * cmask_ref[...]) * scx[2 * CHUNK + 1:2 * CHUNK + 2]
            n_new = nbd * dec + (_dot_tn(kw, ones_rows) * nmask_ref[d]) * gain
            ms[d] = keep(m_new)
            pending.append((r0, hout, c_new, n_new))
        for d, (r0, hout, c_new, n_new) in enumerate(pending):
            hacc[pl.ds(r0, CHUNK), :] += hout
            c_s[d] = c_new
            n_s[d] = n_new
        return tuple(ms)

    lax.fori_loop(0, nch, body, tuple(jnp.zeros((1, LANES), f32) for _ in range(2)))
    o_ref[...] = hacc[...].astype(bf16)


def _mlstm_constants():
    t = np.arange(CHUNK)
    lane = np.arange(LANES)
    hs = np.arange(A_HEADS * CHUNK)
    hd = np.arange(A_HEADS * A_DQK)
    hv = np.arange(A_HEADS * A_DV)
    tri3, lmask, exps, expd, expv, oseg, nmask = [], [], [], [], [], [], []
    for d in range(2):
        pos = t if d == 0 else CHUNK - 1 - t
        tri = pos[None, :] <= pos[:, None]
        tri3.append(np.tile(tri, (1, 2)))
        lmask.append(np.tile(tri, (1, A_HEADS)))
        unit = lambda idx, width: lane[:, None] == d * A_HEADS + idx[None, :] // width
        exps.append(np.tile(unit(hs, CHUNK), (2, 1)))
        expd.append(np.tile(unit(hd, A_DQK), (2, 1)))
        expv.append(np.tile(unit(hv, A_DV), (2, 1)))
        oseg.append(unit(hs, CHUNK).T)
        nmask.append(unit(hd, A_DQK).T)
    kmask = hs[:, None] // CHUNK == hd[None, :] // A_DQK
    vmask = hs[:, None] // CHUNK == hv[None, :] // A_DV
    cmask = hd[:, None] // A_DQK == hv[None, :] // A_DV
    as_b = lambda x: jnp.asarray(np.stack(x) if isinstance(x, list) else x, bf16)
    return (as_b(tri3), as_b(lmask), as_b(exps), as_b(expd), as_b(expv), as_b(oseg), as_b(kmask), as_b(vmask),
            jnp.asarray(cmask, f32), jnp.asarray(np.stack(nmask), f32))


def _mlstm(a_qk, a_v, small, conv_w, conv_b, nb, ctx_len):
    t = a_qk.shape[0]
    tb = t // nb
    consts = _mlstm_constants()
    const_spec = lambda a: pl.BlockSpec(a.shape, lambda b, n=a.ndim: (0,) * n, pipeline_mode=pl.Buffered(1))
    return pl.pallas_call(
        functools.partial(_mlstm_kernel, ctx_len=ctx_len),
        out_shape=SDS((t, MIX_W), bf16), grid=(nb,),
        in_specs=[pl.BlockSpec((tb, 512), lambda b: (b, 0)), pl.BlockSpec((tb, 512), lambda b: (b, 0)),
                  pl.BlockSpec((tb, LANES), lambda b: (b, 0)),
                  pl.BlockSpec((3, 512), lambda b: (0, 0)), pl.BlockSpec((1, 512), lambda b: (0, 0))]
                 + [const_spec(a) for a in consts],
        out_specs=pl.BlockSpec((tb, MIX_W), lambda b: (b, 0)),
        scratch_shapes=[pltpu.VMEM((tb, 512), bf16), pltpu.VMEM((tb, MIX_W), f32),
                        pltpu.VMEM((2, A_HEADS * A_DQK, A_HEADS * A_DV), f32),
                        pltpu.VMEM((2, A_HEADS * A_DQK, LANES), f32)],
        compiler_params=_cp(("arbitrary",)), name="mlstm",
    )(a_qk, a_v, small, conv_w, conv_b.reshape(1, -1), *consts)


GLA_LEVELS = CHUNK.bit_length() - 1


def _gla_constants():
    t = np.arange(CHUNK)
    tri, gref, lmask = [], [], []
    for d in range(2):
        pos = t if d == 0 else CHUNK - 1 - t
        row_of = np.argsort(pos)
        tri.append(pos[None, :] <= pos[:, None])
        g, m = [], []
        for lvl in range(GLA_LEVELS):
            half = CHUNK >> (lvl + 1)
            ref_pos = (pos // (2 * half)) * (2 * half) + half
            g.append(t[None, :] == row_of[ref_pos][:, None])
            late, early = (pos % (2 * half)) >= half, (pos % (2 * half)) < half
            same = (pos[:, None] // (2 * half)) == (pos[None, :] // (2 * half))
            m.append(same & late[:, None] & early[None, :])
        m.append(t[:, None] == t[None, :])
        gref.append(np.concatenate(g, axis=0))
        lmask.append(np.stack([np.tile(x, (1, C_HEADS)) for x in m]))
    tri3 = np.stack([np.tile(x, (1, 2)) for x in tri])
    gref3 = np.stack([np.tile(x, (1, 2)) for x in gref])
    hs = np.arange(C_HEADS * CHUNK)
    kmask = (hs[:, None] // CHUNK) == (np.arange(C_HEADS * C_DK)[None, :] // C_DK)
    vmask = (hs[:, None] // CHUNK) == (np.arange(C_HEADS * C_DV)[None, :] // C_DV)
    bdm = (np.arange(2 * C_DV)[:, None] // C_DV) == (np.arange(2 * C_DK)[None, :] // C_DK)
    as_b = lambda x: jnp.asarray(x, bf16)
    return as_b(tri3), as_b(gref3), jnp.asarray(np.stack(lmask), f32), as_b(kmask), as_b(vmask), jnp.asarray(bdm, f32)


def _gla_kernel(qk_ref, v_ref, la_ref, tri_ref, gref_ref, lmask_ref, kmask_ref, vmask_ref, bdm_ref,
                o_ref, hacc, s_s, *, ctx_len):
    tb = qk_ref.shape[0]
    nch = tb // CHUNK
    ncc = ctx_len // CHUNK
    nk = C_HEADS * C_DK
    pw = 2 * C_DK
    pv = 2 * C_DV
    npair = C_HEADS // 2

    hacc[...] = jnp.zeros_like(hacc)
    s_s[...] = jnp.zeros_like(s_s)
    qscale = C_DK ** -0.5

    def body(i, carry):
        for d in range(2):
            c = i if d == 0 else _bwd_chunk(i, ncc, nch)
            r0 = pl.multiple_of(c * CHUNK, CHUNK)
            la = la_ref[pl.ds(r0, CHUNK), nk * d:nk * (d + 1)]
            qk = qk_ref[pl.ds(r0, CHUNK), :]
            q = qk[:, :nk].astype(f32) * qscale
            k = qk[:, nk:].astype(f32)
            v = v_ref[pl.ds(r0, CHUNK), :]
            b = _dot(tri_ref[d], _split2(la))
            bref = _dot(gref_ref[d], _split2(b))

            def scores(qe, ke):
                kst = jnp.tile(ke, (C_HEADS, 1)) * kmask_ref[...]
                return _dot_nt(qe, kst)

            qb, kb = q.astype(bf16), k.astype(bf16)
            a = scores(qb, kb) * lmask_ref[d, GLA_LEVELS]
            for lvl in range(GLA_LEVELS):
                rel = b - bref[CHUNK * lvl:CHUNK * (lvl + 1)]
                s = scores(qb * jnp.exp(jnp.minimum(rel, 0.0)).astype(bf16),
                           kb * jnp.exp(jnp.minimum(-rel, 0.0)).astype(bf16))
                a = a + s * lmask_ref[d, lvl]
            vst = jnp.tile(v, (C_HEADS, 1)) * vmask_ref[...]
            o = _dot(a.astype(bf16), vst)

            last = CHUNK - 1 if d == 0 else 0
            bl = b[last:last + 1]
            qd = (q * jnp.exp(b)).astype(bf16)
            kd = (k * jnp.exp(bl - b)).astype(bf16)
            dec = jnp.exp(bl)
            o_int = []
            for p in range(npair):
                u = d * npair + p
                st = s_s[u]
                o_int.append(_dot_nt(qd[:, pw * p:pw * (p + 1)], st.astype(bf16)))
                upd = _dot_tn(v[:, pv * p:pv * (p + 1)], kd[:, pw * p:pw * (p + 1)])
                s_s[u] = st * dec[:, pw * p:pw * (p + 1)] + bdm_ref[...] * upd
            hacc[pl.ds(r0, CHUNK), :] += o + jnp.concatenate(o_int, axis=1)
        return carry

    lax.fori_loop(0, nch, body, 0)
    o_ref[...] = hacc[...].astype(bf16)


def _gla(c_qk, c_v, c_la, nb, ctx_len):
    t = c_qk.shape[0]
    tb = t // nb
    consts = _gla_constants()
    const_spec = lambda a: pl.BlockSpec(a.shape, lambda b, n=a.ndim: (0,) * n, pipeline_mode=pl.Buffered(1))
    return pl.pallas_call(
        functools.partial(_gla_kernel, ctx_len=ctx_len),
        out_shape=SDS((t, MIX_W), bf16), grid=(nb,),
        in_specs=[pl.BlockSpec((tb, 512), lambda b: (b, 0)), pl.BlockSpec((tb, 512), lambda b: (b, 0)),
                  pl.BlockSpec((tb, c_la.shape[1]), lambda b: (b, 0), pipeline_mode=pl.Buffered(1))]
                 + [const_spec(a) for a in consts],
        out_specs=pl.BlockSpec((tb, MIX_W), lambda b: (b, 0)),
        scratch_shapes=[pltpu.VMEM((tb, MIX_W), f32), pltpu.VMEM((C_HEADS, 2 * C_DV, 2 * C_DK), f32)],
        compiler_params=_cp(("arbitrary",)), name="gla",
    )(c_qk, c_v, c_la, *consts)


def _rope_tables(seq, ctx_len):
    n_f = B_DH // 4
    t = np.arange(seq)
    freqs = ROPE_THETA ** (-np.arange(n_f, dtype=np.float32) / n_f)
    hd = np.arange(B_DH)
    pos = np.where(hd[None, :] < B_DH // 2, (t // GRID_W)[:, None], (t % GRID_W)[:, None]).astype(np.float32)
    ang = jnp.asarray(pos * freqs[hd % n_f][None, :], f32)
    sign = np.where((hd % (2 * n_f)) < n_f, -1.0, 1.0).astype(np.float32)
    cos = jnp.concatenate([jnp.ones((ctx_len, B_DH), f32), jnp.cos(ang)], axis=0)
    sin = jnp.concatenate([jnp.zeros((ctx_len, B_DH), f32), jnp.sin(ang) * sign[None, :]], axis=0)
    return jnp.tile(cos, (1, 2)), jnp.tile(sin, (1, 2))


def _attn_prep(bq, bkv, cos, sin, qg, kg, qo_ref, ko_ref, vo_ref):
    def norm_rope(x, g):
        w = x.shape[1]
        bd = (_iota((w, w), 0) // B_DH == _iota((w, w), 1) // B_DH).astype(bf16)
        ss = _dot((x * x).astype(bf16), bd)
        xn = x * lax.rsqrt(ss * (1.0 / B_DH) + EPS) * g
        first = (_iota(x.shape, 1) % (B_DH // 2)) < (B_DH // 4)
        swapped = jnp.where(first, pltpu.roll(xn, w - B_DH // 4, 1), pltpu.roll(xn, B_DH // 4, 1))
        reps = w // LANES
        return xn * jnp.tile(cos, (1, reps)) + swapped * jnp.tile(sin, (1, reps))

    q = (norm_rope(bq, qg) * (B_DH ** -0.5)).astype(bf16)
    for h in range(B_HEADS):
        qo_ref[h] = q[:, B_DH * h:B_DH * (h + 1)]
    k = norm_rope(bkv[:, :B_KV * B_DH], kg).astype(bf16)
    v = bkv[:, B_KV * B_DH:].astype(bf16)
    ones_col = (_iota((bkv.shape[0], LANES - B_DH), 1) == 0).astype(bf16)
    for h in range(B_KV):
        ko_ref[h] = k[:, B_DH * h:B_DH * (h + 1)]
        vo_ref[h] = jnp.concatenate([v[:, B_DH * h:B_DH * (h + 1)], ones_col], axis=1)


def _attn_kernel(q_ref, k_ref, v_ref, o_ref, *, ctx_len):
    tb = k_ref.shape[1]
    q = q_ref[...].reshape(B_GROUP * TQ, B_DH)

    def attend(klen):
        starts = [0] + list(range(ctx_len, klen, KEY_CHUNK))
        m = acc = None
        for s0, s1 in zip(starts, starts[1:] + [klen]):
            s = _dot_nt(q, k_ref[0, s0:s1, :])
            smax = jnp.max(s, axis=-1, keepdims=True)
            if m is None:
                m = smax
                acc = _dot(jnp.exp((s - m).astype(bf16)), v_ref[0, s0:s1, :])
            else:
                m_new = jnp.maximum(m, smax)
                acc = jnp.exp(m - m_new) * acc + _dot(jnp.exp((s - m_new).astype(bf16)), v_ref[0, s0:s1, :])
                m = m_new
        o = acc[:, :B_DH] / acc[:, B_DH:B_DH + 1]
        o_ref[...] = o.reshape(B_GROUP, TQ, B_DH).astype(bf16)

    is_ctx = pl.program_id(2) < ctx_len // TQ

    @pl.when(is_ctx)
    def _():
        attend(ctx_len)

    @pl.when(jnp.logical_not(is_ctx))
    def _():
        attend(tb)


def _attn(q, k, v, nb, ctx_len):
    t = q.shape[1]
    tb = t // nb
    nq = tb // TQ
    return pl.pallas_call(
        functools.partial(_attn_kernel, ctx_len=ctx_len),
        out_shape=SDS((B_HEADS, t, B_DH), bf16), grid=(nb, B_KV, nq),
        in_specs=[pl.BlockSpec((B_GROUP, TQ, B_DH), lambda b, g, i: (g, b * nq + i, 0)),
                  pl.BlockSpec((1, tb, B_DH), lambda b, g, i: (g, b, 0)),
                  pl.BlockSpec((1, tb, LANES), lambda b, g, i: (g, b, 0))],
        out_specs=pl.BlockSpec((B_GROUP, TQ, B_DH), lambda b, g, i: (g, b * nq + i, 0)),
        compiler_params=_cp(("arbitrary", "arbitrary", "arbitrary")), name="attn",
    )(q, k, v)


def _head_rms(y, g, dv):
    parts = []
    for h in range(y.shape[1] // dv):
        yh = y[:, dv * h:dv * (h + 1)]
        parts.append(yh * lax.rsqrt(jnp.mean(yh * yh, axis=-1, keepdims=True) + EPS))
    return jnp.concatenate(parts, axis=1) * g


def _merge_kernel(ha_ref, ao_ref, att_ref, hc_ref, cr_ref, gate_ref, x_ref, g1a_ref, g1b_ref, sh2a_ref, sh2b_ref,
                  sc2a_ref, sc2b_ref, ag_ref, cg_ref, wb_ref, wo_ref, gn2_ref, wr_ref, br_ref,
                  xo_ref, h2_ref, ti_ref, tw_ref, rk_ref, cnt_ref, cnt_s):
    i = pl.program_id(0)

    @pl.when(i == 0)
    def _():
        cnt_s[...] = jnp.zeros_like(cnt_s)

    d = x_ref.shape[1]
    ya = _head_rms(ha_ref[...].astype(f32), ag_ref[...], A_DV) * _sigmoid(ao_ref[...].astype(f32))
    cr = cr_ref[...].astype(f32)
    yc = _head_rms(hc_ref[...].astype(f32), cg_ref[...], C_DV) * (cr * _sigmoid(cr))
    yb = jnp.concatenate([att_ref[h] for h in range(B_HEADS)], axis=1)
    merged = jnp.zeros((TM, d), f32)
    for n, y in enumerate((ya.astype(bf16), yb, yc.astype(bf16))):
        merged = merged + _sigmoid(gate_ref[:, d * n:d * (n + 1)]) * _dot(y, wb_ref[n])
    x = x_ref[...] + _rows(g1a_ref, g1b_ref) * _dot(merged.astype(bf16), wo_ref[...])
    xo_ref[...] = x
    h2 = _rms_mod(x, gn2_ref[...], _rows(sh2a_ref, sh2b_ref), _rows(sc2a_ref, sc2b_ref))
    h2_ref[...] = _pack_bf16_pairs(h2)

    logits = _dot_nt(wr_ref[...], h2, HI) + br_ref[...]
    eid = _iota((N_EXPERTS, TM), 0)
    work = logits
    onehot = jnp.zeros((N_EXPERTS, TM), f32)
    vals, sels = [], []
    for k in range(TOP_K):
        mk = jnp.max(work, axis=0, keepdims=True)
        ik = jnp.min(jnp.where(work == mk, eid, N_EXPERTS), axis=0, keepdims=True)
        sel = eid == ik
        work = jnp.where(sel, -jnp.inf, work)
        onehot = onehot + sel.astype(f32)
        ti_ref[k:k + 1, :] = ik
        vals.append(mk)
        sels.append(sel)
    ex = [jnp.exp(vk - vals[0]) for vk in vals]
    tot = ex[0] + ex[1] + ex[2] + ex[3]
    for k in range(TOP_K):
        tw_ref[k:k + 1, :] = ex[k] / tot

    ut = (_iota((TM, TM), 0) <= _iota((TM, TM), 1)).astype(bf16)
    incl = _dot(onehot.astype(bf16), ut)
    rank = cnt_s[...][:, 0:1] + incl - onehot
    for k in range(TOP_K):
        rk_ref[k:k + 1, :] = jnp.sum(jnp.where(sels[k], rank, 0.0), axis=0, keepdims=True).astype(jnp.int32)
    cnt_s[...] = cnt_s[...] + incl[:, TM - 1:TM]
    cnt_ref[...] = cnt_s[...]


def _merge(p, h_a, att, h_c, x, mods, a_norm_g, c_norm_g, w_branch, w_out, g_norm2, w_router, b_router,
           tpb, ctx_tiles, nb):
    t, d = x.shape
    row = lambda w: pl.BlockSpec((TM, w), lambda i: (i, 0))
    const = lambda shape: pl.BlockSpec(shape, lambda i: tuple(0 for _ in shape))
    return pl.pallas_call(
        _merge_kernel,
        out_shape=(SDS((t, d), f32), SDS((t, d // 2), jnp.int32), SDS((TOP_K, t), jnp.int32), SDS((TOP_K, t), f32),
                   SDS((TOP_K, t), jnp.int32), SDS((N_EXPERTS, LANES), f32)),
        grid=(t // TM,),
        in_specs=[row(512), row(512), pl.BlockSpec((B_HEADS, TM, B_DH), lambda i: (0, i, 0)), row(512), row(512),
                  row(3 * d), row(d)]
                 + _mod_specs(2, tpb, ctx_tiles, nb) + _mod_specs(3, tpb, ctx_tiles, nb) + _mod_specs(4, tpb, ctx_tiles, nb)
                 + [const((1, 512)), const((1, 512)), const((3, MIX_W, d)), const((d, d)), const((1, d)),
                  const((N_EXPERTS, d)), const((N_EXPERTS, 1))],
        out_specs=(row(d), row(d // 2), pl.BlockSpec((TOP_K, TM), lambda i: (0, i)), pl.BlockSpec((TOP_K, TM), lambda i: (0, i)),
                   pl.BlockSpec((TOP_K, TM), lambda i: (0, i)), const((N_EXPERTS, LANES))),
        scratch_shapes=[pltpu.VMEM((N_EXPERTS, LANES), f32)],
        compiler_params=_cp(("arbitrary",)), name="merge",
    )(h_a, p["a_o"], att, h_c, p["c_r"], p["gate"], x, mods, mods, mods, mods, mods, mods,
      a_norm_g.reshape(1, -1), c_norm_g.reshape(1, -1), w_branch, w_out, g_norm2.reshape(1, -1),
      w_router.T, b_router.reshape(-1, 1))


def _sc_mesh():
    return plsc.VectorSubcoreMesh(core_axis_name="c", subcore_axis_name="s")


def _sc_scatter_rows(src, idx, n_out):
    v, d = src.shape
    reps = idx.shape[0] // v
    per_w = v // (SC_NUM_CORES * SC_NUM_SUBCORES)
    assert per_w % SC_WINDOW == 0 and idx.shape[0] == reps * v

    @functools.partial(pl.kernel, out_type=SDS((n_out, d), src.dtype), mesh=_sc_mesh(),
                       scratch_types=[pltpu.VMEM((SC_WINDOW,), jnp.int32), pltpu.VMEM((SC_WINDOW, d), src.dtype),
                                      pltpu.SemaphoreType.DMA])
    def k(x_hbm, i_hbm, o_hbm, idx_v, rows_v, sem):
        wid = lax.axis_index("s") * SC_NUM_CORES + lax.axis_index("c")

        @pl.loop(0, per_w // SC_WINDOW)
        def _(j):
            base = wid * per_w + j * SC_WINDOW
            pltpu.sync_copy(x_hbm.at[pl.ds(base, SC_WINDOW)], rows_v)
            for r in range(reps):
                pltpu.sync_copy(i_hbm.at[pl.ds(r * v + base, SC_WINDOW)], idx_v)
                pltpu.async_copy(rows_v, o_hbm.at[idx_v], sem).wait()

    return k(src, idx)


def _sc_gather_rows(table, idx):
    d = table.shape[1]
    n = idx.shape[0]
    per_w = n // (SC_NUM_CORES * SC_NUM_SUBCORES)
    assert per_w % SC_WINDOW == 0

    @functools.partial(pl.kernel, out_type=SDS((n, d), table.dtype), mesh=_sc_mesh(),
                       scratch_types=[pltpu.VMEM((SC_WINDOW,), jnp.int32), pltpu.VMEM((SC_WINDOW, d), table.dtype),
                                      pltpu.SemaphoreType.DMA])
    def k(x_hbm, i_hbm, o_hbm, idx_v, rows_v, sem):
        wid = lax.axis_index("s") * SC_NUM_CORES + lax.axis_index("c")

        @pl.loop(0, per_w // SC_WINDOW)
        def _(j):
            base = wid * per_w + j * SC_WINDOW
            pltpu.sync_copy(i_hbm.at[pl.ds(base, SC_WINDOW)], idx_v)
            pltpu.async_copy(x_hbm.at[idx_v], rows_v, sem).wait()
            pltpu.sync_copy(rows_v, o_hbm.at[pl.ds(base, SC_WINDOW)])

    return k(table, idx)


GU_BLOCK = 2 * LANES


def _deinterleave_perm():
    n = np.arange(GU_BLOCK)
    src = np.where(n < LANES, 2 * n, 2 * (n - LANES) + 1)
    return jnp.asarray(np.arange(GU_BLOCK)[:, None] == src[None, :], bf16)


def _expert_kernel(te_ref, nv_ref, x_ref, w1_ref, b1_ref, w2_ref, b2_ref, perm_ref, y_ref, w1_s, w2_s):
    i = pl.program_id(0)
    valid = i < nv_ref[0]
    new_expert = jnp.logical_or(i == 0, te_ref[i] != te_ref[jnp.maximum(i - 1, 0)])
    dff2 = w1_ref.shape[2]

    @pl.when(jnp.logical_and(valid, new_expert))
    def _():
        for cb in range(dff2 // GU_BLOCK):
            cs = slice(GU_BLOCK * cb, GU_BLOCK * (cb + 1))
            w1_s[:, cs] = _dot(w1_ref[0, :, cs].astype(bf16), perm_ref[...]).astype(bf16)
        w2_s[...] = w2_ref[0].astype(bf16)

    @pl.when(valid)
    def _():
        x = jnp.concatenate(_unpack_bf16_pairs(x_ref[...]), axis=1).astype(bf16)
        gu = _dot(x, w1_s[...]) + b1_ref[0]
        nblk = dff2 // GU_BLOCK
        g = jnp.concatenate([gu[:, GU_BLOCK * cb:GU_BLOCK * cb + LANES] for cb in range(nblk)], axis=1)
        u = jnp.concatenate([gu[:, GU_BLOCK * cb + LANES:GU_BLOCK * (cb + 1)] for cb in range(nblk)], axis=1)
        gate = jnp.minimum(g, SWIGLU_LIMIT)
        up = jnp.clip(u, -SWIGLU_LIMIT, SWIGLU_LIMIT)
        a = (up + 1.0) * gate * _sigmoid(SWIGLU_ALPHA * gate)
        y_ref[...] = _pack_bf16_pairs(_dot(a.astype(bf16), w2_s[...]) + b2_ref[0])


def _experts(xs, tile_e, n_valid, layer, w1, b1, w2, b2):
    p, dw = xs.shape
    d = 2 * dw
    dff2 = w1.shape[-1]
    dff = w2.shape[2]
    row = lambda i, te, nv: (jnp.minimum(i, nv[0] - 1), 0)
    wsel = lambda i, te, nv: (layer, te[i], 0, 0)
    sq = pl.Squeezed()
    return pl.pallas_call(
        _expert_kernel, out_shape=SDS((p, dw), jnp.int32),
        grid_spec=pltpu.PrefetchScalarGridSpec(
            num_scalar_prefetch=2, grid=(p // TME,),
            in_specs=[pl.BlockSpec((TME, dw), row),
                      pl.BlockSpec((sq, 1, d, dff2), wsel), pl.BlockSpec((sq, 1, 1, dff2), wsel),
                      pl.BlockSpec((sq, 1, dff, d), wsel), pl.BlockSpec((sq, 1, 1, d), wsel),
                      pl.BlockSpec((GU_BLOCK, GU_BLOCK), lambda i, te, nv: (0, 0))],
            out_specs=pl.BlockSpec((TME, dw), row),
            scratch_shapes=[pltpu.VMEM((d, dff2), bf16), pltpu.VMEM((dff, d), bf16)]),
        compiler_params=_cp(("arbitrary",)), name="experts",
    )(tile_e, n_valid, xs, w1, b1, w2, b2, _deinterleave_perm())


def _final_kernel(x_ref, yg_ref, w_ref, g2_ref, gf_ref, o_ref):
    x = x_ref[...] + g2_ref[0] * _weighted_expert_sum(yg_ref, w_ref[...])
    o_ref[...] = x * lax.rsqrt(jnp.mean(x * x, axis=-1, keepdims=True) + EPS) * gf_ref[...]


def _final(x, yg, wcol, mods, tpb, ctx_tiles, nb, g_final):
    d = x.shape[1]
    lat = tpb - ctx_tiles
    rmap = lambda i: ((i // lat) * tpb + ctx_tiles + i % lat)
    return pl.pallas_call(
        _final_kernel, out_shape=SDS((nb * lat * TMH, d), f32), grid=(nb * lat,),
        in_specs=[pl.BlockSpec((TMH, d), lambda i: (rmap(i), 0)),
                  pl.BlockSpec((TOP_K, TMH, d // 2), lambda i: (0, rmap(i), 0)),
                  pl.BlockSpec((TMH, TOP_K), lambda i: (rmap(i), 0)),
                  pl.BlockSpec((1, 1, 1024), lambda i: (i // lat, 0, 5)),
                  pl.BlockSpec((1, d), lambda i: (0, 0))],
        out_specs=pl.BlockSpec((TMH, d), lambda i: (i, 0)),
        compiler_params=_cp(("arbitrary",)), name="final",
    )(x, yg, wcol, mods, g_final.reshape(1, -1))


def _routing_tables(top_i, rank, counts, n_tiles):
    cnt = counts[:, 0].astype(jnp.int32)
    padded = ((cnt + TME - 1) // TME) * TME
    ends = jnp.cumsum(padded)
    starts = ends - padded
    eids = jnp.arange(N_EXPERTS, dtype=jnp.int32)
    start_of = jnp.sum(jnp.where(top_i[..., None] == eids, starts, 0), axis=-1)
    pos = (start_of + rank).reshape(-1)
    n_valid = ends[-1] // TME
    tile_start = jnp.arange(n_tiles, dtype=jnp.int32) * TME
    tile_e = jnp.sum(tile_start[:, None] >= ends[None, :], axis=1).astype(jnp.int32)
    tile_e = jnp.where(jnp.arange(n_tiles) < n_valid, tile_e, tile_e[jnp.maximum(n_valid - 1, 0)])
    tile_e = jnp.minimum(tile_e, N_EXPERTS - 1)
    return pos, tile_e, n_valid.reshape(1).astype(jnp.int32)


def kernel(x, c, ctx, c_ctx, w_ada, b_ada, g_norm1, w_in, a_conv_w, a_conv_b, a_gate_b, a_norm_g, b_q_norm_g, b_k_norm_g, c_w_up, c_b_up, c_norm_g, w_branch, w_out, g_norm2, w_router, b_router, w_e1, b_e1, w_e2, b_e2, g_final):
    nb, seq, d = x.shape
    ctx_len = ctx.shape[1]
    depth = w_ada.shape[0]
    tb = ctx_len + seq
    t = nb * tb
    tpb, ctx_tiles = tb // TMH, ctx_len // TMH
    assert d == 1024 and nb < 16 and seq % TMH == 0 and ctx_len % TMH == 0 and ctx_len % TQ == 0 and t % TM == 0
    n_assign = TOP_K * t
    n_sorted = n_assign + N_EXPERTS * TME
    n_tiles = n_sorted // TME

    xs = jnp.concatenate([ctx, x], axis=1).reshape(t, d)
    cc = jnp.zeros((16, d), f32).at[:nb].set(c).at[nb].set(c_ctx)
    mods_all = _adaln(cc, w_ada, b_ada)
    cos, sin = _rope_tables(seq, ctx_len)
    w_in_r = _arrange_w_in(w_in)
    w_branch_b, w_out_b = w_branch.astype(bf16), w_out.astype(bf16)
    col = np.arange(b_e1.shape[-1])
    within = col % GU_BLOCK
    src = (col // GU_BLOCK) * GU_BLOCK + np.where(within < LANES, 2 * within, 2 * (within - LANES) + 1)
    b1 = b_e1[..., src][..., None, :]
    b2 = b_e2[..., None, :]

    out = None
    prev = None
    for l in range(depth):
        mods = mods_all[l].reshape(16, 1, 6 * d)
        xs, p, (qn, kn, vn) = _inproj(xs, mods, g_norm1[l].reshape(1, -1), w_in_r[l], cos, sin,
                                      b_q_norm_g[l], b_k_norm_g[l], a_gate_b[l], c_w_up[l], c_b_up[l],
                                      tpb, ctx_tiles, nb, prev=prev)
        h_a = _mlstm(p["a_qk"], p["a_v"], p["small"], a_conv_w[l], a_conv_b[l], nb, ctx_len)
        h_c = _gla(p["c_qk"], p["c_v"], p["c_la"], nb, ctx_len)
        att = _attn(qn, kn, vn, nb, ctx_len)
        xs, h2, top_i, top_w, rank, counts = _merge(
            p, h_a, att, h_c, xs, mods, a_norm_g[l], c_norm_g[l], w_branch_b[l], w_out_b[l], g_norm2[l],
            w_router[l], b_router[l], tpb, ctx_tiles, nb)
        pos, tile_e, n_valid = _routing_tables(top_i, rank, counts, n_tiles)
        x_sorted = _sc_scatter_rows(h2, pos, n_sorted)
        y_sorted = _experts(x_sorted, tile_e, n_valid, l, w_e1, b1, w_e2, b2)
        yg = _sc_gather_rows(y_sorted, pos).reshape(TOP_K, t, d // 2)
        if l == depth - 1:
            out = _final(xs, yg, top_w.T, mods, tpb, ctx_tiles, nb, g_final)
        else:
            prev = (yg, top_w.T, mods)
    return out.reshape(nb, seq, d)
```

```python
import functools

import jax
import jax.numpy as jnp
import numpy as np
from jax import lax
from jax.experimental import pallas as pl
from jax.experimental.pallas import tpu as pltpu
from jax.experimental.pallas import tpu_sc as plsc

f32 = jnp.float32
bf16 = jnp.bfloat16
HI = lax.Precision.HIGHEST
SDS = jax.ShapeDtypeStruct

EPS = 1e-6
CHUNK = 128
GRID_W = 64
ROPE_THETA = 10000.0
MIX_W = 512
A_HEADS, A_DQK, A_DV = 4, 64, 128
B_HEADS, B_KV, B_DH = 8, 2, 64
B_GROUP = B_HEADS // B_KV
C_HEADS, C_DK, C_DV, C_RANK, C_TAU = 4, 64, 128, 16, 16.0
N_EXPERTS, TOP_K = 32, 4
SWIGLU_LIMIT, SWIGLU_ALPHA = 7.0, 1.702

LANES = 128
SC_NUM_CORES = 2
SC_NUM_SUBCORES = 16
VMEM_LIMIT = 56 * 2 ** 20

TMH = 256
TM = 2 * TMH
TQ = 256
TME = 512
KEY_CHUNK = 2048
SC_WINDOW = 64


def _cp(sem, vmem=VMEM_LIMIT):
    return pltpu.CompilerParams(dimension_semantics=sem, vmem_limit_bytes=vmem)


def _dot(a, b, precision=None):
    return jnp.dot(a, b, preferred_element_type=f32, precision=precision)


def _dot_nt(a, b, precision=None):
    return lax.dot_general(a, b, (((1,), (1,)), ((), ())), preferred_element_type=f32, precision=precision)


def _dot_tn(a, b):
    return lax.dot_general(a, b, (((0,), (0,)), ((), ())), preferred_element_type=f32)


def _sigmoid(x):
    return 0.5 * jnp.tanh(0.5 * x) + 0.5


def _log_sigmoid(x):
    return jnp.minimum(x, 0.0) - jnp.log(1.0 + jnp.exp(-jnp.abs(x)))


def _iota(shape, axis):
    return lax.broadcasted_iota(jnp.int32, shape, axis)


def _pack_bf16_pairs(x):
    n = x.shape[1] // 2
    lo = pltpu.bitcast(x[:, :n].astype(bf16).astype(f32), jnp.uint32)
    hi = pltpu.bitcast(x[:, n:].astype(bf16).astype(f32), jnp.uint32)
    return pltpu.bitcast((lo >> 16) | hi, jnp.int32)


def _unpack_bf16_pairs(w):
    u = pltpu.bitcast(w, jnp.uint32)
    return pltpu.bitcast(u << 16, f32), pltpu.bitcast(u & jnp.uint32(0xFFFF0000), f32)


def _rms_mod(x, g, shift, scale):
    y = x * lax.rsqrt(jnp.mean(x * x, axis=-1, keepdims=True) + EPS) * g
    return y * (1.0 + scale) + shift


def _adaln_kernel(cc_ref, w_ref, b_ref, o_ref):
    cc = cc_ref[...]
    s = cc * _sigmoid(cc)
    o_ref[0] = _dot(s, w_ref[0], HI) + b_ref[0]


def _adaln(cc, w_ada, b_ada):
    nl, d, n = w_ada.shape
    return pl.pallas_call(
        _adaln_kernel, out_shape=SDS((nl, 16, n), f32), grid=(nl, n // 1024),
        in_specs=[pl.BlockSpec((16, d), lambda l, j: (0, 0)),
                  pl.BlockSpec((1, d, 1024), lambda l, j: (l, 0, j)),
                  pl.BlockSpec((1, 1, 1024), lambda l, j: (l, 0, j))],
        out_specs=pl.BlockSpec((1, 16, 1024), lambda l, j: (l, 0, j)),
        compiler_params=_cp(("arbitrary", "arbitrary")), name="adaln",
    )(cc, w_ada, b_ada.reshape(nl, 1, n))


def _mod_specs(col, tpb, ctx_tiles, nb):
    def spec(half):
        def imap(i):
            u = 2 * i + half
            return (jnp.where((u % tpb) >= ctx_tiles, u // tpb, nb), 0, col)
        return pl.BlockSpec((1, 1, 1024), imap)
    return [spec(0), spec(1)]


def _rows(a_ref, b_ref):
    a, b = a_ref[0], b_ref[0]
    return jnp.concatenate([jnp.broadcast_to(a, (TMH, a.shape[1])), jnp.broadcast_to(b, (TMH, b.shape[1]))], axis=0)


_K1_GROUPS = (("a_qk", 512, bf16), ("a_v", 512, bf16), ("a_o", 512, bf16), ("b_q", 512, bf16),
              ("b_kv", 256, bf16), ("c_qk", 512, bf16), ("c_v", 512, bf16), ("c_r", 512, bf16),
              ("gate", 3072, bf16), ("small", 128, f32))
_K1_WIDTH = sum(w for _, w, _ in _K1_GROUPS)


def _arrange_w_in(w_in):
    z = jnp.zeros(w_in.shape[:-1] + (LANES - 48,), w_in.dtype)
    w = jnp.concatenate([w_in[..., 0:1024], w_in[..., 1040:1552], w_in[..., 1552:2320], w_in[..., 2320:3344],
                         w_in[..., 3376:3888], w_in[..., 3888:6960], w_in[..., 1024:1040], w_in[..., 3344:3376], z],
                        axis=-1)
    assert w.shape[-1] == _K1_WIDTH
    return w.astype(bf16)


_ATTN_GROUPS = ("b_q", "b_kv")
_STORED_GROUPS = tuple(g for g in _K1_GROUPS if g[0] not in _ATTN_GROUPS)


def _weighted_expert_sum(yg_ref, w):
    acc_lo = acc_hi = None
    for k in range(TOP_K):
        lo, hi = _unpack_bf16_pairs(yg_ref[k])
        wk = w[:, k:k + 1]
        acc_lo = wk * lo if acc_lo is None else acc_lo + wk * lo
        acc_hi = wk * hi if acc_hi is None else acc_hi + wk * hi
    return jnp.concatenate([acc_lo, acc_hi], axis=1)


def _inproj_kernel(*refs, fuse_prev):
    refs = list(refs)
    x_ref = refs.pop(0)
    x = x_ref[...]
    if fuse_prev:
        yg_ref, wcol_ref, g2a_ref, g2b_ref = refs[:4]
        refs = refs[4:]
        x = x + _rows(g2a_ref, g2b_ref) * _weighted_expert_sum(yg_ref, wcol_ref[...])
    (sha_ref, shb_ref, sca_ref, scb_ref, g_ref, w_ref, cosa_ref, cosb_ref, sina_ref, sinb_ref, qg_ref, kg_ref,
     gb_ref, wup_ref, bup_ref) = refs[:15]
    outs = refs[15:]
    if fuse_prev:
        outs.pop(0)[...] = x
    h = _rms_mod(x, g_ref[...], _rows(sha_ref, shb_ref), _rows(sca_ref, scb_ref)).astype(bf16)
    c0 = 0
    attn_in = {}
    n_stored = len(_STORED_GROUPS) + 1
    stored = iter(outs[:n_stored])
    for name, width, dt in _K1_GROUPS:
        if name in _ATTN_GROUPS:
            attn_in[name] = _dot(h, w_ref[:, c0:c0 + width])
        elif name == "small":
            sm = _dot(h, w_ref[:, c0:c0 + width]) + gb_ref[...]
            lane = _iota(sm.shape, 1)
            is_f = jnp.logical_and(lane >= 2 * A_HEADS, lane < 4 * A_HEADS)
            next(stored)[...] = jnp.where(is_f, _log_sigmoid(sm), sm)
            la_ref = next(stored)
            sm_hi = sm.astype(bf16)
            sm_lo = (sm - sm_hi.astype(f32)).astype(bf16)
            sm3 = jnp.concatenate([sm_hi, sm_lo, sm_hi], axis=1)
            for dr in range(2):
                nk = C_HEADS * C_DK
                la_ref[:, nk * dr:nk * (dr + 1)] = _log_sigmoid(_dot(sm3, wup_ref[dr]) + bup_ref[dr]) * (1.0 / C_TAU)
        else:
            o_ref = next(stored)
            for j in range(0, width, 512):
                wj = min(512, width - j)
                o_ref[:, j:j + wj] = _dot(h, w_ref[:, c0 + j:c0 + j + wj]).astype(dt)
        c0 += width
    qo_ref, ko_ref, vo_ref = outs[n_stored:]
    cos = jnp.concatenate([cosa_ref[...], cosb_ref[...]], axis=0)
    sin = jnp.concatenate([sina_ref[...], sinb_ref[...]], axis=0)
    _attn_prep(attn_in["b_q"], attn_in["b_kv"], cos, sin, qg_ref[...], kg_ref[...], qo_ref, ko_ref, vo_ref)


def _inproj(x, mods, g, w, cos, sin, qg, kg, gate_b, w_up, b_up, tpb, ctx_tiles, nb, prev=None):
    t, d = x.shape
    row = lambda wd: pl.BlockSpec((TM, wd), lambda i: (i, 0))
    const = lambda shape: pl.BlockSpec(shape, lambda i: (0,) * len(shape))
    in_specs, args = [row(d)], [x]
    out_shape, out_specs = [], []
    nk = C_HEADS * C_DK
    gb = jnp.zeros((1, LANES), f32).at[0, :4 * A_HEADS].set(gate_b.reshape(-1))
    wz = jnp.zeros((2, LANES, nk), f32)
    wz = wz.at[0, 16:16 + C_RANK].set(w_up[0]).at[1, 16 + C_RANK:16 + 2 * C_RANK].set(w_up[1])
    wz_hi = wz.astype(bf16)
    wz_lo = (wz - wz_hi.astype(f32)).astype(bf16)
    wz = jnp.concatenate([wz_hi, wz_hi, wz_lo], axis=1)
    if prev is not None:
        yg, wcol, mods_prev = prev
        in_specs += [pl.BlockSpec((TOP_K, TM, d // 2), lambda i: (0, i, 0)), row(TOP_K)]
        in_specs += _mod_specs(5, tpb, ctx_tiles, nb)
        args += [yg, wcol, mods_prev, mods_prev]
        out_shape.append(SDS((t, d), f32))
        out_specs.append(row(d))
    table = lambda half: pl.BlockSpec((TMH, LANES), lambda i: ((2 * i + half) % tpb, 0))
    in_specs += _mod_specs(0, tpb, ctx_tiles, nb) + _mod_specs(1, tpb, ctx_tiles, nb)
    in_specs += [const((1, d)), pl.BlockSpec((d, _K1_WIDTH), lambda i: (0, 0), pipeline_mode=pl.Buffered(1)),
                 table(0), table(1), table(0), table(1), const((1, B_HEADS * B_DH)), const((1, B_KV * B_DH)),
                 const((1, LANES)), const((2, 3 * LANES, nk)), const((2, 1, nk))]
    args += [mods, mods, mods, mods, g, w, cos, cos, sin, sin,
             jnp.tile(qg, B_HEADS).reshape(1, -1), jnp.tile(kg, B_KV).reshape(1, -1), gb, wz, b_up.reshape(2, 1, nk)]
    out_shape += [SDS((t, wd), dt) for _, wd, dt in _STORED_GROUPS] + [SDS((t, 2 * nk), f32)]
    out_specs += [row(wd) for _, wd, _ in _STORED_GROUPS] + [row(2 * nk)]
    out_shape += [SDS((B_HEADS, t, B_DH), bf16), SDS((B_KV, t, B_DH), bf16), SDS((B_KV, t, LANES), bf16)]
    out_specs += [pl.BlockSpec((B_HEADS, TM, B_DH), lambda i: (0, i, 0)),
                  pl.BlockSpec((B_KV, TM, B_DH), lambda i: (0, i, 0)),
                  pl.BlockSpec((B_KV, TM, LANES), lambda i: (0, i, 0))]
    res = list(pl.pallas_call(
        functools.partial(_inproj_kernel, fuse_prev=prev is not None),
        out_shape=tuple(out_shape), grid=(t // TM,), in_specs=in_specs, out_specs=tuple(out_specs),
        compiler_params=_cp(("arbitrary",)), name="inproj",
    )(*args))
    x_new = res.pop(0) if prev is not None else x
    p = dict(zip([n for n, _, _ in _STORED_GROUPS] + ["c_la"], res))
    return x_new, p, tuple(res[len(_STORED_GROUPS) + 1:])


def _bwd_chunk(i, ncc, nch):
    return jnp.where(i < ncc, ncc - 1 - i, nch - 1 + ncc - i)


def _split2(x, axis=0):
    hi = x.astype(bf16)
    return jnp.concatenate([hi, (x - hi.astype(f32)).astype(bf16)], axis=axis)


def _mlstm_kernel(qk_ref, v_ref, sm_ref, cw_ref, cb_ref, tri_ref, lmask_ref, exps_ref, expd_ref, expv_ref,
                  oseg_ref, kmask_ref, vmask_ref, cmask_ref, nmask_ref, o_ref, qk_s, hacc, c_s, n_s, *, ctx_len):
    tb = qk_ref.shape[0]
    nch = tb // CHUNK
    ncc = ctx_len // CHUNK
    nu = 2 * A_HEADS

    cw = cw_ref[...]
    cb = cb_ref[...]
    row = _iota((CHUNK, 2 * A_HEADS * A_DQK), 0)
    kscale = jnp.where(_iota((1, 2 * A_HEADS * A_DQK), 1) >= A_HEADS * A_DQK, A_DQK ** -0.5, 1.0)

    def conv_body(c, carry):
        r0 = pl.multiple_of(c * CHUNK, CHUNK)
        x = qk_ref[pl.ds(r0, CHUNK), :].astype(f32)
        rp = pl.multiple_of(jnp.maximum(r0 - 16, 0), 16)
        rn = pl.multiple_of(jnp.minimum(r0 + CHUNK, tb - 16), 16)
        prev = qk_ref[pl.ds(rp, 16), :].astype(f32)[15:16, :]
        nxt = qk_ref[pl.ds(rn, 16), :].astype(f32)[0:1, :]
        has_prev = jnp.logical_and(r0 != 0, r0 != ctx_len)
        has_next = jnp.logical_and(r0 + CHUNK != ctx_len, r0 + CHUNK != tb)
        prev = jnp.where(has_prev, prev, 0.0)
        nxt = jnp.where(has_next, nxt, 0.0)
        xm = jnp.where(row == 0, prev, pltpu.roll(x, 1, 0))
        xp = jnp.where(row == CHUNK - 1, nxt, pltpu.roll(x, CHUNK - 1, 0))
        y = cw[0:1] * xm + cw[1:2] * x + cw[2:3] * xp + cb
        y = y * _sigmoid(y) * kscale
        qk_s[pl.ds(r0, CHUNK), :] = y.astype(bf16)
        return carry

    lax.fori_loop(0, nch, conv_body, 0)

    hacc[...] = jnp.zeros_like(hacc)
    c_s[...] = jnp.zeros_like(c_s)
    n_s[...] = jnp.zeros_like(n_s)

    lane = _iota((CHUNK, LANES), 1)
    row = _iota((CHUNK, LANES), 0)
    neg_inf = jnp.float32(-jnp.inf)
    nk = A_HEADS * A_DQK
    ones_rows = jnp.ones((CHUNK, LANES), bf16)

    def cummax(x, d):
        for sh in [1 << e for e in range(CHUNK.bit_length() - 1)]:
            if d == 0:
                y = jnp.where(row >= sh, pltpu.roll(x, sh, 0), neg_inf)
            else:
                y = jnp.where(row < CHUNK - sh, pltpu.roll(x, CHUNK - sh, 0), neg_inf)
            x = jnp.maximum(x, y)
        return x

    def body(i, ms):
        ms = list(ms)
        pending = []
        for d in range(2):
            c = i if d == 0 else _bwd_chunk(i, ncc, nch)
            r0 = pl.multiple_of(c * CHUNK, CHUNK)
            qk = qk_s[pl.ds(r0, CHUNK), :]
            q, k = qk[:, :nk], qk[:, nk:]
            v = v_ref[pl.ds(r0, CHUNK), :]
            mine = jnp.logical_and(lane >= d * A_HEADS, lane < (d + 1) * A_HEADS)
            keep = lambda x: jnp.where(mine[:x.shape[0]], x, 0.0)
            pre = sm_ref[pl.ds(r0, CHUNK), :]
            b = pltpu.roll(_dot(tri_ref[d], _split2(pre)), LANES - nu, 1)
            g = keep(pre - b)
            cm = cummax(g, d)
            g_t = g.T
            g_row = jnp.concatenate([g_t[d * A_HEADS + h:d * A_HEADS + h + 1, :] for h in range(A_HEADS)], axis=1)
            pm = jnp.exp(jnp.where(lmask_ref[d] > 0, g_row - _dot(_split2(cm, 1), exps_ref[d]), neg_inf))
            kst = jnp.tile(k, (A_HEADS, 1)) * kmask_ref[...]
            s = (_dot_nt(q, kst) * pm).astype(bf16)
            p1 = _dot(s, jnp.tile(v, (A_HEADS, 1)) * vmask_ref[...])
            sden = _dot(s, oseg_ref[d])
            cbd, nbd = c_s[d], n_s[d]
            qc = _dot(q, cbd.astype(bf16))
            qn = _dot(q, nbd.astype(bf16))
            m = ms[d]
            rmax, li = b + cm, b + m
            mj = jnp.maximum(li, rmax)
            alpha, beta = jnp.exp(rmax - mj), jnp.exp(li - mj)
            inv = 1.0 / jnp.maximum(jnp.abs(alpha * sden + beta * qn), jnp.exp(-mj))
            last = CHUNK - 1 if d == 0 else 0
            b_last, m_new, mloc = b[last:last + 1], mj[last:last + 1], rmax[last:last + 1]
            dec, gain = keep(jnp.exp(b_last + m - m_new)), keep(jnp.exp(mloc - m_new))
            sc = jnp.concatenate([keep(alpha * inv), keep(beta * inv), dec, gain, jnp.zeros((6, LANES), f32)], axis=0)
            scx = _dot(_split2(sc, 1), expv_ref[d])
            hout = scx[:CHUNK] * p1 + scx[CHUNK:2 * CHUNK] * qc
            wl = keep(jnp.exp(b_last - b + pre - mloc))
            kw = (k.astype(f32) * _dot(_split2(wl, 1), expd_ref[d])).astype(bf16)
            c_new = cbd * scx[2 * CHUNK:2 * CHUNK + 1] + (_dot_tn(kw, v) * cmask_ref[...]) * scx[2 * CHUNK + 1:2 * CHUNK + 2]
            n_new = nbd * dec + (_dot_tn(kw, ones_rows) * nmask_ref[d]) * gain
            ms[d] = keep(m_new)
            pending.append((r0, hout, c_new, n_new))
        for d, (r0, hout, c_new, n_new) in enumerate(pending):
            hacc[pl.ds(r0, CHUNK), :] += hout
            c_s[d] = c_new
            n_s[d] = n_new
        return tuple(ms)

    lax.fori_loop(0, nch, body, tuple(jnp.zeros((1, LANES), f32) for _ in range(2)))
    o_ref[...] = hacc[...].astype(bf16)


def _mlstm_constants():
    t = np.arange(CHUNK)
    lane = np.arange(LANES)
    hs = np.arange(A_HEADS * CHUNK)
    hd = np.arange(A_HEADS * A_DQK)
    hv = np.arange(A_HEADS * A_DV)
    tri3, lmask, exps, expd, expv, oseg, nmask = [], [], [], [], [], [], []
    for d in range(2):
        pos = t if d == 0 else CHUNK - 1 - t
        tri = pos[None, :] <= pos[:, None]
        tri3.append(np.tile(tri, (1, 2)))
        lmask.append(np.tile(tri, (1, A_HEADS)))
        unit = lambda idx, width: lane[:, None] == d * A_HEADS + idx[None, :] // width
        exps.append(np.tile(unit(hs, CHUNK), (2, 1)))
        expd.append(np.tile(unit(hd, A_DQK), (2, 1)))
        expv.append(np.tile(unit(hv, A_DV), (2, 1)))
        oseg.append(unit(hs, CHUNK).T)
        nmask.append(unit(hd, A_DQK).T)
    kmask = hs[:, None] // CHUNK == hd[None, :] // A_DQK
    vmask = hs[:, None] // CHUNK == hv[None, :] // A_DV
    cmask = hd[:, None] // A_DQK == hv[None, :] // A_DV
    as_b = lambda x: jnp.asarray(np.stack(x) if isinstance(x, list) else x, bf16)
    return (as_b(tri3), as_b(lmask), as_b(exps), as_b(expd), as_b(expv), as_b(oseg), as_b(kmask), as_b(vmask),
            jnp.asarray(cmask, f32), jnp.asarray(np.stack(nmask), f32))


def _mlstm(a_qk, a_v, small, conv_w, conv_b, nb, ctx_len):
    t = a_qk.shape[0]
    tb = t // nb
    consts = _mlstm_constants()
    const_spec = lambda a: pl.BlockSpec(a.shape, lambda b, n=a.ndim: (0,) * n, pipeline_mode=pl.Buffered(1))
    return pl.pallas_call(
        functools.partial(_mlstm_kernel, ctx_len=ctx_len),
        out_shape=SDS((t, MIX_W), bf16), grid=(nb,),
        in_specs=[pl.BlockSpec((tb, 512), lambda b: (b, 0)), pl.BlockSpec((tb, 512), lambda b: (b, 0)),
                  pl.BlockSpec((tb, LANES), lambda b: (b, 0)),
                  pl.BlockSpec((3, 512), lambda b: (0, 0)), pl.BlockSpec((1, 512), lambda b: (0, 0))]
                 + [const_spec(a) for a in consts],
        out_specs=pl.BlockSpec((tb, MIX_W), lambda b: (b, 0)),
        scratch_shapes=[pltpu.VMEM((tb, 512), bf16), pltpu.VMEM((tb, MIX_W), f32),
                        pltpu.VMEM((2, A_HEADS * A_DQK, A_HEADS * A_DV), f32),
                        pltpu.VMEM((2, A_HEADS * A_DQK, LANES), f32)],
        compiler_params=_cp(("arbitrary",)), name="mlstm",
    )(a_qk, a_v, small, conv_w, conv_b.reshape(1, -1), *consts)


GLA_LEVELS = CHUNK.bit_length() - 1


def _gla_constants():
    t = np.arange(CHUNK)
    tri, gref, lmask = [], [], []
    for d in range(2):
        pos = t if d == 0 else CHUNK - 1 - t
        row_of = np.argsort(pos)
        tri.append(pos[None, :] <= pos[:, None])
        g, m = [], []
        for lvl in range(GLA_LEVELS):
            half = CHUNK >> (lvl + 1)
            ref_pos = (pos // (2 * half)) * (2 * half) + half
            g.append(t[None, :] == row_of[ref_pos][:, None])
            late, early = (pos % (2 * half)) >= half, (pos % (2 * half)) < half
            same = (pos[:, None] // (2 * half)) == (pos[None, :] // (2 * half))
            m.append(same & late[:, None] & early[None, :])
        m.append(t[:, None] == t[None, :])
        gref.append(np.concatenate(g, axis=0))
        lmask.append(np.stack([np.tile(x, (1, C_HEADS)) for x in m]))
    tri3 = np.stack([np.tile(x, (1, 2)) for x in tri])
    gref3 = np.stack([np.tile(x, (1, 2)) for x in gref])
    hs = np.arange(C_HEADS * CHUNK)
    kmask = (hs[:, None] // CHUNK) == (np.arange(C_HEADS * C_DK)[None, :] // C_DK)
    vmask = (hs[:, None] // CHUNK) == (np.arange(C_HEADS * C_DV)[None, :] // C_DV)
    bdm = (np.arange(2 * C_DV)[:, None] // C_DV) == (np.arange(2 * C_DK)[None, :] // C_DK)
    as_b = lambda x: jnp.asarray(x, bf16)
    return as_b(tri3), as_b(gref3), jnp.asarray(np.stack(lmask), f32), as_b(kmask), as_b(vmask), jnp.asarray(bdm, f32)


def _gla_kernel(qk_ref, v_ref, la_ref, tri_ref, gref_ref, lmask_ref, kmask_ref, vmask_ref, bdm_ref,
                o_ref, hacc, s_s, *, ctx_len):
    tb = qk_ref.shape[0]
    nch = tb // CHUNK
    ncc = ctx_len // CHUNK
    nk = C_HEADS * C_DK
    pw = 2 * C_DK
    pv = 2 * C_DV
    npair = C_HEADS // 2

    hacc[...] = jnp.zeros_like(hacc)
    s_s[...] = jnp.zeros_like(s_s)
    qscale = C_DK ** -0.5

    def body(i, carry):
        for d in range(2):
            c = i if d == 0 else _bwd_chunk(i, ncc, nch)
            r0 = pl.multiple_of(c * CHUNK, CHUNK)
            la = la_ref[pl.ds(r0, CHUNK), nk * d:nk * (d + 1)]
            qk = qk_ref[pl.ds(r0, CHUNK), :]
            q = qk[:, :nk].astype(f32) * qscale
            k = qk[:, nk:].astype(f32)
            v = v_ref[pl.ds(r0, CHUNK), :]
            b = _dot(tri_ref[d], _split2(la))
            bref = _dot(gref_ref[d], _split2(b))

            def scores(qe, ke):
                kst = jnp.tile(ke, (C_HEADS, 1)) * kmask_ref[...]
                return _dot_nt(qe, kst)

            qb, kb = q.astype(bf16), k.astype(bf16)
            a = scores(qb, kb) * lmask_ref[d, GLA_LEVELS]
            for lvl in range(GLA_LEVELS):
                rel = b - bref[CHUNK * lvl:CHUNK * (lvl + 1)]
                s = scores(qb * jnp.exp(jnp.minimum(rel, 0.0)).astype(bf16),
                           kb * jnp.exp(jnp.minimum(-rel, 0.0)).astype(bf16))
                a = a + s * lmask_ref[d, lvl]
            vst = jnp.tile(v, (C_HEADS, 1)) * vmask_ref[...]
            o = _dot(a.astype(bf16), vst)

            last = CHUNK - 1 if d == 0 else 0
            bl = b[last:last + 1]
            qd = (q * jnp.exp(b)).astype(bf16)
            kd = (k * jnp.exp(bl - b)).astype(bf16)
            dec = jnp.exp(bl)
            o_int = []
            for p in range(npair):
                u = d * npair + p
                st = s_s[u]
                o_int.append(_dot_nt(qd[:, pw * p:pw * (p + 1)], st.astype(bf16)))
                upd = _dot_tn(v[:, pv * p:pv * (p + 1)], kd[:, pw * p:pw * (p + 1)])
                s_s[u] = st * dec[:, pw * p:pw * (p + 1)] + bdm_ref[...] * upd
            hacc[pl.ds(r0, CHUNK), :] += o + jnp.concatenate(o_int, axis=1)
        return carry

    lax.fori_loop(0, nch, body, 0)
    o_ref[...] = hacc[...].astype(bf16)


def _gla(c_qk, c_v, c_la, nb, ctx_len):
    t = c_qk.shape[0]
    tb = t // nb
    consts = _gla_constants()
    const_spec = lambda a: pl.BlockSpec(a.shape, lambda b, n=a.ndim: (0,) * n, pipeline_mode=pl.Buffered(1))
    return pl.pallas_call(
        functools.partial(_gla_kernel, ctx_len=ctx_len),
        out_shape=SDS((t, MIX_W), bf16), grid=(nb,),
        in_specs=[pl.BlockSpec((tb, 512), lambda b: (b, 0)), pl.BlockSpec((tb, 512), lambda b: (b, 0)),
                  pl.BlockSpec((tb, c_la.shape[1]), lambda b: (b, 0), pipeline_mode=pl.Buffered(1))]
                 + [const_spec(a) for a in consts],
        out_specs=pl.BlockSpec((tb, MIX_W), lambda b: (b, 0)),
        scratch_shapes=[pltpu.VMEM((tb, MIX_W), f32), pltpu.VMEM((C_HEADS, 2 * C_DV, 2 * C_DK), f32)],
        compiler_params=_cp(("arbitrary",)), name="gla",
    )(c_qk, c_v, c_la, *consts)


def _rope_tables(seq, ctx_len):
    n_f = B_DH // 4
    t = np.arange(seq)
    freqs = ROPE_THETA ** (-np.arange(n_f, dtype=np.float32) / n_f)
    hd = np.arange(B_DH)
    pos = np.where(hd[None, :] < B_DH // 2, (t // GRID_W)[:, None], (t % GRID_W)[:, None]).astype(np.float32)
    ang = jnp.asarray(pos * freqs[hd % n_f][None, :], f32)
    sign = np.where((hd % (2 * n_f)) < n_f, -1.0, 1.0).astype(np.float32)
    cos = jnp.concatenate([jnp.ones((ctx_len, B_DH), f32), jnp.cos(ang)], axis=0)
    sin = jnp.concatenate([jnp.zeros((ctx_len, B_DH), f32), jnp.sin(ang) * sign[None, :]], axis=0)
    return jnp.tile(cos, (1, 2)), jnp.tile(sin, (1, 2))


def _attn_prep(bq, bkv, cos, sin, qg, kg, qo_ref, ko_ref, vo_ref):
    def norm_rope(x, g):
        w = x.shape[1]
        bd = (_iota((w, w), 0) // B_DH == _iota((w, w), 1) // B_DH).astype(bf16)
        ss = _dot((x * x).astype(bf16), bd)
        xn = x * lax.rsqrt(ss * (1.0 / B_DH) + EPS) * g
        first = (_iota(x.shape, 1) % (B_DH // 2)) < (B_DH // 4)
        swapped = jnp.where(first, pltpu.roll(xn, w - B_DH // 4, 1), pltpu.roll(xn, B_DH // 4, 1))
        reps = w // LANES
        return xn * jnp.tile(cos, (1, reps)) + swapped * jnp.tile(sin, (1, reps))

    q = (norm_rope(bq, qg) * (B_DH ** -0.5)).astype(bf16)
    for h in range(B_HEADS):
        qo_ref[h] = q[:, B_DH * h:B_DH * (h + 1)]
    k = norm_rope(bkv[:, :B_KV * B_DH], kg).astype(bf16)
    v = bkv[:, B_KV * B_DH:].astype(bf16)
    ones_col = (_iota((bkv.shape[0], LANES - B_DH), 1) == 0).astype(bf16)
    for h in range(B_KV):
        ko_ref[h] = k[:, B_DH * h:B_DH * (h + 1)]
        vo_ref[h] = jnp.concatenate([v[:, B_DH * h:B_DH * (h + 1)], ones_col], axis=1)


def _attn_kernel(q_ref, k_ref, v_ref, o_ref, *, ctx_len):
    tb = k_ref.shape[1]
    q = q_ref[...].reshape(B_GROUP * TQ, B_DH)

    def attend(klen):
        starts = [0] + list(range(ctx_len, klen, KEY_CHUNK))
        m = acc = None
        for s0, s1 in zip(starts, starts[1:] + [klen]):
            s = _dot_nt(q, k_ref[0, s0:s1, :])
            smax = jnp.max(s, axis=-1, keepdims=True)
            if m is None:
                m = smax
                acc = _dot(jnp.exp((s - m).astype(bf16)), v_ref[0, s0:s1, :])
            else:
                m_new = jnp.maximum(m, smax)
                acc = jnp.exp(m - m_new) * acc + _dot(jnp.exp((s - m_new).astype(bf16)), v_ref[0, s0:s1, :])
                m = m_new
        o = acc[:, :B_DH] / acc[:, B_DH:B_DH + 1]
        o_ref[...] = o.reshape(B_GROUP, TQ, B_DH).astype(bf16)

    is_ctx = pl.program_id(2) < ctx_len // TQ

    @pl.when(is_ctx)
    def _():
        attend(ctx_len)

    @pl.when(jnp.logical_not(is_ctx))
    def _():
        attend(tb)


def _attn(q, k, v, nb, ctx_len):
    t = q.shape[1]
    tb = t // nb
    nq = tb // TQ
    return pl.pallas_call(
        functools.partial(_attn_kernel, ctx_len=ctx_len),
        out_shape=SDS((B_HEADS, t, B_DH), bf16), grid=(nb, B_KV, nq),
        in_specs=[pl.BlockSpec((B_GROUP, TQ, B_DH), lambda b, g, i: (g, b * nq + i, 0)),
                  pl.BlockSpec((1, tb, B_DH), lambda b, g, i: (g, b, 0)),
                  pl.BlockSpec((1, tb, LANES), lambda b, g, i: (g, b, 0))],
        out_specs=pl.BlockSpec((B_GROUP, TQ, B_DH), lambda b, g, i: (g, b * nq + i, 0)),
        compiler_params=_cp(("arbitrary", "arbitrary", "arbitrary")), name="attn",
    )(q, k, v)


def _head_rms(y, g, dv):
    parts = []
    for h in range(y.shape[1] // dv):
        yh = y[:, dv * h:dv * (h + 1)]
        parts.append(yh * lax.rsqrt(jnp.mean(yh * yh, axis=-1, keepdims=True) + EPS))
    return jnp.concatenate(parts, axis=1) * g


def _merge_kernel(ha_ref, ao_ref, att_ref, hc_ref, cr_ref, gate_ref, x_ref, g1a_ref, g1b_ref, sh2a_ref, sh2b_ref,
                  sc2a_ref, sc2b_ref, ag_ref, cg_ref, wb_ref, wo_ref, gn2_ref, wr_ref, br_ref,
                  xo_ref, h2_ref, ti_ref, tw_ref, rk_ref, cnt_ref, cnt_s):
    i = pl.program_id(0)

    @pl.when(i == 0)
    def _():
        cnt_s[...] = jnp.zeros_like(cnt_s)

    d = x_ref.shape[1]
    ya = _head_rms(ha_ref[...].astype(f32), ag_ref[...], A_DV) * _sigmoid(ao_ref[...].astype(f32))
    cr = cr_ref[...].astype(f32)
    yc = _head_rms(hc_ref[...].astype(f32), cg_ref[...], C_DV) * (cr * _sigmoid(cr))
    yb = jnp.concatenate([att_ref[h] for h in range(B_HEADS)], axis=1)
    merged = jnp.zeros((TM, d), f32)
    for n, y in enumerate((ya.astype(bf16), yb, yc.astype(bf16))):
        merged = merged + _sigmoid(gate_ref[:, d * n:d * (n + 1)]) * _dot(y, wb_ref[n])
    x = x_ref[...] + _rows(g1a_ref, g1b_ref) * _dot(merged.astype(bf16), wo_ref[...])
    xo_ref[...] = x
    h2 = _rms_mod(x, gn2_ref[...], _rows(sh2a_ref, sh2b_ref), _rows(sc2a_ref, sc2b_ref))
    h2_ref[...] = _pack_bf16_pairs(h2)

    logits = _dot_nt(wr_ref[...], h2, HI) + br_ref[...]
    eid = _iota((N_EXPERTS, TM), 0)
    work = logits
    onehot = jnp.zeros((N_EXPERTS, TM), f32)
    vals, sels = [], []
    for k in range(TOP_K):
        mk = jnp.max(work, axis=0, keepdims=True)
        ik = jnp.min(jnp.where(work == mk, eid, N_EXPERTS), axis=0, keepdims=True)
        sel = eid == ik
        work = jnp.where(sel, -jnp.inf, work)
        onehot = onehot + sel.astype(f32)
        ti_ref[k:k + 1, :] = ik
        vals.append(mk)
        sels.append(sel)
    ex = [jnp.exp(vk - vals[0]) for vk in vals]
    tot = ex[0] + ex[1] + ex[2] + ex[3]
    for k in range(TOP_K):
        tw_ref[k:k + 1, :] = ex[k] / tot

    ut = (_iota((TM, TM), 0) <= _iota((TM, TM), 1)).astype(bf16)
    incl = _dot(onehot.astype(bf16), ut)
    rank = cnt_s[...][:, 0:1] + incl - onehot
    for k in range(TOP_K):
        rk_ref[k:k + 1, :] = jnp.sum(jnp.where(sels[k], rank, 0.0), axis=0, keepdims=True).astype(jnp.int32)
    cnt_s[...] = cnt_s[...] + incl[:, TM - 1:TM]
    cnt_ref[...] = cnt_s[...]


def _merge(p, h_a, att, h_c, x, mods, a_norm_g, c_norm_g, w_branch, w_out, g_norm2, w_router, b_router,
           tpb, ctx_tiles, nb):
    t, d = x.shape
    row = lambda w: pl.BlockSpec((TM, w), lambda i: (i, 0))
    const = lambda shape: pl.BlockSpec(shape, lambda i: tuple(0 for _ in shape))
    return pl.pallas_call(
        _merge_kernel,
        out_shape=(SDS((t, d), f32), SDS((t, d // 2), jnp.int32), SDS((TOP_K, t), jnp.int32), SDS((TOP_K, t), f32),
                   SDS((TOP_K, t), jnp.int32), SDS((N_EXPERTS, LANES), f32)),
        grid=(t // TM,),
        in_specs=[row(512), row(512), pl.BlockSpec((B_HEADS, TM, B_DH), lambda i: (0, i, 0)), row(512), row(512),
                  row(3 * d), row(d)]
                 + _mod_specs(2, tpb, ctx_tiles, nb) + _mod_specs(3, tpb, ctx_tiles, nb) + _mod_specs(4, tpb, ctx_tiles, nb)
                 + [const((1, 512)), const((1, 512)), const((3, MIX_W, d)), const((d, d)), const((1, d)),
                  const((N_EXPERTS, d)), const((N_EXPERTS, 1))],
        out_specs=(row(d), row(d // 2), pl.BlockSpec((TOP_K, TM), lambda i: (0, i)), pl.BlockSpec((TOP_K, TM), lambda i: (0, i)),
                   pl.BlockSpec((TOP_K, TM), lambda i: (0, i)), const((N_EXPERTS, LANES))),
        scratch_shapes=[pltpu.VMEM((N_EXPERTS, LANES), f32)],
        compiler_params=_cp(("arbitrary",)), name="merge",
    )(h_a, p["a_o"], att, h_c, p["c_r"], p["gate"], x, mods, mods, mods, mods, mods, mods,
      a_norm_g.reshape(1, -1), c_norm_g.reshape(1, -1), w_branch, w_out, g_norm2.reshape(1, -1),
      w_router.T, b_router.reshape(-1, 1))


def _sc_mesh():
    return plsc.VectorSubcoreMesh(core_axis_name="c", subcore_axis_name="s")


def _sc_window(per_worker):
    return next(w for w in (SC_WINDOW, 32, 16, 8) if per_worker % w == 0)


def _sc_scatter_rows(src, idx, n_out):
    v, d = src.shape
    reps = idx.shape[0] // v
    per_w = v // (SC_NUM_CORES * SC_NUM_SUBCORES)
    win = _sc_window(per_w)
    assert idx.shape[0] == reps * v and v % (SC_NUM_CORES * SC_NUM_SUBCORES) == 0

    @functools.partial(pl.kernel, out_type=SDS((n_out, d), src.dtype), mesh=_sc_mesh(),
                       scratch_types=[pltpu.VMEM((win,), jnp.int32), pltpu.VMEM((win, d), src.dtype),
                                      pltpu.SemaphoreType.DMA])
    def k(x_hbm, i_hbm, o_hbm, idx_v, rows_v, sem):
        wid = lax.axis_index("s") * SC_NUM_CORES + lax.axis_index("c")

        @pl.loop(0, per_w // win)
        def _(j):
            base = wid * per_w + j * win
            pltpu.sync_copy(x_hbm.at[pl.ds(base, win)], rows_v)
            for r in range(reps):
                pltpu.sync_copy(i_hbm.at[pl.ds(r * v + base, win)], idx_v)
                pltpu.async_copy(rows_v, o_hbm.at[idx_v], sem).wait()

    return k(src, idx)


def _sc_gather_rows(table, idx):
    d = table.shape[1]
    n = idx.shape[0]
    per_w = n // (SC_NUM_CORES * SC_NUM_SUBCORES)
    win = _sc_window(per_w)
    assert n % (SC_NUM_CORES * SC_NUM_SUBCORES) == 0

    @functools.partial(pl.kernel, out_type=SDS((n, d), table.dtype), mesh=_sc_mesh(),
                       scratch_types=[pltpu.VMEM((win,), jnp.int32), pltpu.VMEM((win, d), table.dtype),
                                      pltpu.SemaphoreType.DMA])
    def k(x_hbm, i_hbm, o_hbm, idx_v, rows_v, sem):
        wid = lax.axis_index("s") * SC_NUM_CORES + lax.axis_index("c")

        @pl.loop(0, per_w // win)
        def _(j):
            base = wid * per_w + j * win
            pltpu.sync_copy(i_hbm.at[pl.ds(base, win)], idx_v)
            pltpu.async_copy(x_hbm.at[idx_v], rows_v, sem).wait()
            pltpu.sync_copy(rows_v, o_hbm.at[pl.ds(base, win)])

    return k(table, idx)


GU_BLOCK = 2 * LANES


def _deinterleave_perm():
    n = np.arange(GU_BLOCK)
    src = np.where(n < LANES, 2 * n, 2 * (n - LANES) + 1)
    return jnp.asarray(np.arange(GU_BLOCK)[:, None] == src[None, :], bf16)


def _expert_kernel(te_ref, nv_ref, x_ref, w1_ref, b1_ref, w2_ref, b2_ref, perm_ref, y_ref, w1_s, w2_s):
    i = pl.program_id(0)
    valid = i < nv_ref[0]
    new_expert = jnp.logical_or(i == 0, te_ref[i] != te_ref[jnp.maximum(i - 1, 0)])
    dff2 = w1_ref.shape[2]

    @pl.when(jnp.logical_and(valid, new_expert))
    def _():
        for cb in range(dff2 // GU_BLOCK):
            cs = slice(GU_BLOCK * cb, GU_BLOCK * (cb + 1))
            w1_s[:, cs] = _dot(w1_ref[0, :, cs].astype(bf16), perm_ref[...]).astype(bf16)
        w2_s[...] = w2_ref[0].astype(bf16)

    @pl.when(valid)
    def _():
        x = jnp.concatenate(_unpack_bf16_pairs(x_ref[...]), axis=1).astype(bf16)
        gu = _dot(x, w1_s[...]) + b1_ref[0]
        nblk = dff2 // GU_BLOCK
        g = jnp.concatenate([gu[:, GU_BLOCK * cb:GU_BLOCK * cb + LANES] for cb in range(nblk)], axis=1)
        u = jnp.concatenate([gu[:, GU_BLOCK * cb + LANES:GU_BLOCK * (cb + 1)] for cb in range(nblk)], axis=1)
        gate = jnp.minimum(g, SWIGLU_LIMIT)
        up = jnp.clip(u, -SWIGLU_LIMIT, SWIGLU_LIMIT)
        a = (up + 1.0) * gate * _sigmoid(SWIGLU_ALPHA * gate)
        y_ref[...] = _pack_bf16_pairs(_dot(a.astype(bf16), w2_s[...]) + b2_ref[0])


def _experts(xs, tile_e, n_valid, layer, w1, b1, w2, b2):
    p, dw = xs.shape
    d = 2 * dw
    dff2 = w1.shape[-1]
    dff = w2.shape[2]
    row = lambda i, te, nv: (jnp.minimum(i, nv[0] - 1), 0)
    wsel = lambda i, te, nv: (layer, te[i], 0, 0)
    sq = pl.Squeezed()
    return pl.pallas_call(
        _expert_kernel, out_shape=SDS((p, dw), jnp.int32),
        grid_spec=pltpu.PrefetchScalarGridSpec(
            num_scalar_prefetch=2, grid=(p // TME,),
            in_specs=[pl.BlockSpec((TME, dw), row),
                      pl.BlockSpec((sq, 1, d, dff2), wsel), pl.BlockSpec((sq, 1, 1, dff2), wsel),
                      pl.BlockSpec((sq, 1, dff, d), wsel), pl.BlockSpec((sq, 1, 1, d), wsel),
                      pl.BlockSpec((GU_BLOCK, GU_BLOCK), lambda i, te, nv: (0, 0))],
            out_specs=pl.BlockSpec((TME, dw), row),
            scratch_shapes=[pltpu.VMEM((d, dff2), bf16), pltpu.VMEM((dff, d), bf16)]),
        compiler_params=_cp(("arbitrary",)), name="experts",
    )(tile_e, n_valid, xs, w1, b1, w2, b2, _deinterleave_perm())


def _final_kernel(x_ref, yg_ref, w_ref, g2_ref, gf_ref, o_ref):
    x = x_ref[...] + g2_ref[0] * _weighted_expert_sum(yg_ref, w_ref[...])
    o_ref[...] = x * lax.rsqrt(jnp.mean(x * x, axis=-1, keepdims=True) + EPS) * gf_ref[...]


def _final(x, yg, wcol, mods, tpb, ctx_tiles, nb, g_final):
    d = x.shape[1]
    lat = tpb - ctx_tiles
    rmap = lambda i: ((i // lat) * tpb + ctx_tiles + i % lat)
    return pl.pallas_call(
        _final_kernel, out_shape=SDS((nb * lat * TMH, d), f32), grid=(nb * lat,),
        in_specs=[pl.BlockSpec((TMH, d), lambda i: (rmap(i), 0)),
                  pl.BlockSpec((TOP_K, TMH, d // 2), lambda i: (0, rmap(i), 0)),
                  pl.BlockSpec((TMH, TOP_K), lambda i: (rmap(i), 0)),
                  pl.BlockSpec((1, 1, 1024), lambda i: (i // lat, 0, 5)),
                  pl.BlockSpec((1, d), lambda i: (0, 0))],
        out_specs=pl.BlockSpec((TMH, d), lambda i: (i, 0)),
        compiler_params=_cp(("arbitrary",)), name="final",
    )(x, yg, wcol, mods, g_final.reshape(1, -1))


def _routing_tables(top_i, rank, counts, n_tiles):
    cnt = counts[:, 0].astype(jnp.int32)
    padded = ((cnt + TME - 1) // TME) * TME
    ends = jnp.cumsum(padded)
    starts = ends - padded
    eids = jnp.arange(N_EXPERTS, dtype=jnp.int32)
    start_of = jnp.sum(jnp.where(top_i[..., None] == eids, starts, 0), axis=-1)
    pos = (start_of + rank).reshape(-1)
    n_valid = ends[-1] // TME
    tile_start = jnp.arange(n_tiles, dtype=jnp.int32) * TME
    tile_e = jnp.sum(tile_start[:, None] >= ends[None, :], axis=1).astype(jnp.int32)
    tile_e = jnp.where(jnp.arange(n_tiles) < n_valid, tile_e, tile_e[jnp.maximum(n_valid - 1, 0)])
    tile_e = jnp.minimum(tile_e, N_EXPERTS - 1)
    return pos, tile_e, n_valid.reshape(1).astype(jnp.int32)


def kernel(x, c, ctx, c_ctx, w_ada, b_ada, g_norm1, w_in, a_conv_w, a_conv_b, a_gate_b, a_norm_g, b_q_norm_g, b_k_norm_g, c_w_up, c_b_up, c_norm_g, w_branch, w_out, g_norm2, w_router, b_router, w_e1, b_e1, w_e2, b_e2, g_final):
    nb, seq, d = x.shape
    ctx_len = ctx.shape[1]
    depth = w_ada.shape[0]
    tb = ctx_len + seq
    t = nb * tb
    tpb, ctx_tiles = tb // TMH, ctx_len // TMH
    assert d == 1024 and nb < 16 and seq % TMH == 0 and ctx_len % TMH == 0 and ctx_len % TQ == 0 and t % TM == 0
    groups = 2 if nb % 2 == 0 and (nb // 2 * tb) % TM == 0 else 1
    nbg = nb // groups
    tg = nbg * tb
    n_sorted = TOP_K * tg + N_EXPERTS * TME
    n_tiles = n_sorted // TME

    x_all = jnp.concatenate([ctx, x], axis=1)
    xs = [x_all[g * nbg:(g + 1) * nbg].reshape(tg, d) for g in range(groups)]
    cc = jnp.zeros((16, d), f32).at[:nb].set(c).at[nb].set(c_ctx)
    mods_all = _adaln(cc, w_ada, b_ada)
    pad_rows = jnp.zeros((depth, 16 - nbg - 1, 6 * d), f32)
    mods_grp = [jnp.concatenate([mods_all[:, g * nbg:(g + 1) * nbg], mods_all[:, nb:nb + 1], pad_rows], axis=1)
                for g in range(groups)]
    cos, sin = _rope_tables(seq, ctx_len)
    w_in_r = _arrange_w_in(w_in)
    w_branch_b, w_out_b = w_branch.astype(bf16), w_out.astype(bf16)
    col = np.arange(b_e1.shape[-1])
    within = col % GU_BLOCK
    src = (col // GU_BLOCK) * GU_BLOCK + np.where(within < LANES, 2 * within, 2 * (within - LANES) + 1)
    b1 = b_e1[..., src][..., None, :]
    b2 = b_e2[..., None, :]

    outs = [None] * groups
    prev = [None] * groups
    for l in range(depth):
        for g in range(groups):
            mods = mods_grp[g][l].reshape(16, 1, 6 * d)
            x_g, p, (qn, kn, vn) = _inproj(xs[g], mods, g_norm1[l].reshape(1, -1), w_in_r[l], cos, sin,
                                           b_q_norm_g[l], b_k_norm_g[l], a_gate_b[l], c_w_up[l], c_b_up[l],
                                           tpb, ctx_tiles, nbg, prev=prev[g])
            h_a = _mlstm(p["a_qk"], p["a_v"], p["small"], a_conv_w[l], a_conv_b[l], nbg, ctx_len)
            h_c = _gla(p["c_qk"], p["c_v"], p["c_la"], nbg, ctx_len)
            att = _attn(qn, kn, vn, nbg, ctx_len)
            xs[g], h2, top_i, top_w, rank, counts = _merge(
                p, h_a, att, h_c, x_g, mods, a_norm_g[l], c_norm_g[l], w_branch_b[l], w_out_b[l], g_norm2[l],
                w_router[l], b_router[l], tpb, ctx_tiles, nbg)
            pos, tile_e, n_valid = _routing_tables(top_i, rank, counts, n_tiles)
            x_sorted = _sc_scatter_rows(h2, pos, n_sorted)
            y_sorted = _experts(x_sorted, tile_e, n_valid, l, w_e1, b1, w_e2, b2)
            yg = _sc_gather_rows(y_sorted, pos).reshape(TOP_K, tg, d // 2)
            if l == depth - 1:
                outs[g] = _final(xs[g], yg, top_w.T, mods, tpb, ctx_tiles, nbg, g_final)
            else:
                prev[g] = (yg, top_w.T, mods)
    return jnp.concatenate(outs, axis=0).reshape(nb, seq, d)
```

```python
import functools

import jax
import jax.numpy as jnp
import numpy as np
from jax import lax
from jax.experimental import pallas as pl
from jax.experimental.pallas import tpu as pltpu
from jax.experimental.pallas import tpu_sc as plsc

f32 = jnp.float32
bf16 = jnp.bfloat16
HI = lax.Precision.HIGHEST
SDS = jax.ShapeDtypeStruct

EPS = 1e-6
CHUNK = 128
GRID_W = 64
ROPE_THETA = 10000.0
MIX_W = 512
A_HEADS, A_DQK, A_DV = 4, 64, 128
B_HEADS, B_KV, B_DH = 8, 2, 64
B_GROUP = B_HEADS // B_KV
C_HEADS, C_DK, C_DV, C_RANK, C_TAU = 4, 64, 128, 16, 16.0
N_EXPERTS, TOP_K = 32, 4
SWIGLU_LIMIT, SWIGLU_ALPHA = 7.0, 1.702

LANES = 128
SC_NUM_CORES = 2
SC_NUM_SUBCORES = 16
VMEM_LIMIT = 56 * 2 ** 20

TMH = 256
TM = 2 * TMH
TQ = 256
TME = 1024
KEY_CHUNK = 2048
SC_WINDOW = 64


def _cp(sem, vmem=VMEM_LIMIT):
    return pltpu.CompilerParams(dimension_semantics=sem, vmem_limit_bytes=vmem)


def _dot(a, b, precision=None):
    return jnp.dot(a, b, preferred_element_type=f32, precision=precision)


def _dot_nt(a, b, precision=None):
    return lax.dot_general(a, b, (((1,), (1,)), ((), ())), preferred_element_type=f32, precision=precision)


def _dot_tn(a, b):
    return lax.dot_general(a, b, (((0,), (0,)), ((), ())), preferred_element_type=f32)


def _sigmoid(x):
    return 0.5 * jnp.tanh(0.5 * x) + 0.5


def _log_sigmoid(x):
    return jnp.minimum(x, 0.0) - jnp.log(1.0 + jnp.exp(-jnp.abs(x)))


def _iota(shape, axis):
    return lax.broadcasted_iota(jnp.int32, shape, axis)


def _pack_bf16_pairs(x):
    n = x.shape[1] // 2
    lo = pltpu.bitcast(x[:, :n].astype(bf16).astype(f32), jnp.uint32)
    hi = pltpu.bitcast(x[:, n:].astype(bf16).astype(f32), jnp.uint32)
    return pltpu.bitcast((lo >> 16) | hi, jnp.int32)


def _unpack_bf16_pairs(w):
    u = pltpu.bitcast(w, jnp.uint32)
    return pltpu.bitcast(u << 16, f32), pltpu.bitcast(u & jnp.uint32(0xFFFF0000), f32)


def _rms_mod(x, g, shift, scale):
    y = x * lax.rsqrt(jnp.mean(x * x, axis=-1, keepdims=True) + EPS) * g
    return y * (1.0 + scale) + shift


def _adaln_kernel(cc_ref, w_ref, b_ref, o_ref):
    cc = cc_ref[...]
    s = cc * _sigmoid(cc)
    o_ref[0] = _dot(s, w_ref[0], HI) + b_ref[0]


def _adaln(cc, w_ada, b_ada):
    nl, d, n = w_ada.shape
    return pl.pallas_call(
        _adaln_kernel, out_shape=SDS((nl, 16, n), f32), grid=(nl, n // 1024),
        in_specs=[pl.BlockSpec((16, d), lambda l, j: (0, 0)),
                  pl.BlockSpec((1, d, 1024), lambda l, j: (l, 0, j)),
                  pl.BlockSpec((1, 1, 1024), lambda l, j: (l, 0, j))],
        out_specs=pl.BlockSpec((1, 16, 1024), lambda l, j: (l, 0, j)),
        compiler_params=_cp(("arbitrary", "arbitrary")), name="adaln",
    )(cc, w_ada, b_ada.reshape(nl, 1, n))


def _mod_specs(col, tpb, ctx_tiles, nb):
    def spec(half):
        def imap(i):
            u = 2 * i + half
            return (jnp.where((u % tpb) >= ctx_tiles, u // tpb, nb), 0, col)
        return pl.BlockSpec((1, 1, 1024), imap)
    return [spec(0), spec(1)]


def _rows(a_ref, b_ref):
    a, b = a_ref[0], b_ref[0]
    return jnp.concatenate([jnp.broadcast_to(a, (TMH, a.shape[1])), jnp.broadcast_to(b, (TMH, b.shape[1]))], axis=0)


_K1_GROUPS = (("a_qk", 512, bf16), ("a_v", 512, bf16), ("a_o", 512, bf16), ("b_q", 512, bf16),
              ("b_kv", 256, bf16), ("c_qk", 512, bf16), ("c_v", 512, bf16), ("c_r", 512, bf16),
              ("gate", 3072, bf16), ("small", 128, f32))
_K1_WIDTH = sum(w for _, w, _ in _K1_GROUPS)


def _arrange_w_in(w_in):
    z = jnp.zeros(w_in.shape[:-1] + (LANES - 48,), w_in.dtype)
    w = jnp.concatenate([w_in[..., 0:1024], w_in[..., 1040:1552], w_in[..., 1552:2320], w_in[..., 2320:3344],
                         w_in[..., 3376:3888], w_in[..., 3888:6960], w_in[..., 1024:1040], w_in[..., 3344:3376], z],
                        axis=-1)
    assert w.shape[-1] == _K1_WIDTH
    return w.astype(bf16)


_ATTN_GROUPS = ("b_q", "b_kv")
_STORED_GROUPS = tuple(g for g in _K1_GROUPS if g[0] not in _ATTN_GROUPS)


def _weighted_expert_sum(yg_ref, w):
    acc_lo = acc_hi = None
    for k in range(TOP_K):
        lo, hi = _unpack_bf16_pairs(yg_ref[k])
        wk = w[:, k:k + 1]
        acc_lo = wk * lo if acc_lo is None else acc_lo + wk * lo
        acc_hi = wk * hi if acc_hi is None else acc_hi + wk * hi
    return jnp.concatenate([acc_lo, acc_hi], axis=1)


def _inproj_kernel(*refs, fuse_prev):
    refs = list(refs)
    x_ref = refs.pop(0)
    x = x_ref[...]
    if fuse_prev:
        yg_ref, wcol_ref, g2a_ref, g2b_ref = refs[:4]
        refs = refs[4:]
        x = x + _rows(g2a_ref, g2b_ref) * _weighted_expert_sum(yg_ref, wcol_ref[...])
    (sha_ref, shb_ref, sca_ref, scb_ref, g_ref, w_ref, cosa_ref, cosb_ref, sina_ref, sinb_ref, qg_ref, kg_ref,
     gb_ref, wup_ref, bup_ref) = refs[:15]
    outs = refs[15:]
    if fuse_prev:
        outs.pop(0)[...] = x
    h = _rms_mod(x, g_ref[...], _rows(sha_ref, shb_ref), _rows(sca_ref, scb_ref)).astype(bf16)
    c0 = 0
    attn_in = {}
    n_stored = len(_STORED_GROUPS) + 1
    stored = iter(outs[:n_stored])
    for name, width, dt in _K1_GROUPS:
        if name in _ATTN_GROUPS:
            attn_in[name] = _dot(h, w_ref[:, c0:c0 + width])
        elif name == "small":
            sm = _dot(h, w_ref[:, c0:c0 + width]) + gb_ref[...]
            lane = _iota(sm.shape, 1)
            is_f = jnp.logical_and(lane >= 2 * A_HEADS, lane < 4 * A_HEADS)
            next(stored)[...] = jnp.where(is_f, _log_sigmoid(sm), sm)
            la_ref = next(stored)
            sm_hi = sm.astype(bf16)
            sm_lo = (sm - sm_hi.astype(f32)).astype(bf16)
            sm3 = jnp.concatenate([sm_hi, sm_lo, sm_hi], axis=1)
            for dr in range(2):
                nk = C_HEADS * C_DK
                la_ref[:, nk * dr:nk * (dr + 1)] = _log_sigmoid(_dot(sm3, wup_ref[dr]) + bup_ref[dr]) * (1.0 / C_TAU)
        else:
            o_ref = next(stored)
            for j in range(0, width, 512):
                wj = min(512, width - j)
                o_ref[:, j:j + wj] = _dot(h, w_ref[:, c0 + j:c0 + j + wj]).astype(dt)
        c0 += width
    qo_ref, ko_ref, vo_ref = outs[n_stored:]
    cos = jnp.concatenate([cosa_ref[...], cosb_ref[...]], axis=0)
    sin = jnp.concatenate([sina_ref[...], sinb_ref[...]], axis=0)
    _attn_prep(attn_in["b_q"], attn_in["b_kv"], cos, sin, qg_ref[...], kg_ref[...], qo_ref, ko_ref, vo_ref)


def _inproj(x, mods, g, w, cos, sin, qg, kg, gate_b, w_up, b_up, tpb, ctx_tiles, nb, prev=None):
    t, d = x.shape
    row = lambda wd: pl.BlockSpec((TM, wd), lambda i: (i, 0))
    const = lambda shape: pl.BlockSpec(shape, lambda i: (0,) * len(shape))
    in_specs, args = [row(d)], [x]
    out_shape, out_specs = [], []
    nk = C_HEADS * C_DK
    gb = jnp.zeros((1, LANES), f32).at[0, :4 * A_HEADS].set(gate_b.reshape(-1))
    wz = jnp.zeros((2, LANES, nk), f32)
    wz = wz.at[0, 16:16 + C_RANK].set(w_up[0]).at[1, 16 + C_RANK:16 + 2 * C_RANK].set(w_up[1])
    wz_hi = wz.astype(bf16)
    wz_lo = (wz - wz_hi.astype(f32)).astype(bf16)
    wz = jnp.concatenate([wz_hi, wz_hi, wz_lo], axis=1)
    if prev is not None:
        yg, wcol, mods_prev = prev
        in_specs += [pl.BlockSpec((TOP_K, TM, d // 2), lambda i: (0, i, 0)), row(TOP_K)]
        in_specs += _mod_specs(5, tpb, ctx_tiles, nb)
        args += [yg, wcol, mods_prev, mods_prev]
        out_shape.append(SDS((t, d), f32))
        out_specs.append(row(d))
    table = lambda half: pl.BlockSpec((TMH, LANES), lambda i: ((2 * i + half) % tpb, 0))
    in_specs += _mod_specs(0, tpb, ctx_tiles, nb) + _mod_specs(1, tpb, ctx_tiles, nb)
    in_specs += [const((1, d)), pl.BlockSpec((d, _K1_WIDTH), lambda i: (0, 0), pipeline_mode=pl.Buffered(1)),
                 table(0), table(1), table(0), table(1), const((1, B_HEADS * B_DH)), const((1, B_KV * B_DH)),
                 const((1, LANES)), const((2, 3 * LANES, nk)), const((2, 1, nk))]
    args += [mods, mods, mods, mods, g, w, cos, cos, sin, sin,
             jnp.tile(qg, B_HEADS).reshape(1, -1), jnp.tile(kg, B_KV).reshape(1, -1), gb, wz, b_up.reshape(2, 1, nk)]
    out_shape += [SDS((t, wd), dt) for _, wd, dt in _STORED_GROUPS] + [SDS((t, 2 * nk), f32)]
    out_specs += [row(wd) for _, wd, _ in _STORED_GROUPS] + [row(2 * nk)]
    out_shape += [SDS((B_HEADS, t, B_DH), bf16), SDS((B_KV, t, B_DH), bf16), SDS((B_KV, t, LANES), bf16)]
    out_specs += [pl.BlockSpec((B_HEADS, TM, B_DH), lambda i: (0, i, 0)),
                  pl.BlockSpec((B_KV, TM, B_DH), lambda i: (0, i, 0)),
                  pl.BlockSpec((B_KV, TM, LANES), lambda i: (0, i, 0))]
    res = list(pl.pallas_call(
        functools.partial(_inproj_kernel, fuse_prev=prev is not None),
        out_shape=tuple(out_shape), grid=(t // TM,), in_specs=in_specs, out_specs=tuple(out_specs),
        compiler_params=_cp(("arbitrary",)), name="inproj",
    )(*args))
    x_new = res.pop(0) if prev is not None else x
    p = dict(zip([n for n, _, _ in _STORED_GROUPS] + ["c_la"], res))
    return x_new, p, tuple(res[len(_STORED_GROUPS) + 1:])


def _bwd_chunk(i, ncc, nch):
    return jnp.where(i < ncc, ncc - 1 - i, nch - 1 + ncc - i)


def _split2(x, axis=0):
    hi = x.astype(bf16)
    return jnp.concatenate([hi, (x - hi.astype(f32)).astype(bf16)], axis=axis)


def _mlstm_kernel(qk_ref, v_ref, sm_ref, cw_ref, cb_ref, tri_ref, lmask_ref, exps_ref, expd_ref, expv_ref,
                  oseg_ref, kmask_ref, vmask_ref, cmask_ref, nmask_ref, o_ref, qk_s, hacc, c_s, n_s, *, ctx_len):
    tb = qk_ref.shape[0]
    nch = tb // CHUNK
    ncc = ctx_len // CHUNK
    nu = 2 * A_HEADS

    cw = cw_ref[...]
    cb = cb_ref[...]
    row = _iota((CHUNK, 2 * A_HEADS * A_DQK), 0)
    kscale = jnp.where(_iota((1, 2 * A_HEADS * A_DQK), 1) >= A_HEADS * A_DQK, A_DQK ** -0.5, 1.0)

    def conv_body(c, carry):
        r0 = pl.multiple_of(c * CHUNK, CHUNK)
        x = qk_ref[pl.ds(r0, CHUNK), :].astype(f32)
        rp = pl.multiple_of(jnp.maximum(r0 - 16, 0), 16)
        rn = pl.multiple_of(jnp.minimum(r0 + CHUNK, tb - 16), 16)
        prev = qk_ref[pl.ds(rp, 16), :].astype(f32)[15:16, :]
        nxt = qk_ref[pl.ds(rn, 16), :].astype(f32)[0:1, :]
        has_prev = jnp.logical_and(r0 != 0, r0 != ctx_len)
        has_next = jnp.logical_and(r0 + CHUNK != ctx_len, r0 + CHUNK != tb)
        prev = jnp.where(has_prev, prev, 0.0)
        nxt = jnp.where(has_next, nxt, 0.0)
        xm = jnp.where(row == 0, prev, pltpu.roll(x, 1, 0))
        xp = jnp.where(row == CHUNK - 1, nxt, pltpu.roll(x, CHUNK - 1, 0))
        y = cw[0:1] * xm + cw[1:2] * x + cw[2:3] * xp + cb
        y = y * _sigmoid(y) * kscale
        qk_s[pl.ds(r0, CHUNK), :] = y.astype(bf16)
        return carry

    lax.fori_loop(0, nch, conv_body, 0)

    hacc[...] = jnp.zeros_like(hacc)
    c_s[...] = jnp.zeros_like(c_s)
    n_s[...] = jnp.zeros_like(n_s)

    lane = _iota((CHUNK, LANES), 1)
    row = _iota((CHUNK, LANES), 0)
    neg_inf = jnp.float32(-jnp.inf)
    nk = A_HEADS * A_DQK
    ones_rows = jnp.ones((CHUNK, LANES), bf16)

    def cummax(x, d):
        for sh in [1 << e for e in range(CHUNK.bit_length() - 1)]:
            if d == 0:
                y = jnp.where(row >= sh, pltpu.roll(x, sh, 0), neg_inf)
            else:
                y = jnp.where(row < CHUNK - sh, pltpu.roll(x, CHUNK - sh, 0), neg_inf)
            x = jnp.maximum(x, y)
        return x

    def body(i, ms):
        ms = list(ms)
        pending = []
        for d in range(2):
            c = i if d == 0 else _bwd_chunk(i, ncc, nch)
            r0 = pl.multiple_of(c * CHUNK, CHUNK)
            qk = qk_s[pl.ds(r0, CHUNK), :]
            q, k = qk[:, :nk], qk[:, nk:]
            v = v_ref[pl.ds(r0, CHUNK), :]
            mine = jnp.logical_and(lane >= d * A_HEADS, lane < (d + 1) * A_HEADS)
            keep = lambda x: jnp.where(mine[:x.shape[0]], x, 0.0)
            pre = sm_ref[pl.ds(r0, CHUNK), :]
            b = pltpu.roll(_dot(tri_ref[d], _split2(pre)), LANES - nu, 1)
            g = keep(pre - b)
            cm = cummax(g, d)
            g_t = g.T
            g_row = jnp.concatenate([g_t[d * A_HEADS + h:d * A_HEADS + h + 1, :] for h in range(A_HEADS)], axis=1)
            pm = jnp.exp(jnp.where(lmask_ref[d] > 0, g_row - _dot(_split2(cm, 1), exps_ref[d]), neg_inf))
            kst = jnp.tile(k, (A_HEADS, 1)) * kmask_ref[...]
            s = (_dot_nt(q, kst) * pm).astype(bf16)
            p1 = _dot(s, jnp.tile(v, (A_HEADS, 1)) * vmask_ref[...])
            sden = _dot(s, oseg_ref[d])
            cbd, nbd = c_s[d], n_s[d]
            qc = _dot(q, cbd.astype(bf16))
            qn = _dot(q, nbd.astype(bf16))
            m = ms[d]
            rmax, li = b + cm, b + m
            mj = jnp.maximum(li, rmax)
            alpha, beta = jnp.exp(rmax - mj), jnp.exp(li - mj)
            inv = 1.0 / jnp.maximum(jnp.abs(alpha * sden + beta * qn), jnp.exp(-mj))
            last = CHUNK - 1 if d == 0 else 0
            b_last, m_new, mloc = b[last:last + 1], mj[last:last + 1], rmax[last:last + 1]
            dec, gain = keep(jnp.exp(b_last + m - m_new)), keep(jnp.exp(mloc - m_new))
            sc = jnp.concatenate([keep(alpha * inv), keep(beta * inv), dec, gain, jnp.zeros((6, LANES), f32)], axis=0)
            scx = _dot(_split2(sc, 1), expv_ref[d])
            hout = scx[:CHUNK] * p1 + scx[CHUNK:2 * CHUNK] * qc
            wl = keep(jnp.exp(b_last - b + pre - mloc))
            kw = (k.astype(f32) * _dot(_split2(wl, 1), expd_ref[d])).astype(bf16)
            c_new = cbd * scx[2 * CHUNK:2 * CHUNK + 1] + (_dot_tn(kw, v) * cmask_ref[...]) * scx[2 * CHUNK + 1:2 * CHUNK + 2]
            n_new = nbd * dec + (_dot_tn(kw, ones_rows) * nmask_ref[d]) * gain
            ms[d] = keep(m_new)
            pending.append((r0, hout, c_new, n_new))
        for d, (r0, hout, c_new, n_new) in enumerate(pending):
            hacc[pl.ds(r0, CHUNK), :] += hout
            c_s[d] = c_new
            n_s[d] = n_new
        return tuple(ms)

    lax.fori_loop(0, nch, body, tuple(jnp.zeros((1, LANES), f32) for _ in range(2)))
    o_ref[...] = hacc[...].astype(bf16)


def _mlstm_constants():
    t = np.arange(CHUNK)
    lane = np.arange(LANES)
    hs = np.arange(A_HEADS * CHUNK)
    hd = np.arange(A_HEADS * A_DQK)
    hv = np.arange(A_HEADS * A_DV)
    tri3, lmask, exps, expd, expv, oseg, nmask = [], [], [], [], [], [], []
    for d in range(2):
        pos = t if d == 0 else CHUNK - 1 - t
        tri = pos[None, :] <= pos[:, None]
        tri3.append(np.tile(tri, (1, 2)))
        lmask.append(np.tile(tri, (1, A_HEADS)))
        unit = lambda idx, width: lane[:, None] == d * A_HEADS + idx[None, :] // width
        exps.append(np.tile(unit(hs, CHUNK), (2, 1)))
        expd.append(np.tile(unit(hd, A_DQK), (2, 1)))
        expv.append(np.tile(unit(hv, A_DV), (2, 1)))
        oseg.append(unit(hs, CHUNK).T)
        nmask.append(unit(hd, A_DQK).T)
    kmask = hs[:, None] // CHUNK == hd[None, :] // A_DQK
    vmask = hs[:, None] // CHUNK == hv[None, :] // A_DV
    cmask = hd[:, None] // A_DQK == hv[None, :] // A_DV
    as_b = lambda x: jnp.asarray(np.stack(x) if isinstance(x, list) else x, bf16)
    return (as_b(tri3), as_b(lmask), as_b(exps), as_b(expd), as_b(expv), as_b(oseg), as_b(kmask), as_b(vmask),
            jnp.asarray(cmask, f32), jnp.asarray(np.stack(nmask), f32))


def _mlstm(a_qk, a_v, small, conv_w, conv_b, nb, ctx_len):
    t = a_qk.shape[0]
    tb = t // nb
    consts = _mlstm_constants()
    const_spec = lambda a: pl.BlockSpec(a.shape, lambda b, n=a.ndim: (0,) * n, pipeline_mode=pl.Buffered(1))
    return pl.pallas_call(
        functools.partial(_mlstm_kernel, ctx_len=ctx_len),
        out_shape=SDS((t, MIX_W), bf16), grid=(nb,),
        in_specs=[pl.BlockSpec((tb, 512), lambda b: (b, 0)), pl.BlockSpec((tb, 512), lambda b: (b, 0)),
                  pl.BlockSpec((tb, LANES), lambda b: (b, 0)),
                  pl.BlockSpec((3, 512), lambda b: (0, 0)), pl.BlockSpec((1, 512), lambda b: (0, 0))]
                 + [const_spec(a) for a in consts],
        out_specs=pl.BlockSpec((tb, MIX_W), lambda b: (b, 0)),
        scratch_shapes=[pltpu.VMEM((tb, 512), bf16), pltpu.VMEM((tb, MIX_W), f32),
                        pltpu.VMEM((2, A_HEADS * A_DQK, A_HEADS * A_DV), f32),
                        pltpu.VMEM((2, A_HEADS * A_DQK, LANES), f32)],
        compiler_params=_cp(("arbitrary",)), name="mlstm",
    )(a_qk, a_v, small, conv_w, conv_b.reshape(1, -1), *consts)


GLA_LEVELS = CHUNK.bit_length() - 1


def _gla_constants():
    t = np.arange(CHUNK)
    tri, gref, lmask = [], [], []
    for d in range(2):
        pos = t if d == 0 else CHUNK - 1 - t
        row_of = np.argsort(pos)
        tri.append(pos[None, :] <= pos[:, None])
        g, m = [], []
        for lvl in range(GLA_LEVELS):
            half = CHUNK >> (lvl + 1)
            ref_pos = (pos // (2 * half)) * (2 * half) + half
            g.append(t[None, :] == row_of[ref_pos][:, None])
            late, early = (pos % (2 * half)) >= half, (pos % (2 * half)) < half
            same = (pos[:, None] // (2 * half)) == (pos[None, :] // (2 * half))
            m.append(same & late[:, None] & early[None, :])
        m.append(t[:, None] == t[None, :])
        gref.append(np.concatenate(g, axis=0))
        lmask.append(np.stack([np.tile(x, (1, C_HEADS)) for x in m]))
    tri3 = np.stack([np.tile(x, (1, 2)) for x in tri])
    gref3 = np.stack([np.tile(x, (1, 2)) for x in gref])
    hs = np.arange(C_HEADS * CHUNK)
    kmask = (hs[:, None] // CHUNK) == (np.arange(C_HEADS * C_DK)[None, :] // C_DK)
    vmask = (hs[:, None] // CHUNK) == (np.arange(C_HEADS * C_DV)[None, :] // C_DV)
    bdm = (np.arange(2 * C_DV)[:, None] // C_DV) == (np.arange(2 * C_DK)[None, :] // C_DK)
    as_b = lambda x: jnp.asarray(x, bf16)
    return as_b(tri3), as_b(gref3), jnp.asarray(np.stack(lmask), f32), as_b(kmask), as_b(vmask), jnp.asarray(bdm, f32)


def _gla_kernel(qk_ref, v_ref, la_ref, tri_ref, gref_ref, lmask_ref, kmask_ref, vmask_ref, bdm_ref,
                o_ref, hacc, s_s, *, ctx_len):
    tb = qk_ref.shape[0]
    nch = tb // CHUNK
    ncc = ctx_len // CHUNK
    nk = C_HEADS * C_DK
    pw = 2 * C_DK
    pv = 2 * C_DV
    npair = C_HEADS // 2

    hacc[...] = jnp.zeros_like(hacc)
    s_s[...] = jnp.zeros_like(s_s)
    qscale = C_DK ** -0.5

    def body(i, carry):
        for d in range(2):
            c = i if d == 0 else _bwd_chunk(i, ncc, nch)
            r0 = pl.multiple_of(c * CHUNK, CHUNK)
            la = la_ref[pl.ds(r0, CHUNK), nk * d:nk * (d + 1)]
            qk = qk_ref[pl.ds(r0, CHUNK), :]
            q = qk[:, :nk].astype(f32) * qscale
            k = qk[:, nk:].astype(f32)
            v = v_ref[pl.ds(r0, CHUNK), :]
            b = _dot(tri_ref[d], _split2(la))
            bref = _dot(gref_ref[d], _split2(b))

            def scores(qe, ke):
                kst = jnp.tile(ke, (C_HEADS, 1)) * kmask_ref[...]
                return _dot_nt(qe, kst)

            qb, kb = q.astype(bf16), k.astype(bf16)
            a = scores(qb, kb) * lmask_ref[d, GLA_LEVELS]
            for lvl in range(GLA_LEVELS):
                rel = b - bref[CHUNK * lvl:CHUNK * (lvl + 1)]
                s = scores(qb * jnp.exp(jnp.minimum(rel, 0.0)).astype(bf16),
                           kb * jnp.exp(jnp.minimum(-rel, 0.0)).astype(bf16))
                a = a + s * lmask_ref[d, lvl]
            vst = jnp.tile(v, (C_HEADS, 1)) * vmask_ref[...]
            o = _dot(a.astype(bf16), vst)

            last = CHUNK - 1 if d == 0 else 0
            bl = b[last:last + 1]
            qd = (q * jnp.exp(b)).astype(bf16)
            kd = (k * jnp.exp(bl - b)).astype(bf16)
            dec = jnp.exp(bl)
            o_int = []
            for p in range(npair):
                u = d * npair + p
                st = s_s[u]
                o_int.append(_dot_nt(qd[:, pw * p:pw * (p + 1)], st.astype(bf16)))
                upd = _dot_tn(v[:, pv * p:pv * (p + 1)], kd[:, pw * p:pw * (p + 1)])
                s_s[u] = st * dec[:, pw * p:pw * (p + 1)] + bdm_ref[...] * upd
            hacc[pl.ds(r0, CHUNK), :] += o + jnp.concatenate(o_int, axis=1)
        return carry

    lax.fori_loop(0, nch, body, 0)
    o_ref[...] = hacc[...].astype(bf16)


def _gla(c_qk, c_v, c_la, nb, ctx_len):
    t = c_qk.shape[0]
    tb = t // nb
    consts = _gla_constants()
    const_spec = lambda a: pl.BlockSpec(a.shape, lambda b, n=a.ndim: (0,) * n, pipeline_mode=pl.Buffered(1))
    return pl.pallas_call(
        functools.partial(_gla_kernel, ctx_len=ctx_len),
        out_shape=SDS((t, MIX_W), bf16), grid=(nb,),
        in_specs=[pl.BlockSpec((tb, 512), lambda b: (b, 0)), pl.BlockSpec((tb, 512), lambda b: (b, 0)),
                  pl.BlockSpec((tb, c_la.shape[1]), lambda b: (b, 0), pipeline_mode=pl.Buffered(1))]
                 + [const_spec(a) for a in consts],
        out_specs=pl.BlockSpec((tb, MIX_W), lambda b: (b, 0)),
        scratch_shapes=[pltpu.VMEM((tb, MIX_W), f32), pltpu.VMEM((C_HEADS, 2 * C_DV, 2 * C_DK), f32)],
        compiler_params=_cp(("arbitrary",)), name="gla",
    )(c_qk, c_v, c_la, *consts)


def _rope_tables(seq, ctx_len):
    n_f = B_DH // 4
    t = np.arange(seq)
    freqs = ROPE_THETA ** (-np.arange(n_f, dtype=np.float32) / n_f)
    hd = np.arange(B_DH)
    pos = np.where(hd[None, :] < B_DH // 2, (t // GRID_W)[:, None], (t % GRID_W)[:, None]).astype(np.float32)
    ang = jnp.asarray(pos * freqs[hd % n_f][None, :], f32)
    sign = np.where((hd % (2 * n_f)) < n_f, -1.0, 1.0).astype(np.float32)
    cos = jnp.concatenate([jnp.ones((ctx_len, B_DH), f32), jnp.cos(ang)], axis=0)
    sin = jnp.concatenate([jnp.zeros((ctx_len, B_DH), f32), jnp.sin(ang) * sign[None, :]], axis=0)
    return jnp.tile(cos, (1, 2)), jnp.tile(sin, (1, 2))


def _attn_prep(bq, bkv, cos, sin, qg, kg, qo_ref, ko_ref, vo_ref):
    def norm_rope(x, g):
        w = x.shape[1]
        bd = (_iota((w, w), 0) // B_DH == _iota((w, w), 1) // B_DH).astype(bf16)
        ss = _dot((x * x).astype(bf16), bd)
        xn = x * lax.rsqrt(ss * (1.0 / B_DH) + EPS) * g
        first = (_iota(x.shape, 1) % (B_DH // 2)) < (B_DH // 4)
        swapped = jnp.where(first, pltpu.roll(xn, w - B_DH // 4, 1), pltpu.roll(xn, B_DH // 4, 1))
        reps = w // LANES
        return xn * jnp.tile(cos, (1, reps)) + swapped * jnp.tile(sin, (1, reps))

    q = (norm_rope(bq, qg) * (B_DH ** -0.5)).astype(bf16)
    for h in range(B_HEADS):
        qo_ref[h] = q[:, B_DH * h:B_DH * (h + 1)]
    k = norm_rope(bkv[:, :B_KV * B_DH], kg).astype(bf16)
    v = bkv[:, B_KV * B_DH:].astype(bf16)
    ones_col = (_iota((bkv.shape[0], LANES - B_DH), 1) == 0).astype(bf16)
    for h in range(B_KV):
        ko_ref[h] = k[:, B_DH * h:B_DH * (h + 1)]
        vo_ref[h] = jnp.concatenate([v[:, B_DH * h:B_DH * (h + 1)], ones_col], axis=1)


def _attn_kernel(q_ref, k_ref, v_ref, o_ref, *, ctx_len):
    tb = k_ref.shape[1]
    q = q_ref[...].reshape(B_GROUP * TQ, B_DH)

    def attend(klen):
        starts = [0] + list(range(ctx_len, klen, KEY_CHUNK))
        m = acc = None
        for s0, s1 in zip(starts, starts[1:] + [klen]):
            s = _dot_nt(q, k_ref[0, s0:s1, :])
            smax = jnp.max(s, axis=-1, keepdims=True)
            if m is None:
                m = smax
                acc = _dot(jnp.exp((s - m).astype(bf16)), v_ref[0, s0:s1, :])
            else:
                m_new = jnp.maximum(m, smax)
                acc = jnp.exp(m - m_new) * acc + _dot(jnp.exp((s - m_new).astype(bf16)), v_ref[0, s0:s1, :])
                m = m_new
        o = acc[:, :B_DH] / acc[:, B_DH:B_DH + 1]
        o_ref[...] = o.reshape(B_GROUP, TQ, B_DH).astype(bf16)

    is_ctx = pl.program_id(2) < ctx_len // TQ

    @pl.when(is_ctx)
    def _():
        attend(ctx_len)

    @pl.when(jnp.logical_not(is_ctx))
    def _():
        attend(tb)


def _attn(q, k, v, nb, ctx_len):
    t = q.shape[1]
    tb = t // nb
    nq = tb // TQ
    return pl.pallas_call(
        functools.partial(_attn_kernel, ctx_len=ctx_len),
        out_shape=SDS((B_HEADS, t, B_DH), bf16), grid=(nb, B_KV, nq),
        in_specs=[pl.BlockSpec((B_GROUP, TQ, B_DH), lambda b, g, i: (g, b * nq + i, 0)),
                  pl.BlockSpec((1, tb, B_DH), lambda b, g, i: (g, b, 0)),
                  pl.BlockSpec((1, tb, LANES), lambda b, g, i: (g, b, 0))],
        out_specs=pl.BlockSpec((B_GROUP, TQ, B_DH), lambda b, g, i: (g, b * nq + i, 0)),
        compiler_params=_cp(("arbitrary", "arbitrary", "arbitrary")), name="attn",
    )(q, k, v)


def _head_rms(y, g, dv):
    parts = []
    for h in range(y.shape[1] // dv):
        yh = y[:, dv * h:dv * (h + 1)]
        parts.append(yh * lax.rsqrt(jnp.mean(yh * yh, axis=-1, keepdims=True) + EPS))
    return jnp.concatenate(parts, axis=1) * g


def _merge_kernel(ha_ref, ao_ref, att_ref, hc_ref, cr_ref, gate_ref, x_ref, g1a_ref, g1b_ref, sh2a_ref, sh2b_ref,
                  sc2a_ref, sc2b_ref, ag_ref, cg_ref, wb_ref, wo_ref, gn2_ref, wr_ref, br_ref,
                  xo_ref, h2_ref, ti_ref, tw_ref, rk_ref, cnt_ref, cnt_s):
    i = pl.program_id(0)

    @pl.when(i == 0)
    def _():
        cnt_s[...] = jnp.zeros_like(cnt_s)

    d = x_ref.shape[1]
    ya = _head_rms(ha_ref[...].astype(f32), ag_ref[...], A_DV) * _sigmoid(ao_ref[...].astype(f32))
    cr = cr_ref[...].astype(f32)
    yc = _head_rms(hc_ref[...].astype(f32), cg_ref[...], C_DV) * (cr * _sigmoid(cr))
    yb = jnp.concatenate([att_ref[h] for h in range(B_HEADS)], axis=1)
    merged = jnp.zeros((TM, d), f32)
    for n, y in enumerate((ya.astype(bf16), yb, yc.astype(bf16))):
        merged = merged + _sigmoid(gate_ref[:, d * n:d * (n + 1)]) * _dot(y, wb_ref[n])
    x = x_ref[...] + _rows(g1a_ref, g1b_ref) * _dot(merged.astype(bf16), wo_ref[...])
    xo_ref[...] = x
    h2 = _rms_mod(x, gn2_ref[...], _rows(sh2a_ref, sh2b_ref), _rows(sc2a_ref, sc2b_ref))
    h2_ref[...] = _pack_bf16_pairs(h2)

    logits = _dot_nt(wr_ref[...], h2, HI) + br_ref[...]
    eid = _iota((N_EXPERTS, TM), 0)
    work = logits
    onehot = jnp.zeros((N_EXPERTS, TM), f32)
    vals, sels = [], []
    for k in range(TOP_K):
        mk = jnp.max(work, axis=0, keepdims=True)
        ik = jnp.min(jnp.where(work == mk, eid, N_EXPERTS), axis=0, keepdims=True)
        sel = eid == ik
        work = jnp.where(sel, -jnp.inf, work)
        onehot = onehot + sel.astype(f32)
        ti_ref[k:k + 1, :] = ik
        vals.append(mk)
        sels.append(sel)
    ex = [jnp.exp(vk - vals[0]) for vk in vals]
    tot = ex[0] + ex[1] + ex[2] + ex[3]
    for k in range(TOP_K):
        tw_ref[k:k + 1, :] = ex[k] / tot

    ut = (_iota((TM, TM), 0) <= _iota((TM, TM), 1)).astype(bf16)
    incl = _dot(onehot.astype(bf16), ut)
    rank = cnt_s[...][:, 0:1] + incl - onehot
    for k in range(TOP_K):
        rk_ref[k:k + 1, :] = jnp.sum(jnp.where(sels[k], rank, 0.0), axis=0, keepdims=True).astype(jnp.int32)
    cnt_s[...] = cnt_s[...] + incl[:, TM - 1:TM]
    cnt_ref[...] = cnt_s[...]


def _merge(p, h_a, att, h_c, x, mods, a_norm_g, c_norm_g, w_branch, w_out, g_norm2, w_router, b_router,
           tpb, ctx_tiles, nb):
    t, d = x.shape
    row = lambda w: pl.BlockSpec((TM, w), lambda i: (i, 0))
    const = lambda shape: pl.BlockSpec(shape, lambda i: tuple(0 for _ in shape))
    return pl.pallas_call(
        _merge_kernel,
        out_shape=(SDS((t, d), f32), SDS((t, d // 2), jnp.int32), SDS((TOP_K, t), jnp.int32), SDS((TOP_K, t), f32),
                   SDS((TOP_K, t), jnp.int32), SDS((N_EXPERTS, LANES), f32)),
        grid=(t // TM,),
        in_specs=[row(512), row(512), pl.BlockSpec((B_HEADS, TM, B_DH), lambda i: (0, i, 0)), row(512), row(512),
                  row(3 * d), row(d)]
                 + _mod_specs(2, tpb, ctx_tiles, nb) + _mod_specs(3, tpb, ctx_tiles, nb) + _mod_specs(4, tpb, ctx_tiles, nb)
                 + [const((1, 512)), const((1, 512)), const((3, MIX_W, d)), const((d, d)), const((1, d)),
                  const((N_EXPERTS, d)), const((N_EXPERTS, 1))],
        out_specs=(row(d), row(d // 2), pl.BlockSpec((TOP_K, TM), lambda i: (0, i)), pl.BlockSpec((TOP_K, TM), lambda i: (0, i)),
                   pl.BlockSpec((TOP_K, TM), lambda i: (0, i)), const((N_EXPERTS, LANES))),
        scratch_shapes=[pltpu.VMEM((N_EXPERTS, LANES), f32)],
        compiler_params=_cp(("arbitrary",)), name="merge",
    )(h_a, p["a_o"], att, h_c, p["c_r"], p["gate"], x, mods, mods, mods, mods, mods, mods,
      a_norm_g.reshape(1, -1), c_norm_g.reshape(1, -1), w_branch, w_out, g_norm2.reshape(1, -1),
      w_router.T, b_router.reshape(-1, 1))


def _sc_mesh():
    return plsc.VectorSubcoreMesh(core_axis_name="c", subcore_axis_name="s")


def _sc_scatter_rows(src, idx, n_out):
    v, d = src.shape
    reps = idx.shape[0] // v
    per_w = v // (SC_NUM_CORES * SC_NUM_SUBCORES)
    assert per_w % SC_WINDOW == 0 and idx.shape[0] == reps * v

    @functools.partial(pl.kernel, out_type=SDS((n_out, d), src.dtype), mesh=_sc_mesh(),
                       scratch_types=[pltpu.VMEM((SC_WINDOW,), jnp.int32), pltpu.VMEM((SC_WINDOW, d), src.dtype),
                                      pltpu.SemaphoreType.DMA])
    def k(x_hbm, i_hbm, o_hbm, idx_v, rows_v, sem):
        wid = lax.axis_index("s") * SC_NUM_CORES + lax.axis_index("c")

        @pl.loop(0, per_w // SC_WINDOW)
        def _(j):
            base = wid * per_w + j * SC_WINDOW
            pltpu.sync_copy(x_hbm.at[pl.ds(base, SC_WINDOW)], rows_v)
            for r in range(reps):
                pltpu.sync_copy(i_hbm.at[pl.ds(r * v + base, SC_WINDOW)], idx_v)
                pltpu.async_copy(rows_v, o_hbm.at[idx_v], sem).wait()

    return k(src, idx)


def _sc_gather_rows(table, idx):
    d = table.shape[1]
    n = idx.shape[0]
    per_w = n // (SC_NUM_CORES * SC_NUM_SUBCORES)
    assert per_w % SC_WINDOW == 0

    @functools.partial(pl.kernel, out_type=SDS((n, d), table.dtype), mesh=_sc_mesh(),
                       scratch_types=[pltpu.VMEM((SC_WINDOW,), jnp.int32), pltpu.VMEM((SC_WINDOW, d), table.dtype),
                                      pltpu.SemaphoreType.DMA])
    def k(x_hbm, i_hbm, o_hbm, idx_v, rows_v, sem):
        wid = lax.axis_index("s") * SC_NUM_CORES + lax.axis_index("c")

        @pl.loop(0, per_w // SC_WINDOW)
        def _(j):
            base = wid * per_w + j * SC_WINDOW
            pltpu.sync_copy(i_hbm.at[pl.ds(base, SC_WINDOW)], idx_v)
            pltpu.async_copy(x_hbm.at[idx_v], rows_v, sem).wait()
            pltpu.sync_copy(rows_v, o_hbm.at[pl.ds(base, SC_WINDOW)])

    return k(table, idx)


GU_BLOCK = 2 * LANES


def _deinterleave_perm():
    n = np.arange(GU_BLOCK)
    src = np.where(n < LANES, 2 * n, 2 * (n - LANES) + 1)
    return jnp.asarray(np.arange(GU_BLOCK)[:, None] == src[None, :], bf16)


def _expert_kernel(te_ref, nv_ref, x_ref, w1_ref, b1_ref, w2_ref, b2_ref, perm_ref, y_ref, w1_s, w2_s):
    i = pl.program_id(0)
    valid = i < nv_ref[0]
    new_expert = jnp.logical_or(i == 0, te_ref[i] != te_ref[jnp.maximum(i - 1, 0)])
    dff2 = w1_ref.shape[2]

    @pl.when(jnp.logical_and(valid, new_expert))
    def _():
        for cb in range(dff2 // GU_BLOCK):
            cs = slice(GU_BLOCK * cb, GU_BLOCK * (cb + 1))
            w1_s[:, cs] = _dot(w1_ref[0, :, cs].astype(bf16), perm_ref[...]).astype(bf16)
        w2_s[...] = w2_ref[0].astype(bf16)

    @pl.when(valid)
    def _():
        x = jnp.concatenate(_unpack_bf16_pairs(x_ref[...]), axis=1).astype(bf16)
        gu = _dot(x, w1_s[...]) + b1_ref[0]
        nblk = dff2 // GU_BLOCK
        g = jnp.concatenate([gu[:, GU_BLOCK * cb:GU_BLOCK * cb + LANES] for cb in range(nblk)], axis=1)
        u = jnp.concatenate([gu[:, GU_BLOCK * cb + LANES:GU_BLOCK * (cb + 1)] for cb in range(nblk)], axis=1)
        gate = jnp.minimum(g, SWIGLU_LIMIT)
        up = jnp.clip(u, -SWIGLU_LIMIT, SWIGLU_LIMIT)
        a = (up + 1.0) * gate * _sigmoid(SWIGLU_ALPHA * gate)
        y_ref[...] = _pack_bf16_pairs(_dot(a.astype(bf16), w2_s[...]) + b2_ref[0])


def _experts(xs, tile_e, n_valid, layer, w1, b1, w2, b2):
    p, dw = xs.shape
    d = 2 * dw
    dff2 = w1.shape[-1]
    dff = w2.shape[2]
    row = lambda i, te, nv: (jnp.minimum(i, nv[0] - 1), 0)
    wsel = lambda i, te, nv: (layer, te[i], 0, 0)
    sq = pl.Squeezed()
    return pl.pallas_call(
        _expert_kernel, out_shape=SDS((p, dw), jnp.int32),
        grid_spec=pltpu.PrefetchScalarGridSpec(
            num_scalar_prefetch=2, grid=(p // TME,),
            in_specs=[pl.BlockSpec((TME, dw), row),
                      pl.BlockSpec((sq, 1, d, dff2), wsel), pl.BlockSpec((sq, 1, 1, dff2), wsel),
                      pl.BlockSpec((sq, 1, dff, d), wsel), pl.BlockSpec((sq, 1, 1, d), wsel),
                      pl.BlockSpec((GU_BLOCK, GU_BLOCK), lambda i, te, nv: (0, 0))],
            out_specs=pl.BlockSpec((TME, dw), row),
            scratch_shapes=[pltpu.VMEM((d, dff2), bf16), pltpu.VMEM((dff, d), bf16)]),
        compiler_params=_cp(("arbitrary",)), name="experts",
    )(tile_e, n_valid, xs, w1, b1, w2, b2, _deinterleave_perm())


def _final_kernel(x_ref, yg_ref, w_ref, g2_ref, gf_ref, o_ref):
    x = x_ref[...] + g2_ref[0] * _weighted_expert_sum(yg_ref, w_ref[...])
    o_ref[...] = x * lax.rsqrt(jnp.mean(x * x, axis=-1, keepdims=True) + EPS) * gf_ref[...]


def _final(x, yg, wcol, mods, tpb, ctx_tiles, nb, g_final):
    d = x.shape[1]
    lat = tpb - ctx_tiles
    rmap = lambda i: ((i // lat) * tpb + ctx_tiles + i % lat)
    return pl.pallas_call(
        _final_kernel, out_shape=SDS((nb * lat * TMH, d), f32), grid=(nb * lat,),
        in_specs=[pl.BlockSpec((TMH, d), lambda i: (rmap(i), 0)),
                  pl.BlockSpec((TOP_K, TMH, d // 2), lambda i: (0, rmap(i), 0)),
                  pl.BlockSpec((TMH, TOP_K), lambda i: (rmap(i), 0)),
                  pl.BlockSpec((1, 1, 1024), lambda i: (i // lat, 0, 5)),
                  pl.BlockSpec((1, d), lambda i: (0, 0))],
        out_specs=pl.BlockSpec((TMH, d), lambda i: (i, 0)),
        compiler_params=_cp(("arbitrary",)), name="final",
    )(x, yg, wcol, mods, g_final.reshape(1, -1))


def _routing_tables(top_i, rank, counts, n_tiles):
    cnt = counts[:, 0].astype(jnp.int32)
    padded = ((cnt + TME - 1) // TME) * TME
    ends = jnp.cumsum(padded)
    starts = ends - padded
    eids = jnp.arange(N_EXPERTS, dtype=jnp.int32)
    start_of = jnp.sum(jnp.where(top_i[..., None] == eids, starts, 0), axis=-1)
    pos = (start_of + rank).reshape(-1)
    n_valid = ends[-1] // TME
    tile_start = jnp.arange(n_tiles, dtype=jnp.int32) * TME
    tile_e = jnp.sum(tile_start[:, None] >= ends[None, :], axis=1).astype(jnp.int32)
    tile_e = jnp.where(jnp.arange(n_tiles) < n_valid, tile_e, tile_e[jnp.maximum(n_valid - 1, 0)])
    tile_e = jnp.minimum(tile_e, N_EXPERTS - 1)
    return pos, tile_e, n_valid.reshape(1).astype(jnp.int32)


def kernel(x, c, ctx, c_ctx, w_ada, b_ada, g_norm1, w_in, a_conv_w, a_conv_b, a_gate_b, a_norm_g, b_q_norm_g, b_k_norm_g, c_w_up, c_b_up, c_norm_g, w_branch, w_out, g_norm2, w_router, b_router, w_e1, b_e1, w_e2, b_e2, g_final):
    nb, seq, d = x.shape
    ctx_len = ctx.shape[1]
    depth = w_ada.shape[0]
    tb = ctx_len + seq
    t = nb * tb
    tpb, ctx_tiles = tb // TMH, ctx_len // TMH
    assert d == 1024 and nb < 16 and seq % TMH == 0 and ctx_len % TMH == 0 and ctx_len % TQ == 0 and t % TM == 0
    n_assign = TOP_K * t
    n_sorted = n_assign + N_EXPERTS * TME
    n_tiles = n_sorted // TME

    xs = jnp.concatenate([ctx, x], axis=1).reshape(t, d)
    cc = jnp.zeros((16, d), f32).at[:nb].set(c).at[nb].set(c_ctx)
    mods_all = _adaln(cc, w_ada, b_ada)
    cos, sin = _rope_tables(seq, ctx_len)
    w_in_r = _arrange_w_in(w_in)
    w_branch_b, w_out_b = w_branch.astype(bf16), w_out.astype(bf16)
    col = np.arange(b_e1.shape[-1])
    within = col % GU_BLOCK
    src = (col // GU_BLOCK) * GU_BLOCK + np.where(within < LANES, 2 * within, 2 * (within - LANES) + 1)
    b1 = b_e1[..., src][..., None, :]
    b2 = b_e2[..., None, :]

    out = None
    prev = None
    for l in range(depth):
        mods = mods_all[l].reshape(16, 1, 6 * d)
        xs, p, (qn, kn, vn) = _inproj(xs, mods, g_norm1[l].reshape(1, -1), w_in_r[l], cos, sin,
                                      b_q_norm_g[l], b_k_norm_g[l], a_gate_b[l], c_w_up[l], c_b_up[l],
                                      tpb, ctx_tiles, nb, prev=prev)
        h_a = _mlstm(p["a_qk"], p["a_v"], p["small"], a_conv_w[l], a_conv_b[l], nb, ctx_len)
        h_c = _gla(p["c_qk"], p["c_v"], p["c_la"], nb, ctx_len)
        att = _attn(qn, kn, vn, nb, ctx_len)
        xs, h2, top_i, top_w, rank, counts = _merge(
            p, h_a, att, h_c, xs, mods, a_norm_g[l], c_norm_g[l], w_branch_b[l], w_out_b[l], g_norm2[l],
            w_router[l], b_router[l], tpb, ctx_tiles, nb)
        pos, tile_e, n_valid = _routing_tables(top_i, rank, counts, n_tiles)
        x_sorted = _sc_scatter_rows(h2, pos, n_sorted)
        y_sorted = _experts(x_sorted, tile_e, n_valid, l, w_e1, b1, w_e2, b2)
        yg = _sc_gather_rows(y_sorted, pos).reshape(TOP_K, t, d // 2)
        if l == depth - 1:
            out = _final(xs, yg, top_w.T, mods, tpb, ctx_tiles, nb, g_final)
        else:
            prev = (yg, top_w.T, mods)
    return out.reshape(nb, seq, d)
```

```python
import functools

import jax
import jax.numpy as jnp
import numpy as np
from jax import lax
from jax.experimental import pallas as pl
from jax.experimental.pallas import tpu as pltpu
from jax.experimental.pallas import tpu_sc as plsc

f32 = jnp.float32
bf16 = jnp.bfloat16
HI = lax.Precision.HIGHEST
SDS = jax.ShapeDtypeStruct

EPS = 1e-6
CHUNK = 256
GLA_CHUNK = 128
GRID_W = 64
ROPE_THETA = 10000.0
MIX_W = 512
A_HEADS, A_DQK, A_DV = 4, 64, 128
B_HEADS, B_KV, B_DH = 8, 2, 64
B_GROUP = B_HEADS // B_KV
C_HEADS, C_DK, C_DV, C_RANK, C_TAU = 4, 64, 128, 16, 16.0
N_EXPERTS, TOP_K = 32, 4
SWIGLU_LIMIT, SWIGLU_ALPHA = 7.0, 1.702

LANES = 128
SC_NUM_CORES = 2
SC_NUM_SUBCORES = 16
VMEM_LIMIT = 56 * 2 ** 20

TMH = 256
TM = 2 * TMH
TQ = 256
TME = 1024
KEY_CHUNK = 2048
SC_WINDOW = 64


def _cp(sem, vmem=VMEM_LIMIT):
    return pltpu.CompilerParams(dimension_semantics=sem, vmem_limit_bytes=vmem)


def _dot(a, b, precision=None):
    return jnp.dot(a, b, preferred_element_type=f32, precision=precision)


def _dot_nt(a, b, precision=None):
    return lax.dot_general(a, b, (((1,), (1,)), ((), ())), preferred_element_type=f32, precision=precision)


def _dot_tn(a, b):
    return lax.dot_general(a, b, (((0,), (0,)), ((), ())), preferred_element_type=f32)


def _sigmoid(x):
    return 0.5 * jnp.tanh(0.5 * x) + 0.5


def _log_sigmoid(x):
    return jnp.minimum(x, 0.0) - jnp.log(1.0 + jnp.exp(-jnp.abs(x)))


def _iota(shape, axis):
    return lax.broadcasted_iota(jnp.int32, shape, axis)


def _pack_bf16_pairs(x):
    n = x.shape[1] // 2
    lo = pltpu.bitcast(x[:, :n].astype(bf16).astype(f32), jnp.uint32)
    hi = pltpu.bitcast(x[:, n:].astype(bf16).astype(f32), jnp.uint32)
    return pltpu.bitcast((lo >> 16) | hi, jnp.int32)


def _unpack_bf16_pairs(w):
    u = pltpu.bitcast(w, jnp.uint32)
    return pltpu.bitcast(u << 16, f32), pltpu.bitcast(u & jnp.uint32(0xFFFF0000), f32)


def _rms_mod(x, g, shift, scale):
    y = x * lax.rsqrt(jnp.mean(x * x, axis=-1, keepdims=True) + EPS) * g
    return y * (1.0 + scale) + shift


def _adaln_kernel(cc_ref, w_ref, b_ref, o_ref):
    cc = cc_ref[...]
    s = cc * _sigmoid(cc)
    o_ref[0] = _dot(s, w_ref[0], HI) + b_ref[0]


def _adaln(cc, w_ada, b_ada):
    nl, d, n = w_ada.shape
    return pl.pallas_call(
        _adaln_kernel, out_shape=SDS((nl, 16, n), f32), grid=(nl, n // 1024),
        in_specs=[pl.BlockSpec((16, d), lambda l, j: (0, 0)),
                  pl.BlockSpec((1, d, 1024), lambda l, j: (l, 0, j)),
                  pl.BlockSpec((1, 1, 1024), lambda l, j: (l, 0, j))],
        out_specs=pl.BlockSpec((1, 16, 1024), lambda l, j: (l, 0, j)),
        compiler_params=_cp(("arbitrary", "arbitrary")), name="adaln",
    )(cc, w_ada, b_ada.reshape(nl, 1, n))


def _mod_specs(col, tpb, ctx_tiles, nb):
    def spec(half):
        def imap(i):
            u = 2 * i + half
            return (jnp.where((u % tpb) >= ctx_tiles, u // tpb, nb), 0, col)
        return pl.BlockSpec((1, 1, 1024), imap)
    return [spec(0), spec(1)]


def _rows(a_ref, b_ref):
    a, b = a_ref[0], b_ref[0]
    return jnp.concatenate([jnp.broadcast_to(a, (TMH, a.shape[1])), jnp.broadcast_to(b, (TMH, b.shape[1]))], axis=0)


_K1_GROUPS = (("a_qk", 512, bf16), ("a_v", 512, bf16), ("a_o", 512, bf16), ("b_q", 512, bf16),
              ("b_kv", 256, bf16), ("c_qk", 512, bf16), ("c_v", 512, bf16), ("c_r", 512, bf16),
              ("gate", 3072, bf16), ("small", 128, f32))
_K1_WIDTH = sum(w for _, w, _ in _K1_GROUPS)


def _arrange_w_in(w_in):
    z = jnp.zeros(w_in.shape[:-1] + (LANES - 48,), w_in.dtype)
    w = jnp.concatenate([w_in[..., 0:1024], w_in[..., 1040:1552], w_in[..., 1552:2320], w_in[..., 2320:3344],
                         w_in[..., 3376:3888], w_in[..., 3888:6960], w_in[..., 1024:1040], w_in[..., 3344:3376], z],
                        axis=-1)
    assert w.shape[-1] == _K1_WIDTH
    return w.astype(bf16)


_ATTN_GROUPS = ("b_q", "b_kv")
_STORED_GROUPS = tuple(g for g in _K1_GROUPS if g[0] not in _ATTN_GROUPS)


def _weighted_expert_sum(yg_ref, w):
    acc_lo = acc_hi = None
    for k in range(TOP_K):
        lo, hi = _unpack_bf16_pairs(yg_ref[k])
        wk = w[:, k:k + 1]
        acc_lo = wk * lo if acc_lo is None else acc_lo + wk * lo
        acc_hi = wk * hi if acc_hi is None else acc_hi + wk * hi
    return jnp.concatenate([acc_lo, acc_hi], axis=1)


def _inproj_kernel(*refs, fuse_prev):
    refs = list(refs)
    x_ref = refs.pop(0)
    x = x_ref[...]
    if fuse_prev:
        yg_ref, wcol_ref, g2a_ref, g2b_ref = refs[:4]
        refs = refs[4:]
        x = x + _rows(g2a_ref, g2b_ref) * _weighted_expert_sum(yg_ref, wcol_ref[...])
    (sha_ref, shb_ref, sca_ref, scb_ref, g_ref, w_ref, cosa_ref, cosb_ref, sina_ref, sinb_ref, qg_ref, kg_ref,
     gb_ref, wup_ref, bup_ref) = refs[:15]
    outs = refs[15:]
    if fuse_prev:
        outs.pop(0)[...] = x
    h = _rms_mod(x, g_ref[...], _rows(sha_ref, shb_ref), _rows(sca_ref, scb_ref)).astype(bf16)
    c0 = 0
    attn_in = {}
    n_stored = len(_STORED_GROUPS) + 1
    stored = iter(outs[:n_stored])
    for name, width, dt in _K1_GROUPS:
        if name in _ATTN_GROUPS:
            attn_in[name] = _dot(h, w_ref[:, c0:c0 + width])
        elif name == "small":
            sm = _dot(h, w_ref[:, c0:c0 + width]) + gb_ref[...]
            lane = _iota(sm.shape, 1)
            is_f = jnp.logical_and(lane >= 2 * A_HEADS, lane < 4 * A_HEADS)
            next(stored)[...] = jnp.where(is_f, _log_sigmoid(sm), sm)
            la_ref = next(stored)
            sm_hi = sm.astype(bf16)
            sm_lo = (sm - sm_hi.astype(f32)).astype(bf16)
            sm3 = jnp.concatenate([sm_hi, sm_lo, sm_hi], axis=1)
            for dr in range(2):
                nk = C_HEADS * C_DK
                la_ref[:, nk * dr:nk * (dr + 1)] = _log_sigmoid(_dot(sm3, wup_ref[dr]) + bup_ref[dr]) * (1.0 / C_TAU)
        else:
            o_ref = next(stored)
            for j in range(0, width, 512):
                wj = min(512, width - j)
                o_ref[:, j:j + wj] = _dot(h, w_ref[:, c0 + j:c0 + j + wj]).astype(dt)
        c0 += width
    qo_ref, ko_ref, vo_ref = outs[n_stored:]
    cos = jnp.concatenate([cosa_ref[...], cosb_ref[...]], axis=0)
    sin = jnp.concatenate([sina_ref[...], sinb_ref[...]], axis=0)
    _attn_prep(attn_in["b_q"], attn_in["b_kv"], cos, sin, qg_ref[...], kg_ref[...], qo_ref, ko_ref, vo_ref)


def _inproj(x, mods, g, w, cos, sin, qg, kg, gate_b, w_up, b_up, tpb, ctx_tiles, nb, prev=None):
    t, d = x.shape
    row = lambda wd: pl.BlockSpec((TM, wd), lambda i: (i, 0))
    const = lambda shape: pl.BlockSpec(shape, lambda i: (0,) * len(shape))
    in_specs, args = [row(d)], [x]
    out_shape, out_specs = [], []
    nk = C_HEADS * C_DK
    gb = jnp.zeros((1, LANES), f32).at[0, :4 * A_HEADS].set(gate_b.reshape(-1))
    wz = jnp.zeros((2, LANES, nk), f32)
    wz = wz.at[0, 16:16 + C_RANK].set(w_up[0]).at[1, 16 + C_RANK:16 + 2 * C_RANK].set(w_up[1])
    wz_hi = wz.astype(bf16)
    wz_lo = (wz - wz_hi.astype(f32)).astype(bf16)
    wz = jnp.concatenate([wz_hi, wz_hi, wz_lo], axis=1)
    if prev is not None:
        yg, wcol, mods_prev = prev
        in_specs += [pl.BlockSpec((TOP_K, TM, d // 2), lambda i: (0, i, 0)), row(TOP_K)]
        in_specs += _mod_specs(5, tpb, ctx_tiles, nb)
        args += [yg, wcol, mods_prev, mods_prev]
        out_shape.append(SDS((t, d), f32))
        out_specs.append(row(d))
    table = lambda half: pl.BlockSpec((TMH, LANES), lambda i: ((2 * i + half) % tpb, 0))
    in_specs += _mod_specs(0, tpb, ctx_tiles, nb) + _mod_specs(1, tpb, ctx_tiles, nb)
    in_specs += [const((1, d)), pl.BlockSpec((d, _K1_WIDTH), lambda i: (0, 0), pipeline_mode=pl.Buffered(1)),
                 table(0), table(1), table(0), table(1), const((1, B_HEADS * B_DH)), const((1, B_KV * B_DH)),
                 const((1, LANES)), const((2, 3 * LANES, nk)), const((2, 1, nk))]
    args += [mods, mods, mods, mods, g, w, cos, cos, sin, sin,
             jnp.tile(qg, B_HEADS).reshape(1, -1), jnp.tile(kg, B_KV).reshape(1, -1), gb, wz, b_up.reshape(2, 1, nk)]
    out_shape += [SDS((t, wd), dt) for _, wd, dt in _STORED_GROUPS] + [SDS((t, 2 * nk), f32)]
    out_specs += [row(wd) for _, wd, _ in _STORED_GROUPS] + [row(2 * nk)]
    out_shape += [SDS((B_HEADS, t, B_DH), bf16), SDS((B_KV, t, B_DH), bf16), SDS((B_KV, t, LANES), bf16)]
    out_specs += [pl.BlockSpec((B_HEADS, TM, B_DH), lambda i: (0, i, 0)),
                  pl.BlockSpec((B_KV, TM, B_DH), lambda i: (0, i, 0)),
                  pl.BlockSpec((B_KV, TM, LANES), lambda i: (0, i, 0))]
    res = list(pl.pallas_call(
        functools.partial(_inproj_kernel, fuse_prev=prev is not None),
        out_shape=tuple(out_shape), grid=(t // TM,), in_specs=in_specs, out_specs=tuple(out_specs),
        compiler_params=_cp(("arbitrary",)), name="inproj",
    )(*args))
    x_new = res.pop(0) if prev is not None else x
    p = dict(zip([n for n, _, _ in _STORED_GROUPS] + ["c_la"], res))
    return x_new, p, tuple(res[len(_STORED_GROUPS) + 1:])


def _bwd_chunk(i, ncc, nch):
    return jnp.where(i < ncc, ncc - 1 - i, nch - 1 + ncc - i)


def _split2(x, axis=0):
    hi = x.astype(bf16)
    return jnp.concatenate([hi, (x - hi.astype(f32)).astype(bf16)], axis=axis)


def _mlstm_kernel(qk_ref, v_ref, sm_ref, cw_ref, cb_ref, tri_ref, lmask_ref, exps_ref, expd_ref, expv_ref,
                  oseg_ref, kmask_ref, vmask_ref, cmask_ref, nmask_ref, o_ref, qk_s, hacc, c_s, n_s, *, ctx_len):
    tb = qk_ref.shape[0]
    nch = tb // CHUNK
    ncc = ctx_len // CHUNK
    nu = 2 * A_HEADS

    cw = cw_ref[...]
    cb = cb_ref[...]
    row = _iota((CHUNK, 2 * A_HEADS * A_DQK), 0)
    kscale = jnp.where(_iota((1, 2 * A_HEADS * A_DQK), 1) >= A_HEADS * A_DQK, A_DQK ** -0.5, 1.0)

    def conv_body(c, carry):
        r0 = pl.multiple_of(c * CHUNK, CHUNK)
        x = qk_ref[pl.ds(r0, CHUNK), :].astype(f32)
        rp = pl.multiple_of(jnp.maximum(r0 - 16, 0), 16)
        rn = pl.multiple_of(jnp.minimum(r0 + CHUNK, tb - 16), 16)
        prev = qk_ref[pl.ds(rp, 16), :].astype(f32)[15:16, :]
        nxt = qk_ref[pl.ds(rn, 16), :].astype(f32)[0:1, :]
        has_prev = jnp.logical_and(r0 != 0, r0 != ctx_len)
        has_next = jnp.logical_and(r0 + CHUNK != ctx_len, r0 + CHUNK != tb)
        prev = jnp.where(has_prev, prev, 0.0)
        nxt = jnp.where(has_next, nxt, 0.0)
        xm = jnp.where(row == 0, prev, pltpu.roll(x, 1, 0))
        xp = jnp.where(row == CHUNK - 1, nxt, pltpu.roll(x, CHUNK - 1, 0))
        y = cw[0:1] * xm + cw[1:2] * x + cw[2:3] * xp + cb
        y = y * _sigmoid(y) * kscale
        qk_s[pl.ds(r0, CHUNK), :] = y.astype(bf16)
        return carry

    lax.fori_loop(0, nch, conv_body, 0)

    hacc[...] = jnp.zeros_like(hacc)
    c_s[...] = jnp.zeros_like(c_s)
    n_s[...] = jnp.zeros_like(n_s)

    lane = _iota((CHUNK, LANES), 1)
    row = _iota((CHUNK, LANES), 0)
    neg_inf = jnp.float32(-jnp.inf)
    nk = A_HEADS * A_DQK
    ones_rows = jnp.ones((CHUNK, LANES), bf16)

    def cummax(x, d):
        for sh in [1 << e for e in range(CHUNK.bit_length() - 1)]:
            if d == 0:
                y = jnp.where(row >= sh, pltpu.roll(x, sh, 0), neg_inf)
            else:
                y = jnp.where(row < CHUNK - sh, pltpu.roll(x, CHUNK - sh, 0), neg_inf)
            x = jnp.maximum(x, y)
        return x

    def body(i, ms):
        ms = list(ms)
        pending = []
        for d in range(2):
            c = i if d == 0 else _bwd_chunk(i, ncc, nch)
            r0 = pl.multiple_of(c * CHUNK, CHUNK)
            qk = qk_s[pl.ds(r0, CHUNK), :]
            q, k = qk[:, :nk], qk[:, nk:]
            v = v_ref[pl.ds(r0, CHUNK), :]
            mine = jnp.logical_and(lane >= d * A_HEADS, lane < (d + 1) * A_HEADS)
            keep = lambda x: jnp.where(mine[:x.shape[0]], x, 0.0)
            pre = sm_ref[pl.ds(r0, CHUNK), :]
            b = pltpu.roll(_dot(tri_ref[d], _split2(pre)), LANES - nu, 1)
            g = keep(pre - b)
            cm = cummax(g, d)
            g_t = g.T
            g_row = jnp.concatenate([g_t[d * A_HEADS + h:d * A_HEADS + h + 1, :] for h in range(A_HEADS)], axis=1)
            pm = jnp.exp(jnp.where(lmask_ref[d] > 0, g_row - _dot(_split2(cm, 1), exps_ref[d]), neg_inf))
            kst = jnp.tile(k, (A_HEADS, 1)) * kmask_ref[...]
            s = (_dot_nt(q, kst) * pm).astype(bf16)
            p1 = _dot(s, jnp.tile(v, (A_HEADS, 1)) * vmask_ref[...])
            sden = _dot(s, oseg_ref[d])
            cbd, nbd = c_s[d], n_s[d]
            qc = _dot(q, cbd.astype(bf16))
            qn = _dot(q, nbd.astype(bf16))
            m = ms[d]
            rmax, li = b + cm, b + m
            mj = jnp.maximum(li, rmax)
            alpha, beta = jnp.exp(rmax - mj), jnp.exp(li - mj)
            inv = 1.0 / jnp.maximum(jnp.abs(alpha * sden + beta * qn), jnp.exp(-mj))
            last = CHUNK - 1 if d == 0 else 0
            b_last, m_new, mloc = b[last:last + 1], mj[last:last + 1], rmax[last:last + 1]
            dec, gain = keep(jnp.exp(b_last + m - m_new)), keep(jnp.exp(mloc - m_new))
            sc = jnp.concatenate([keep(alpha * inv), keep(beta * inv), dec, gain, jnp.zeros((6, LANES), f32)], axis=0)
            scx = _dot(_split2(sc, 1), expv_ref[d])
            hout = scx[:CHUNK] * p1 + scx[CHUNK:2 * CHUNK] * qc
            wl = keep(jnp.exp(b_last - b + pre - mloc))
            kw = (k.astype(f32) * _dot(_split2(wl, 1), expd_ref[d])).astype(bf16)
            c_new = cbd * scx[2 * CHUNK:2 * CHUNK + 1] + (_dot_tn(kw, v) * cmask_ref[...]) * scx[2 * CHUNK + 1:2 * CHUNK + 2]
            n_new = nbd * dec + (_dot_tn(kw, ones_rows) * nmask_ref[d]) * gain
            ms[d] = keep(m_new)
            pending.append((r0, hout, c_new, n_new))
        for d, (r0, hout, c_new, n_new) in enumerate(pending):
            hacc[pl.ds(r0, CHUNK), :] += hout
            c_s[d] = c_new
            n_s[d] = n_new
        return tuple(ms)

    lax.fori_loop(0, nch, body, tuple(jnp.zeros((1, LANES), f32) for _ in range(2)))
    o_ref[...] = hacc[...].astype(bf16)


def _mlstm_constants():
    t = np.arange(CHUNK)
    lane = np.arange(LANES)
    hs = np.arange(A_HEADS * CHUNK)
    hd = np.arange(A_HEADS * A_DQK)
    hv = np.arange(A_HEADS * A_DV)
    tri3, lmask, exps, expd, expv, oseg, nmask = [], [], [], [], [], [], []
    for d in range(2):
        pos = t if d == 0 else CHUNK - 1 - t
        tri = pos[None, :] <= pos[:, None]
        tri3.append(np.tile(tri, (1, 2)))
        lmask.append(np.tile(tri, (1, A_HEADS)))
        unit = lambda idx, width: lane[:, None] == d * A_HEADS + idx[None, :] // width
        exps.append(np.tile(unit(hs, CHUNK), (2, 1)))
        expd.append(np.tile(unit(hd, A_DQK), (2, 1)))
        expv.append(np.tile(unit(hv, A_DV), (2, 1)))
        oseg.append(unit(hs, CHUNK).T)
        nmask.append(unit(hd, A_DQK).T)
    kmask = hs[:, None] // CHUNK == hd[None, :] // A_DQK
    vmask = hs[:, None] // CHUNK == hv[None, :] // A_DV
    cmask = hd[:, None] // A_DQK == hv[None, :] // A_DV
    as_b = lambda x: jnp.asarray(np.stack(x) if isinstance(x, list) else x, bf16)
    return (as_b(tri3), as_b(lmask), as_b(exps), as_b(expd), as_b(expv), as_b(oseg), as_b(kmask), as_b(vmask),
            jnp.asarray(cmask, f32), jnp.asarray(np.stack(nmask), f32))


def _mlstm(a_qk, a_v, small, conv_w, conv_b, nb, ctx_len):
    t = a_qk.shape[0]
    tb = t // nb
    consts = _mlstm_constants()
    const_spec = lambda a: pl.BlockSpec(a.shape, lambda b, n=a.ndim: (0,) * n, pipeline_mode=pl.Buffered(1))
    return pl.pallas_call(
        functools.partial(_mlstm_kernel, ctx_len=ctx_len),
        out_shape=SDS((t, MIX_W), bf16), grid=(nb,),
        in_specs=[pl.BlockSpec((tb, 512), lambda b: (b, 0)), pl.BlockSpec((tb, 512), lambda b: (b, 0)),
                  pl.BlockSpec((tb, LANES), lambda b: (b, 0)),
                  pl.BlockSpec((3, 512), lambda b: (0, 0)), pl.BlockSpec((1, 512), lambda b: (0, 0))]
                 + [const_spec(a) for a in consts],
        out_specs=pl.BlockSpec((tb, MIX_W), lambda b: (b, 0)),
        scratch_shapes=[pltpu.VMEM((tb, 512), bf16), pltpu.VMEM((tb, MIX_W), f32),
                        pltpu.VMEM((2, A_HEADS * A_DQK, A_HEADS * A_DV), f32),
                        pltpu.VMEM((2, A_HEADS * A_DQK, LANES), f32)],
        compiler_params=_cp(("arbitrary",)), name="mlstm",
    )(a_qk, a_v, small, conv_w, conv_b.reshape(1, -1), *consts)


GLA_LEVELS = GLA_CHUNK.bit_length() - 1


def _gla_constants():
    CHUNK = GLA_CHUNK
    t = np.arange(CHUNK)
    tri, gref, lmask = [], [], []
    for d in range(2):
        pos = t if d == 0 else CHUNK - 1 - t
        row_of = np.argsort(pos)
        tri.append(pos[None, :] <= pos[:, None])
        g, m = [], []
        for lvl in range(GLA_LEVELS):
            half = CHUNK >> (lvl + 1)
            ref_pos = (pos // (2 * half)) * (2 * half) + half
            g.append(t[None, :] == row_of[ref_pos][:, None])
            late, early = (pos % (2 * half)) >= half, (pos % (2 * half)) < half
            same = (pos[:, None] // (2 * half)) == (pos[None, :] // (2 * half))
            m.append(same & late[:, None] & early[None, :])
        m.append(t[:, None] == t[None, :])
        gref.append(np.concatenate(g, axis=0))
        lmask.append(np.stack([np.tile(x, (1, C_HEADS)) for x in m]))
    tri3 = np.stack([np.tile(x, (1, 2)) for x in tri])
    gref3 = np.stack([np.tile(x, (1, 2)) for x in gref])
    hs = np.arange(C_HEADS * CHUNK)
    kmask = (hs[:, None] // CHUNK) == (np.arange(C_HEADS * C_DK)[None, :] // C_DK)
    vmask = (hs[:, None] // CHUNK) == (np.arange(C_HEADS * C_DV)[None, :] // C_DV)
    bdm = (np.arange(2 * C_DV)[:, None] // C_DV) == (np.arange(2 * C_DK)[None, :] // C_DK)
    as_b = lambda x: jnp.asarray(x, bf16)
    return as_b(tri3), as_b(gref3), jnp.asarray(np.stack(lmask), f32), as_b(kmask), as_b(vmask), jnp.asarray(bdm, f32)


def _gla_kernel(qk_ref, v_ref, la_ref, tri_ref, gref_ref, lmask_ref, kmask_ref, vmask_ref, bdm_ref,
                o_ref, hacc, s_s, *, ctx_len):
    CHUNK = GLA_CHUNK
    tb = qk_ref.shape[0]
    nch = tb // CHUNK
    ncc = ctx_len // CHUNK
    nk = C_HEADS * C_DK
    pw = 2 * C_DK
    pv = 2 * C_DV
    npair = C_HEADS // 2

    hacc[...] = jnp.zeros_like(hacc)
    s_s[...] = jnp.zeros_like(s_s)
    qscale = C_DK ** -0.5

    def body(i, carry):
        for d in range(2):
            c = i if d == 0 else _bwd_chunk(i, ncc, nch)
            r0 = pl.multiple_of(c * CHUNK, CHUNK)
            la = la_ref[pl.ds(r0, CHUNK), nk * d:nk * (d + 1)]
            qk = qk_ref[pl.ds(r0, CHUNK), :]
            q = qk[:, :nk].astype(f32) * qscale
            k = qk[:, nk:].astype(f32)
            v = v_ref[pl.ds(r0, CHUNK), :]
            b = _dot(tri_ref[d], _split2(la))
            bref = _dot(gref_ref[d], _split2(b))

            def scores(qe, ke):
                kst = jnp.tile(ke, (C_HEADS, 1)) * kmask_ref[...]
                return _dot_nt(qe, kst)

            qb, kb = q.astype(bf16), k.astype(bf16)
            a = scores(qb, kb) * lmask_ref[d, GLA_LEVELS]
            for lvl in range(GLA_LEVELS):
                rel = b - bref[CHUNK * lvl:CHUNK * (lvl + 1)]
                s = scores(qb * jnp.exp(jnp.minimum(rel, 0.0)).astype(bf16),
                           kb * jnp.exp(jnp.minimum(-rel, 0.0)).astype(bf16))
                a = a + s * lmask_ref[d, lvl]
            vst = jnp.tile(v, (C_HEADS, 1)) * vmask_ref[...]
            o = _dot(a.astype(bf16), vst)

            last = CHUNK - 1 if d == 0 else 0
            bl = b[last:last + 1]
            qd = (q * jnp.exp(b)).astype(bf16)
            kd = (k * jnp.exp(bl - b)).astype(bf16)
            dec = jnp.exp(bl)
            o_int = []
            for p in range(npair):
                u = d * npair + p
                st = s_s[u]
                o_int.append(_dot_nt(qd[:, pw * p:pw * (p + 1)], st.astype(bf16)))
                upd = _dot_tn(v[:, pv * p:pv * (p + 1)], kd[:, pw * p:pw * (p + 1)])
                s_s[u] = st * dec[:, pw * p:pw * (p + 1)] + bdm_ref[...] * upd
            hacc[pl.ds(r0, CHUNK), :] += o + jnp.concatenate(o_int, axis=1)
        return carry

    lax.fori_loop(0, nch, body, 0)
    o_ref[...] = hacc[...].astype(bf16)


def _gla(c_qk, c_v, c_la, nb, ctx_len):
    t = c_qk.shape[0]
    tb = t // nb
    consts = _gla_constants()
    const_spec = lambda a: pl.BlockSpec(a.shape, lambda b, n=a.ndim: (0,) * n, pipeline_mode=pl.Buffered(1))
    return pl.pallas_call(
        functools.partial(_gla_kernel, ctx_len=ctx_len),
        out_shape=SDS((t, MIX_W), bf16), grid=(nb,),
        in_specs=[pl.BlockSpec((tb, 512), lambda b: (b, 0)), pl.BlockSpec((tb, 512), lambda b: (b, 0)),
                  pl.BlockSpec((tb, c_la.shape[1]), lambda b: (b, 0), pipeline_mode=pl.Buffered(1))]
                 + [const_spec(a) for a in consts],
        out_specs=pl.BlockSpec((tb, MIX_W), lambda b: (b, 0)),
        scratch_shapes=[pltpu.VMEM((tb, MIX_W), f32), pltpu.VMEM((C_HEADS, 2 * C_DV, 2 * C_DK), f32)],
        compiler_params=_cp(("arbitrary",)), name="gla",
    )(c_qk, c_v, c_la, *consts)


def _rope_tables(seq, ctx_len):
    n_f = B_DH // 4
    t = np.arange(seq)
    freqs = ROPE_THETA ** (-np.arange(n_f, dtype=np.float32) / n_f)
    hd = np.arange(B_DH)
    pos = np.where(hd[None, :] < B_DH // 2, (t // GRID_W)[:, None], (t % GRID_W)[:, None]).astype(np.float32)
    ang = jnp.asarray(pos * freqs[hd % n_f][None, :], f32)
    sign = np.where((hd % (2 * n_f)) < n_f, -1.0, 1.0).astype(np.float32)
    cos = jnp.concatenate([jnp.ones((ctx_len, B_DH), f32), jnp.cos(ang)], axis=0)
    sin = jnp.concatenate([jnp.zeros((ctx_len, B_DH), f32), jnp.sin(ang) * sign[None, :]], axis=0)
    return jnp.tile(cos, (1, 2)), jnp.tile(sin, (1, 2))


def _attn_prep(bq, bkv, cos, sin, qg, kg, qo_ref, ko_ref, vo_ref):
    def norm_rope(x, g):
        w = x.shape[1]
        bd = (_iota((w, w), 0) // B_DH == _iota((w, w), 1) // B_DH).astype(bf16)
        ss = _dot((x * x).astype(bf16), bd)
        xn = x * lax.rsqrt(ss * (1.0 / B_DH) + EPS) * g
        first = (_iota(x.shape, 1) % (B_DH // 2)) < (B_DH // 4)
        swapped = jnp.where(first, pltpu.roll(xn, w - B_DH // 4, 1), pltpu.roll(xn, B_DH // 4, 1))
        reps = w // LANES
        return xn * jnp.tile(cos, (1, reps)) + swapped * jnp.tile(sin, (1, reps))

    q = (norm_rope(bq, qg) * (B_DH ** -0.5)).astype(bf16)
    for h in range(B_HEADS):
        qo_ref[h] = q[:, B_DH * h:B_DH * (h + 1)]
    k = norm_rope(bkv[:, :B_KV * B_DH], kg).astype(bf16)
    v = bkv[:, B_KV * B_DH:].astype(bf16)
    ones_col = (_iota((bkv.shape[0], LANES - B_DH), 1) == 0).astype(bf16)
    for h in range(B_KV):
        ko_ref[h] = k[:, B_DH * h:B_DH * (h + 1)]
        vo_ref[h] = jnp.concatenate([v[:, B_DH * h:B_DH * (h + 1)], ones_col], axis=1)


def _attn_kernel(q_ref, k_ref, v_ref, o_ref, *, ctx_len):
    tb = k_ref.shape[1]
    q = q_ref[...].reshape(B_GROUP * TQ, B_DH)

    def attend(klen):
        starts = [0] + list(range(ctx_len, klen, KEY_CHUNK))
        m = acc = None
        for s0, s1 in zip(starts, starts[1:] + [klen]):
            s = _dot_nt(q, k_ref[0, s0:s1, :])
            smax = jnp.max(s, axis=-1, keepdims=True)
            if m is None:
                m = smax
                acc = _dot(jnp.exp((s - m).astype(bf16)), v_ref[0, s0:s1, :])
            else:
                m_new = jnp.maximum(m, smax)
                acc = jnp.exp(m - m_new) * acc + _dot(jnp.exp((s - m_new).astype(bf16)), v_ref[0, s0:s1, :])
                m = m_new
        o = acc[:, :B_DH] / acc[:, B_DH:B_DH + 1]
        o_ref[...] = o.reshape(B_GROUP, TQ, B_DH).astype(bf16)

    is_ctx = pl.program_id(2) < ctx_len // TQ

    @pl.when(is_ctx)
    def _():
        attend(ctx_len)

    @pl.when(jnp.logical_not(is_ctx))
    def _():
        attend(tb)


def _attn(q, k, v, nb, ctx_len):
    t = q.shape[1]
    tb = t // nb
    nq = tb // TQ
    return pl.pallas_call(
        functools.partial(_attn_kernel, ctx_len=ctx_len),
        out_shape=SDS((B_HEADS, t, B_DH), bf16), grid=(nb, B_KV, nq),
        in_specs=[pl.BlockSpec((B_GROUP, TQ, B_DH), lambda b, g, i: (g, b * nq + i, 0)),
                  pl.BlockSpec((1, tb, B_DH), lambda b, g, i: (g, b, 0)),
                  pl.BlockSpec((1, tb, LANES), lambda b, g, i: (g, b, 0))],
        out_specs=pl.BlockSpec((B_GROUP, TQ, B_DH), lambda b, g, i: (g, b * nq + i, 0)),
        compiler_params=_cp(("arbitrary", "arbitrary", "arbitrary")), name="attn",
    )(q, k, v)


def _head_rms(y, g, dv):
    parts = []
    for h in range(y.shape[1] // dv):
        yh = y[:, dv * h:dv * (h + 1)]
        parts.append(yh * lax.rsqrt(jnp.mean(yh * yh, axis=-1, keepdims=True) + EPS))
    return jnp.concatenate(parts, axis=1) * g


def _merge_kernel(ha_ref, ao_ref, att_ref, hc_ref, cr_ref, gate_ref, x_ref, g1a_ref, g1b_ref, sh2a_ref, sh2b_ref,
                  sc2a_ref, sc2b_ref, ag_ref, cg_ref, wb_ref, wo_ref, gn2_ref, wr_ref, br_ref,
                  xo_ref, h2_ref, ti_ref, tw_ref, rk_ref, cnt_ref, cnt_s):
    i = pl.program_id(0)

    @pl.when(i == 0)
    def _():
        cnt_s[...] = jnp.zeros_like(cnt_s)

    d = x_ref.shape[1]
    ya = _head_rms(ha_ref[...].astype(f32), ag_ref[...], A_DV) * _sigmoid(ao_ref[...].astype(f32))
    cr = cr_ref[...].astype(f32)
    yc = _head_rms(hc_ref[...].astype(f32), cg_ref[...], C_DV) * (cr * _sigmoid(cr))
    yb = jnp.concatenate([att_ref[h] for h in range(B_HEADS)], axis=1)
    merged = jnp.zeros((TM, d), f32)
    for n, y in enumerate((ya.astype(bf16), yb, yc.astype(bf16))):
        merged = merged + _sigmoid(gate_ref[:, d * n:d * (n + 1)]) * _dot(y, wb_ref[n])
    x = x_ref[...] + _rows(g1a_ref, g1b_ref) * _dot(merged.astype(bf16), wo_ref[...])
    xo_ref[...] = x
    h2 = _rms_mod(x, gn2_ref[...], _rows(sh2a_ref, sh2b_ref), _rows(sc2a_ref, sc2b_ref))
    h2_ref[...] = _pack_bf16_pairs(h2)

    logits = _dot_nt(wr_ref[...], h2, HI) + br_ref[...]
    eid = _iota((N_EXPERTS, TM), 0)
    work = logits
    onehot = jnp.zeros((N_EXPERTS, TM), f32)
    vals, sels = [], []
    for k in range(TOP_K):
        mk = jnp.max(work, axis=0, keepdims=True)
        ik = jnp.min(jnp.where(work == mk, eid, N_EXPERTS), axis=0, keepdims=True)
        sel = eid == ik
        work = jnp.where(sel, -jnp.inf, work)
        onehot = onehot + sel.astype(f32)
        ti_ref[k:k + 1, :] = ik
        vals.append(mk)
        sels.append(sel)
    ex = [jnp.exp(vk - vals[0]) for vk in vals]
    tot = ex[0] + ex[1] + ex[2] + ex[3]
    for k in range(TOP_K):
        tw_ref[k:k + 1, :] = ex[k] / tot

    ut = (_iota((TM, TM), 0) <= _iota((TM, TM), 1)).astype(bf16)
    incl = _dot(onehot.astype(bf16), ut)
    rank = cnt_s[...][:, 0:1] + incl - onehot
    for k in range(TOP_K):
        rk_ref[k:k + 1, :] = jnp.sum(jnp.where(sels[k], rank, 0.0), axis=0, keepdims=True).astype(jnp.int32)
    cnt_s[...] = cnt_s[...] + incl[:, TM - 1:TM]
    cnt_ref[...] = cnt_s[...]


def _merge(p, h_a, att, h_c, x, mods, a_norm_g, c_norm_g, w_branch, w_out, g_norm2, w_router, b_router,
           tpb, ctx_tiles, nb):
    t, d = x.shape
    row = lambda w: pl.BlockSpec((TM, w), lambda i: (i, 0))
    const = lambda shape: pl.BlockSpec(shape, lambda i: tuple(0 for _ in shape))
    return pl.pallas_call(
        _merge_kernel,
        out_shape=(SDS((t, d), f32), SDS((t, d // 2), jnp.int32), SDS((TOP_K, t), jnp.int32), SDS((TOP_K, t), f32),
                   SDS((TOP_K, t), jnp.int32), SDS((N_EXPERTS, LANES), f32)),
        grid=(t // TM,),
        in_specs=[row(512), row(512), pl.BlockSpec((B_HEADS, TM, B_DH), lambda i: (0, i, 0)), row(512), row(512),
                  row(3 * d), row(d)]
                 + _mod_specs(2, tpb, ctx_tiles, nb) + _mod_specs(3, tpb, ctx_tiles, nb) + _mod_specs(4, tpb, ctx_tiles, nb)
                 + [const((1, 512)), const((1, 512)), const((3, MIX_W, d)), const((d, d)), const((1, d)),
                  const((N_EXPERTS, d)), const((N_EXPERTS, 1))],
        out_specs=(row(d), row(d // 2), pl.BlockSpec((TOP_K, TM), lambda i: (0, i)), pl.BlockSpec((TOP_K, TM), lambda i: (0, i)),
                   pl.BlockSpec((TOP_K, TM), lambda i: (0, i)), const((N_EXPERTS, LANES))),
        scratch_shapes=[pltpu.VMEM((N_EXPERTS, LANES), f32)],
        compiler_params=_cp(("arbitrary",)), name="merge",
    )(h_a, p["a_o"], att, h_c, p["c_r"], p["gate"], x, mods, mods, mods, mods, mods, mods,
      a_norm_g.reshape(1, -1), c_norm_g.reshape(1, -1), w_branch, w_out, g_norm2.reshape(1, -1),
      w_router.T, b_router.reshape(-1, 1))


def _sc_mesh():
    return plsc.VectorSubcoreMesh(core_axis_name="c", subcore_axis_name="s")


def _sc_scatter_rows(src, idx, n_out):
    v, d = src.shape
    reps = idx.shape[0] // v
    per_w = v // (SC_NUM_CORES * SC_NUM_SUBCORES)
    assert per_w % SC_WINDOW == 0 and idx.shape[0] == reps * v

    @functools.partial(pl.kernel, out_type=SDS((n_out, d), src.dtype), mesh=_sc_mesh(),
                       scratch_types=[pltpu.VMEM((SC_WINDOW,), jnp.int32), pltpu.VMEM((SC_WINDOW, d), src.dtype),
                                      pltpu.SemaphoreType.DMA])
    def k(x_hbm, i_hbm, o_hbm, idx_v, rows_v, sem):
        wid = lax.axis_index("s") * SC_NUM_CORES + lax.axis_index("c")

        @pl.loop(0, per_w // SC_WINDOW)
        def _(j):
            base = wid * per_w + j * SC_WINDOW
            pltpu.sync_copy(x_hbm.at[pl.ds(base, SC_WINDOW)], rows_v)
            for r in range(reps):
                pltpu.sync_copy(i_hbm.at[pl.ds(r * v + base, SC_WINDOW)], idx_v)
                pltpu.async_copy(rows_v, o_hbm.at[idx_v], sem).wait()

    return k(src, idx)


def _sc_gather_rows(table, idx):
    d = table.shape[1]
    n = idx.shape[0]
    per_w = n // (SC_NUM_CORES * SC_NUM_SUBCORES)
    assert per_w % SC_WINDOW == 0

    @functools.partial(pl.kernel, out_type=SDS((n, d), table.dtype), mesh=_sc_mesh(),
                       scratch_types=[pltpu.VMEM((SC_WINDOW,), jnp.int32), pltpu.VMEM((SC_WINDOW, d), table.dtype),
                                      pltpu.SemaphoreType.DMA])
    def k(x_hbm, i_hbm, o_hbm, idx_v, rows_v, sem):
        wid = lax.axis_index("s") * SC_NUM_CORES + lax.axis_index("c")

        @pl.loop(0, per_w // SC_WINDOW)
        def _(j):
            base = wid * per_w + j * SC_WINDOW
            pltpu.sync_copy(i_hbm.at[pl.ds(base, SC_WINDOW)], idx_v)
            pltpu.async_copy(x_hbm.at[idx_v], rows_v, sem).wait()
            pltpu.sync_copy(rows_v, o_hbm.at[pl.ds(base, SC_WINDOW)])

    return k(table, idx)


GU_BLOCK = 2 * LANES


def _deinterleave_perm():
    n = np.arange(GU_BLOCK)
    src = np.where(n < LANES, 2 * n, 2 * (n - LANES) + 1)
    return jnp.asarray(np.arange(GU_BLOCK)[:, None] == src[None, :], bf16)


def _expert_kernel(te_ref, nv_ref, x_ref, w1_ref, b1_ref, w2_ref, b2_ref, perm_ref, y_ref, w1_s, w2_s):
    i = pl.program_id(0)
    valid = i < nv_ref[0]
    new_expert = jnp.logical_or(i == 0, te_ref[i] != te_ref[jnp.maximum(i - 1, 0)])
    dff2 = w1_ref.shape[2]

    @pl.when(jnp.logical_and(valid, new_expert))
    def _():
        for cb in range(dff2 // GU_BLOCK):
            cs = slice(GU_BLOCK * cb, GU_BLOCK * (cb + 1))
            w1_s[:, cs] = _dot(w1_ref[0, :, cs].astype(bf16), perm_ref[...]).astype(bf16)
        w2_s[...] = w2_ref[0].astype(bf16)

    @pl.when(valid)
    def _():
        x = jnp.concatenate(_unpack_bf16_pairs(x_ref[...]), axis=1).astype(bf16)
        gu = _dot(x, w1_s[...]) + b1_ref[0]
        nblk = dff2 // GU_BLOCK
        g = jnp.concatenate([gu[:, GU_BLOCK * cb:GU_BLOCK * cb + LANES] for cb in range(nblk)], axis=1)
        u = jnp.concatenate([gu[:, GU_BLOCK * cb + LANES:GU_BLOCK * (cb + 1)] for cb in range(nblk)], axis=1)
        gate = jnp.minimum(g, SWIGLU_LIMIT)
        up = jnp.clip(u, -SWIGLU_LIMIT, SWIGLU_LIMIT)
        a = (up + 1.0) * gate * _sigmoid(SWIGLU_ALPHA * gate)
        y_ref[...] = _pack_bf16_pairs(_dot(a.astype(bf16), w2_s[...]) + b2_ref[0])


def _experts(xs, tile_e, n_valid, layer, w1, b1, w2, b2):
    p, dw = xs.shape
    d = 2 * dw
    dff2 = w1.shape[-1]
    dff = w2.shape[2]
    row = lambda i, te, nv: (jnp.minimum(i, nv[0] - 1), 0)
    wsel = lambda i, te, nv: (layer, te[i], 0, 0)
    sq = pl.Squeezed()
    return pl.pallas_call(
        _expert_kernel, out_shape=SDS((p, dw), jnp.int32),
        grid_spec=pltpu.PrefetchScalarGridSpec(
            num_scalar_prefetch=2, grid=(p // TME,),
            in_specs=[pl.BlockSpec((TME, dw), row),
                      pl.BlockSpec((sq, 1, d, dff2), wsel), pl.BlockSpec((sq, 1, 1, dff2), wsel),
                      pl.BlockSpec((sq, 1, dff, d), wsel), pl.BlockSpec((sq, 1, 1, d), wsel),
                      pl.BlockSpec((GU_BLOCK, GU_BLOCK), lambda i, te, nv: (0, 0))],
            out_specs=pl.BlockSpec((TME, dw), row),
            scratch_shapes=[pltpu.VMEM((d, dff2), bf16), pltpu.VMEM((dff, d), bf16)]),
        compiler_params=_cp(("arbitrary",)), name="experts",
    )(tile_e, n_valid, xs, w1, b1, w2, b2, _deinterleave_perm())


def _final_kernel(x_ref, yg_ref, w_ref, g2_ref, gf_ref, o_ref):
    x = x_ref[...] + g2_ref[0] * _weighted_expert_sum(yg_ref, w_ref[...])
    o_ref[...] = x * lax.rsqrt(jnp.mean(x * x, axis=-1, keepdims=True) + EPS) * gf_ref[...]


def _final(x, yg, wcol, mods, tpb, ctx_tiles, nb, g_final):
    d = x.shape[1]
    lat = tpb - ctx_tiles
    rmap = lambda i: ((i // lat) * tpb + ctx_tiles + i % lat)
    return pl.pallas_call(
        _final_kernel, out_shape=SDS((nb * lat * TMH, d), f32), grid=(nb * lat,),
        in_specs=[pl.BlockSpec((TMH, d), lambda i: (rmap(i), 0)),
                  pl.BlockSpec((TOP_K, TMH, d // 2), lambda i: (0, rmap(i), 0)),
                  pl.BlockSpec((TMH, TOP_K), lambda i: (rmap(i), 0)),
                  pl.BlockSpec((1, 1, 1024), lambda i: (i // lat, 0, 5)),
                  pl.BlockSpec((1, d), lambda i: (0, 0))],
        out_specs=pl.BlockSpec((TMH, d), lambda i: (i, 0)),
        compiler_params=_cp(("arbitrary",)), name="final",
    )(x, yg, wcol, mods, g_final.reshape(1, -1))


def _routing_tables(top_i, rank, counts, n_tiles):
    cnt = counts[:, 0].astype(jnp.int32)
    padded = ((cnt + TME - 1) // TME) * TME
    ends = jnp.cumsum(padded)
    starts = ends - padded
    eids = jnp.arange(N_EXPERTS, dtype=jnp.int32)
    start_of = jnp.sum(jnp.where(top_i[..., None] == eids, starts, 0), axis=-1)
    pos = (start_of + rank).reshape(-1)
    n_valid = ends[-1] // TME
    tile_start = jnp.arange(n_tiles, dtype=jnp.int32) * TME
    tile_e = jnp.sum(tile_start[:, None] >= ends[None, :], axis=1).astype(jnp.int32)
    tile_e = jnp.where(jnp.arange(n_tiles) < n_valid, tile_e, tile_e[jnp.maximum(n_valid - 1, 0)])
    tile_e = jnp.minimum(tile_e, N_EXPERTS - 1)
    return pos, tile_e, n_valid.reshape(1).astype(jnp.int32)


def kernel(x, c, ctx, c_ctx, w_ada, b_ada, g_norm1, w_in, a_conv_w, a_conv_b, a_gate_b, a_norm_g, b_q_norm_g, b_k_norm_g, c_w_up, c_b_up, c_norm_g, w_branch, w_out, g_norm2, w_router, b_router, w_e1, b_e1, w_e2, b_e2, g_final):
    nb, seq, d = x.shape
    ctx_len = ctx.shape[1]
    depth = w_ada.shape[0]
    tb = ctx_len + seq
    t = nb * tb
    tpb, ctx_tiles = tb // TMH, ctx_len // TMH
    assert d == 1024 and nb < 16 and seq % TMH == 0 and ctx_len % TMH == 0 and ctx_len % TQ == 0 and t % TM == 0
    n_assign = TOP_K * t
    n_sorted = n_assign + N_EXPERTS * TME
    n_tiles = n_sorted // TME

    xs = jnp.concatenate([ctx, x], axis=1).reshape(t, d)
    cc = jnp.zeros((16, d), f32).at[:nb].set(c).at[nb].set(c_ctx)
    mods_all = _adaln(cc, w_ada, b_ada)
    cos, sin = _rope_tables(seq, ctx_len)
    w_in_r = _arrange_w_in(w_in)
    w_branch_b, w_out_b = w_branch.astype(bf16), w_out.astype(bf16)
    col = np.arange(b_e1.shape[-1])
    within = col % GU_BLOCK
    src = (col // GU_BLOCK) * GU_BLOCK + np.where(within < LANES, 2 * within, 2 * (within - LANES) + 1)
    b1 = b_e1[..., src][..., None, :]
    b2 = b_e2[..., None, :]

    out = None
    prev = None
    for l in range(depth):
        mods = mods_all[l].reshape(16, 1, 6 * d)
        xs, p, (qn, kn, vn) = _inproj(xs, mods, g_norm1[l].reshape(1, -1), w_in_r[l], cos, sin,
                                      b_q_norm_g[l], b_k_norm_g[l], a_gate_b[l], c_w_up[l], c_b_up[l],
                                      tpb, ctx_tiles, nb, prev=prev)
        h_a = _mlstm(p["a_qk"], p["a_v"], p["small"], a_conv_w[l], a_conv_b[l], nb, ctx_len)
        h_c = _gla(p["c_qk"], p["c_v"], p["c_la"], nb, ctx_len)
        att = _attn(qn, kn, vn, nb, ctx_len)
        xs, h2, top_i, top_w, rank, counts = _merge(
            p, h_a, att, h_c, xs, mods, a_norm_g[l], c_norm_g[l], w_branch_b[l], w_out_b[l], g_norm2[l],
            w_router[l], b_router[l], tpb, ctx_tiles, nb)
        pos, tile_e, n_valid = _routing_tables(top_i, rank, counts, n_tiles)
        x_sorted = _sc_scatter_rows(h2, pos, n_sorted)
        y_sorted = _experts(x_sorted, tile_e, n_valid, l, w_e1, b1, w_e2, b2)
        yg = _sc_gather_rows(y_sorted, pos).reshape(TOP_K, t, d // 2)
        if l == depth - 1:
            out = _final(xs, yg, top_w.T, mods, tpb, ctx_tiles, nb, g_final)
        else:
            prev = (yg, top_w.T, mods)
    return out.reshape(nb, seq, d)
```

```python
import functools

import jax
import jax.numpy as jnp
import numpy as np
from jax import lax
from jax.experimental import pallas as pl
from jax.experimental.pallas import tpu as pltpu
from jax.experimental.pallas import tpu_sc as plsc

f32 = jnp.float32
bf16 = jnp.bfloat16
HI = lax.Precision.HIGHEST
SDS = jax.ShapeDtypeStruct

EPS = 1e-6
CHUNK = 256
GLA_CHUNK = 128
GRID_W = 64
ROPE_THETA = 10000.0
MIX_W = 512
A_HEADS, A_DQK, A_DV = 4, 64, 128
B_HEADS, B_KV, B_DH = 8, 2, 64
B_GROUP = B_HEADS // B_KV
C_HEADS, C_DK, C_DV, C_RANK, C_TAU = 4, 64, 128, 16, 16.0
N_EXPERTS, TOP_K = 32, 4
SWIGLU_LIMIT, SWIGLU_ALPHA = 7.0, 1.702

LANES = 128
SC_NUM_CORES = 2
SC_NUM_SUBCORES = 16
VMEM_LIMIT = 56 * 2 ** 20

TMH = 256
TM = 2 * TMH
TQ = 256
TME = 1024
KEY_CHUNK = 2048
SC_WINDOW = 64


def _cp(sem, vmem=VMEM_LIMIT):
    return pltpu.CompilerParams(dimension_semantics=sem, vmem_limit_bytes=vmem)


def _dot(a, b, precision=None):
    return jnp.dot(a, b, preferred_element_type=f32, precision=precision)


def _dot_nt(a, b, precision=None):
    return lax.dot_general(a, b, (((1,), (1,)), ((), ())), preferred_element_type=f32, precision=precision)


def _dot_tn(a, b):
    return lax.dot_general(a, b, (((0,), (0,)), ((), ())), preferred_element_type=f32)


def _sigmoid(x):
    return 0.5 * jnp.tanh(0.5 * x) + 0.5


def _log_sigmoid(x):
    return jnp.minimum(x, 0.0) - jnp.log(1.0 + jnp.exp(-jnp.abs(x)))


def _iota(shape, axis):
    return lax.broadcasted_iota(jnp.int32, shape, axis)


def _pack_bf16_pairs(x):
    n = x.shape[1] // 2
    lo = pltpu.bitcast(x[:, :n].astype(bf16).astype(f32), jnp.uint32)
    hi = pltpu.bitcast(x[:, n:].astype(bf16).astype(f32), jnp.uint32)
    return pltpu.bitcast((lo >> 16) | hi, jnp.int32)


def _unpack_bf16_pairs(w):
    u = pltpu.bitcast(w, jnp.uint32)
    return pltpu.bitcast(u << 16, f32), pltpu.bitcast(u & jnp.uint32(0xFFFF0000), f32)


def _rms_mod(x, g, shift, scale):
    y = x * lax.rsqrt(jnp.mean(x * x, axis=-1, keepdims=True) + EPS) * g
    return y * (1.0 + scale) + shift


def _adaln_kernel(cc_ref, w_ref, b_ref, o_ref):
    cc = cc_ref[...]
    s = cc * _sigmoid(cc)
    o_ref[0] = _dot(s, w_ref[0], HI) + b_ref[0]


def _adaln(cc, w_ada, b_ada):
    nl, d, n = w_ada.shape
    return pl.pallas_call(
        _adaln_kernel, out_shape=SDS((nl, 16, n), f32), grid=(nl, n // 1024),
        in_specs=[pl.BlockSpec((16, d), lambda l, j: (0, 0)),
                  pl.BlockSpec((1, d, 1024), lambda l, j: (l, 0, j)),
                  pl.BlockSpec((1, 1, 1024), lambda l, j: (l, 0, j))],
        out_specs=pl.BlockSpec((1, 16, 1024), lambda l, j: (l, 0, j)),
        compiler_params=_cp(("arbitrary", "arbitrary")), name="adaln",
    )(cc, w_ada, b_ada.reshape(nl, 1, n))


def _mod_specs(col, tpb, ctx_tiles, nb):
    def spec(half):
        def imap(i):
            u = 2 * i + half
            return (jnp.where((u % tpb) >= ctx_tiles, u // tpb, nb), 0, col)
        return pl.BlockSpec((1, 1, 1024), imap)
    return [spec(0), spec(1)]


def _rows(a_ref, b_ref):
    a, b = a_ref[0], b_ref[0]
    return jnp.concatenate([jnp.broadcast_to(a, (TMH, a.shape[1])), jnp.broadcast_to(b, (TMH, b.shape[1]))], axis=0)


_K1_GROUPS = (("a_qk", 512, bf16), ("a_v", 512, bf16), ("a_o", 512, bf16), ("b_q", 512, bf16),
              ("b_kv", 256, bf16), ("c_qk", 512, bf16), ("c_v", 512, bf16), ("c_r", 512, bf16),
              ("gate", 3072, bf16), ("small", 128, f32))
_K1_WIDTH = sum(w for _, w, _ in _K1_GROUPS)


def _arrange_w_in(w_in):
    z = jnp.zeros(w_in.shape[:-1] + (LANES - 48,), w_in.dtype)
    w = jnp.concatenate([w_in[..., 0:1024], w_in[..., 1040:1552], w_in[..., 1552:2320], w_in[..., 2320:3344],
                         w_in[..., 3376:3888], w_in[..., 3888:6960], w_in[..., 1024:1040], w_in[..., 3344:3376], z],
                        axis=-1)
    assert w.shape[-1] == _K1_WIDTH
    return w.astype(bf16)


_ATTN_GROUPS = ("b_q", "b_kv")
_STORED_GROUPS = tuple(g for g in _K1_GROUPS if g[0] not in _ATTN_GROUPS)


def _weighted_expert_sum(yg_ref, w):
    acc_lo = acc_hi = None
    for k in range(TOP_K):
        lo, hi = _unpack_bf16_pairs(yg_ref[k])
        wk = w[:, k:k + 1]
        acc_lo = wk * lo if acc_lo is None else acc_lo + wk * lo
        acc_hi = wk * hi if acc_hi is None else acc_hi + wk * hi
    return jnp.concatenate([acc_lo, acc_hi], axis=1)


def _inproj_kernel(*refs, fuse_prev):
    refs = list(refs)
    x_ref = refs.pop(0)
    x = x_ref[...]
    if fuse_prev:
        yg_ref, wcol_ref, g2a_ref, g2b_ref = refs[:4]
        refs = refs[4:]
        x = x + _rows(g2a_ref, g2b_ref) * _weighted_expert_sum(yg_ref, wcol_ref[...])
    (sha_ref, shb_ref, sca_ref, scb_ref, g_ref, w_ref, cosa_ref, cosb_ref, sina_ref, sinb_ref, qg_ref, kg_ref,
     gb_ref, wup_ref, bup_ref) = refs[:15]
    outs = refs[15:]
    if fuse_prev:
        outs.pop(0)[...] = x
    h = _rms_mod(x, g_ref[...], _rows(sha_ref, shb_ref), _rows(sca_ref, scb_ref)).astype(bf16)
    c0 = 0
    attn_in = {}
    n_stored = len(_STORED_GROUPS) + 1
    stored = iter(outs[:n_stored])
    for name, width, dt in _K1_GROUPS:
        if name in _ATTN_GROUPS:
            attn_in[name] = _dot(h, w_ref[:, c0:c0 + width])
        elif name == "small":
            sm = _dot(h, w_ref[:, c0:c0 + width]) + gb_ref[...]
            lane = _iota(sm.shape, 1)
            is_f = jnp.logical_and(lane >= 2 * A_HEADS, lane < 4 * A_HEADS)
            next(stored)[...] = jnp.where(is_f, _log_sigmoid(sm), sm)
            la_ref = next(stored)
            sm_hi = sm.astype(bf16)
            sm_lo = (sm - sm_hi.astype(f32)).astype(bf16)
            sm3 = jnp.concatenate([sm_hi, sm_lo, sm_hi], axis=1)
            for dr in range(2):
                nk = C_HEADS * C_DK
                la_ref[:, nk * dr:nk * (dr + 1)] = _log_sigmoid(_dot(sm3, wup_ref[dr]) + bup_ref[dr]) * (1.0 / C_TAU)
        else:
            o_ref = next(stored)
            for j in range(0, width, 512):
                wj = min(512, width - j)
                o_ref[:, j:j + wj] = _dot(h, w_ref[:, c0 + j:c0 + j + wj]).astype(dt)
        c0 += width
    qo_ref, ko_ref, vo_ref = outs[n_stored:]
    cos = jnp.concatenate([cosa_ref[...], cosb_ref[...]], axis=0)
    sin = jnp.concatenate([sina_ref[...], sinb_ref[...]], axis=0)
    _attn_prep(attn_in["b_q"], attn_in["b_kv"], cos, sin, qg_ref[...], kg_ref[...], qo_ref, ko_ref, vo_ref)


def _inproj(x, mods, g, w, cos, sin, qg, kg, gate_b, w_up, b_up, tpb, ctx_tiles, nb, prev=None):
    t, d = x.shape
    row = lambda wd: pl.BlockSpec((TM, wd), lambda i: (i, 0))
    const = lambda shape: pl.BlockSpec(shape, lambda i: (0,) * len(shape))
    in_specs, args = [row(d)], [x]
    out_shape, out_specs = [], []
    nk = C_HEADS * C_DK
    gb = jnp.zeros((1, LANES), f32).at[0, :4 * A_HEADS].set(gate_b.reshape(-1))
    wz = jnp.zeros((2, LANES, nk), f32)
    wz = wz.at[0, 16:16 + C_RANK].set(w_up[0]).at[1, 16 + C_RANK:16 + 2 * C_RANK].set(w_up[1])
    wz_hi = wz.astype(bf16)
    wz_lo = (wz - wz_hi.astype(f32)).astype(bf16)
    wz = jnp.concatenate([wz_hi, wz_hi, wz_lo], axis=1)
    if prev is not None:
        yg, wcol, mods_prev = prev
        in_specs += [pl.BlockSpec((TOP_K, TM, d // 2), lambda i: (0, i, 0)), row(TOP_K)]
        in_specs += _mod_specs(5, tpb, ctx_tiles, nb)
        args += [yg, wcol, mods_prev, mods_prev]
        out_shape.append(SDS((t, d), f32))
        out_specs.append(row(d))
    table = lambda half: pl.BlockSpec((TMH, LANES), lambda i: ((2 * i + half) % tpb, 0))
    in_specs += _mod_specs(0, tpb, ctx_tiles, nb) + _mod_specs(1, tpb, ctx_tiles, nb)
    in_specs += [const((1, d)), pl.BlockSpec((d, _K1_WIDTH), lambda i: (0, 0), pipeline_mode=pl.Buffered(1)),
                 table(0), table(1), table(0), table(1), const((1, B_HEADS * B_DH)), const((1, B_KV * B_DH)),
                 const((1, LANES)), const((2, 3 * LANES, nk)), const((2, 1, nk))]
    args += [mods, mods, mods, mods, g, w, cos, cos, sin, sin,
             jnp.tile(qg, B_HEADS).reshape(1, -1), jnp.tile(kg, B_KV).reshape(1, -1), gb, wz, b_up.reshape(2, 1, nk)]
    out_shape += [SDS((t, wd), dt) for _, wd, dt in _STORED_GROUPS] + [SDS((t, 2 * nk), f32)]
    out_specs += [row(wd) for _, wd, _ in _STORED_GROUPS] + [row(2 * nk)]
    out_shape += [SDS((B_HEADS, t, B_DH), bf16), SDS((B_KV, t, B_DH), bf16), SDS((B_KV, t, LANES), bf16)]
    out_specs += [pl.BlockSpec((B_HEADS, TM, B_DH), lambda i: (0, i, 0)),
                  pl.BlockSpec((B_KV, TM, B_DH), lambda i: (0, i, 0)),
                  pl.BlockSpec((B_KV, TM, LANES), lambda i: (0, i, 0))]
    res = list(pl.pallas_call(
        functools.partial(_inproj_kernel, fuse_prev=prev is not None),
        out_shape=tuple(out_shape), grid=(t // TM,), in_specs=in_specs, out_specs=tuple(out_specs),
        compiler_params=_cp(("arbitrary",)), name="inproj",
    )(*args))
    x_new = res.pop(0) if prev is not None else x
    p = dict(zip([n for n, _, _ in _STORED_GROUPS] + ["c_la"], res))
    return x_new, p, tuple(res[len(_STORED_GROUPS) + 1:])


def _bwd_chunk(i, ncc, nch):
    return jnp.where(i < ncc, ncc - 1 - i, nch - 1 + ncc - i)


def _split2(x, axis=0):
    hi = x.astype(bf16)
    return jnp.concatenate([hi, (x - hi.astype(f32)).astype(bf16)], axis=axis)


def _mlstm_kernel(qk_ref, v_ref, sm_ref, cw_ref, cb_ref, tri_ref, lmask_ref, exps_ref, expd_ref, expv_ref,
                  oseg_ref, kmask_ref, vmask_ref, cmask_ref, nmask_ref, o_ref, qk_s, hacc, c_s, n_s, *, ctx_len):
    tb = qk_ref.shape[0]
    nch = tb // CHUNK
    ncc = ctx_len // CHUNK
    nu = 2 * A_HEADS

    cw = cw_ref[...]
    cb = cb_ref[...]
    row = _iota((CHUNK, 2 * A_HEADS * A_DQK), 0)
    kscale = jnp.where(_iota((1, 2 * A_HEADS * A_DQK), 1) >= A_HEADS * A_DQK, A_DQK ** -0.5, 1.0)

    def conv_body(c, carry):
        r0 = pl.multiple_of(c * CHUNK, CHUNK)
        x = qk_ref[pl.ds(r0, CHUNK), :].astype(f32)
        rp = pl.multiple_of(jnp.maximum(r0 - 16, 0), 16)
        rn = pl.multiple_of(jnp.minimum(r0 + CHUNK, tb - 16), 16)
        prev = qk_ref[pl.ds(rp, 16), :].astype(f32)[15:16, :]
        nxt = qk_ref[pl.ds(rn, 16), :].astype(f32)[0:1, :]
        has_prev = jnp.logical_and(r0 != 0, r0 != ctx_len)
        has_next = jnp.logical_and(r0 + CHUNK != ctx_len, r0 + CHUNK != tb)
        prev = jnp.where(has_prev, prev, 0.0)
        nxt = jnp.where(has_next, nxt, 0.0)
        xm = jnp.where(row == 0, prev, pltpu.roll(x, 1, 0))
        xp = jnp.where(row == CHUNK - 1, nxt, pltpu.roll(x, CHUNK - 1, 0))
        y = cw[0:1] * xm + cw[1:2] * x + cw[2:3] * xp + cb
        y = y * _sigmoid(y) * kscale
        qk_s[pl.ds(r0, CHUNK), :] = y.astype(bf16)
        return carry

    lax.fori_loop(0, nch, conv_body, 0)

    hacc[...] = jnp.zeros_like(hacc)
    c_s[...] = jnp.zeros_like(c_s)
    n_s[...] = jnp.zeros_like(n_s)

    lane = _iota((CHUNK, LANES), 1)
    row = _iota((CHUNK, LANES), 0)
    neg_inf = jnp.float32(-jnp.inf)
    nk = A_HEADS * A_DQK
    ones_rows = jnp.ones((CHUNK, LANES), bf16)

    def cummax(x, d):
        for sh in [1 << e for e in range(CHUNK.bit_length() - 1)]:
            if d == 0:
                y = jnp.where(row >= sh, pltpu.roll(x, sh, 0), neg_inf)
            else:
                y = jnp.where(row < CHUNK - sh, pltpu.roll(x, CHUNK - sh, 0), neg_inf)
            x = jnp.maximum(x, y)
        return x

    def body(i, ms):
        ms = list(ms)
        pending = []
        for d in range(2):
            c = i if d == 0 else _bwd_chunk(i, ncc, nch)
            r0 = pl.multiple_of(c * CHUNK, CHUNK)
            qk = qk_s[pl.ds(r0, CHUNK), :]
            q, k = qk[:, :nk], qk[:, nk:]
            v = v_ref[pl.ds(r0, CHUNK), :]
            mine = jnp.logical_and(lane >= d * A_HEADS, lane < (d + 1) * A_HEADS)
            keep = lambda x: jnp.where(mine[:x.shape[0]], x, 0.0)
            pre = sm_ref[pl.ds(r0, CHUNK), :]
            b = pltpu.roll(_dot(tri_ref[d], _split2(pre)), LANES - nu, 1)
            g = keep(pre - b)
            cm = cummax(g, d)
            g_t = g.T
            g_row = jnp.concatenate([g_t[d * A_HEADS + h:d * A_HEADS + h + 1, :] for h in range(A_HEADS)], axis=1)
            pm = jnp.exp(jnp.where(lmask_ref[d] > 0, g_row - _dot(_split2(cm, 1), exps_ref[d]), neg_inf))
            kst = jnp.tile(k, (A_HEADS, 1)) * kmask_ref[...]
            s = (_dot_nt(q, kst) * pm).astype(bf16)
            p1 = _dot(s, jnp.tile(v, (A_HEADS, 1)) * vmask_ref[...])
            sden = _dot(s, oseg_ref[d])
            cbd, nbd = c_s[d], n_s[d]
            qc = _dot(q, cbd.astype(bf16))
            qn = _dot(q, nbd.astype(bf16))
            m = ms[d]
            rmax, li = b + cm, b + m
            mj = jnp.maximum(li, rmax)
            alpha, beta = jnp.exp(rmax - mj), jnp.exp(li - mj)
            inv = 1.0 / jnp.maximum(jnp.abs(alpha * sden + beta * qn), jnp.exp(-mj))
            last = CHUNK - 1 if d == 0 else 0
            b_last, m_new, mloc = b[last:last + 1], mj[last:last + 1], rmax[last:last + 1]
            dec, gain = keep(jnp.exp(b_last + m - m_new)), keep(jnp.exp(mloc - m_new))
            sc = jnp.concatenate([keep(alpha * inv), keep(beta * inv), dec, gain, jnp.zeros((6, LANES), f32)], axis=0)
            scx = _dot(_split2(sc, 1), expv_ref[d])
            hout = scx[:CHUNK] * p1 + scx[CHUNK:2 * CHUNK] * qc
            wl = keep(jnp.exp(b_last - b + pre - mloc))
            kw = (k.astype(f32) * _dot(_split2(wl, 1), expd_ref[d])).astype(bf16)
            c_new = cbd * scx[2 * CHUNK:2 * CHUNK + 1] + (_dot_tn(kw, v) * cmask_ref[...]) * scx[2 * CHUNK + 1:2 * CHUNK + 2]
            n_new = nbd * dec + (_dot_tn(kw, ones_rows) * nmask_ref[d]) * gain
            ms[d] = keep(m_new)
            pending.append((r0, hout, c_new, n_new))
        for d, (r0, hout, c_new, n_new) in enumerate(pending):
            hacc[pl.ds(r0, CHUNK), :] += hout
            c_s[d] = c_new
            n_s[d] = n_new
        return tuple(ms)

    lax.fori_loop(0, nch, body, tuple(jnp.zeros((1, LANES), f32) for _ in range(2)))
    o_ref[...] = hacc[...].astype(bf16)


def _mlstm_constants():
    t = np.arange(CHUNK)
    lane = np.arange(LANES)
    hs = np.arange(A_HEADS * CHUNK)
    hd = np.arange(A_HEADS * A_DQK)
    hv = np.arange(A_HEADS * A_DV)
    tri3, lmask, exps, expd, expv, oseg, nmask = [], [], [], [], [], [], []
    for d in range(2):
        pos = t if d == 0 else CHUNK - 1 - t
        tri = pos[None, :] <= pos[:, None]
        tri3.append(np.tile(tri, (1, 2)))
        lmask.append(np.tile(tri, (1, A_HEADS)))
        unit = lambda idx, width: lane[:, None] == d * A_HEADS + idx[None, :] // width
        exps.append(np.tile(unit(hs, CHUNK), (2, 1)))
        expd.append(np.tile(unit(hd, A_DQK), (2, 1)))
        expv.append(np.tile(unit(hv, A_DV), (2, 1)))
        oseg.append(unit(hs, CHUNK).T)
        nmask.append(unit(hd, A_DQK).T)
    kmask = hs[:, None] // CHUNK == hd[None, :] // A_DQK
    vmask = hs[:, None] // CHUNK == hv[None, :] // A_DV
    cmask = hd[:, None] // A_DQK == hv[None, :] // A_DV
    as_b = lambda x: jnp.asarray(np.stack(x) if isinstance(x, list) else x, bf16)
    return (as_b(tri3), as_b(lmask), as_b(exps), as_b(expd), as_b(expv), as_b(oseg), as_b(kmask), as_b(vmask),
            jnp.asarray(cmask, f32), jnp.asarray(np.stack(nmask), f32))


def _mlstm(a_qk, a_v, small, conv_w, conv_b, nb, ctx_len):
    t = a_qk.shape[0]
    tb = t // nb
    consts = _mlstm_constants()
    const_spec = lambda a: pl.BlockSpec(a.shape, lambda b, n=a.ndim: (0,) * n, pipeline_mode=pl.Buffered(1))
    return pl.pallas_call(
        functools.partial(_mlstm_kernel, ctx_len=ctx_len),
        out_shape=SDS((t, MIX_W), bf16), grid=(nb,),
        in_specs=[pl.BlockSpec((tb, 512), lambda b: (b, 0)), pl.BlockSpec((tb, 512), lambda b: (b, 0)),
                  pl.BlockSpec((tb, LANES), lambda b: (b, 0)),
                  pl.BlockSpec((3, 512), lambda b: (0, 0)), pl.BlockSpec((1, 512), lambda b: (0, 0))]
                 + [const_spec(a) for a in consts],
        out_specs=pl.BlockSpec((tb, MIX_W), lambda b: (b, 0)),
        scratch_shapes=[pltpu.VMEM((tb, 512), bf16), pltpu.VMEM((tb, MIX_W), f32),
                        pltpu.VMEM((2, A_HEADS * A_DQK, A_HEADS * A_DV), f32),
                        pltpu.VMEM((2, A_HEADS * A_DQK, LANES), f32)],
        compiler_params=_cp(("arbitrary",)), name="mlstm",
    )(a_qk, a_v, small, conv_w, conv_b.reshape(1, -1), *consts)


GLA_LEVELS = GLA_CHUNK.bit_length() - 1


def _gla_constants():
    CHUNK = GLA_CHUNK
    t = np.arange(CHUNK)
    tri, gref, lmask = [], [], []
    for d in range(2):
        pos = t if d == 0 else CHUNK - 1 - t
        row_of = np.argsort(pos)
        tri.append(pos[None, :] <= pos[:, None])
        g, m = [], []
        for lvl in range(GLA_LEVELS):
            half = CHUNK >> (lvl + 1)
            ref_pos = (pos // (2 * half)) * (2 * half) + half
            g.append(t[None, :] == row_of[ref_pos][:, None])
            late, early = (pos % (2 * half)) >= half, (pos % (2 * half)) < half
            same = (pos[:, None] // (2 * half)) == (pos[None, :] // (2 * half))
            m.append(same & late[:, None] & early[None, :])
        m.append(t[:, None] == t[None, :])
        gref.append(np.concatenate(g, axis=0))
        lmask.append(np.stack([np.tile(x, (1, C_HEADS)) for x in m]))
    tri3 = np.stack([np.tile(x, (1, 2)) for x in tri])
    gref3 = np.stack([np.tile(x, (1, 2)) for x in gref])
    hs = np.arange(C_HEADS * CHUNK)
    kmask = (hs[:, None] // CHUNK) == (np.arange(C_HEADS * C_DK)[None, :] // C_DK)
    vmask = (hs[:, None] // CHUNK) == (np.arange(C_HEADS * C_DV)[None, :] // C_DV)
    bdm = (np.arange(2 * C_DV)[:, None] // C_DV) == (np.arange(2 * C_DK)[None, :] // C_DK)
    as_b = lambda x: jnp.asarray(x, bf16)
    return as_b(tri3), as_b(gref3), jnp.asarray(np.stack(lmask), f32), as_b(kmask), as_b(vmask), jnp.asarray(bdm, f32)


def _gla_kernel(qk_ref, v_ref, la_ref, tri_ref, gref_ref, lmask_ref, kmask_ref, vmask_ref, bdm_ref,
                o_ref, hacc, s_s, *, ctx_len):
    CHUNK = GLA_CHUNK
    tb = qk_ref.shape[0]
    nch = tb // CHUNK
    ncc = ctx_len // CHUNK
    nk = C_HEADS * C_DK
    pw = 2 * C_DK
    pv = 2 * C_DV
    npair = C_HEADS // 2

    hacc[...] = jnp.zeros_like(hacc)
    s_s[...] = jnp.zeros_like(s_s)
    qscale = C_DK ** -0.5

    def body(i, carry):
        for d in range(2):
            c = i if d == 0 else _bwd_chunk(i, ncc, nch)
            r0 = pl.multiple_of(c * CHUNK, CHUNK)
            la = la_ref[pl.ds(r0, CHUNK), nk * d:nk * (d + 1)]
            qk = qk_ref[pl.ds(r0, CHUNK), :]
            q = qk[:, :nk].astype(f32) * qscale
            k = qk[:, nk:].astype(f32)
            v = v_ref[pl.ds(r0, CHUNK), :]
            b = _dot(tri_ref[d], _split2(la))
            bref = _dot(gref_ref[d], _split2(b))

            def scores(qe, ke):
                kst = jnp.tile(ke, (C_HEADS, 1)) * kmask_ref[...]
                return _dot_nt(qe, kst)

            qb, kb = q.astype(bf16), k.astype(bf16)
            a = scores(qb, kb) * lmask_ref[d, GLA_LEVELS]
            for lvl in range(GLA_LEVELS):
                rel = b - bref[CHUNK * lvl:CHUNK * (lvl + 1)]
                s = scores(qb * jnp.exp(jnp.minimum(rel, 0.0)).astype(bf16),
                           kb * jnp.exp(jnp.minimum(-rel, 0.0)).astype(bf16))
                a = a + s * lmask_ref[d, lvl]
            vst = jnp.tile(v, (C_HEADS, 1)) * vmask_ref[...]
            o = _dot(a.astype(bf16), vst)

            last = CHUNK - 1 if d == 0 else 0
            bl = b[last:last + 1]
            qd = (q * jnp.exp(b)).astype(bf16)
            kd = (k * jnp.exp(bl - b)).astype(bf16)
            dec = jnp.exp(bl)
            o_int = []
            for p in range(npair):
                u = d * npair + p
                st = s_s[u]
                o_int.append(_dot_nt(qd[:, pw * p:pw * (p + 1)], st.astype(bf16)))
                upd = _dot_tn(v[:, pv * p:pv * (p + 1)], kd[:, pw * p:pw * (p + 1)])
                s_s[u] = st * dec[:, pw * p:pw * (p + 1)] + bdm_ref[...] * upd
            hacc[pl.ds(r0, CHUNK), :] += o + jnp.concatenate(o_int, axis=1)
        return carry

    lax.fori_loop(0, nch, body, 0)
    o_ref[...] = hacc[...].astype(bf16)


def _gla(c_qk, c_v, c_la, nb, ctx_len):
    t = c_qk.shape[0]
    tb = t // nb
    consts = _gla_constants()
    const_spec = lambda a: pl.BlockSpec(a.shape, lambda b, n=a.ndim: (0,) * n, pipeline_mode=pl.Buffered(1))
    return pl.pallas_call(
        functools.partial(_gla_kernel, ctx_len=ctx_len),
        out_shape=SDS((t, MIX_W), bf16), grid=(nb,),
        in_specs=[pl.BlockSpec((tb, 512), lambda b: (b, 0)), pl.BlockSpec((tb, 512), lambda b: (b, 0)),
                  pl.BlockSpec((tb, c_la.shape[1]), lambda b: (b, 0), pipeline_mode=pl.Buffered(1))]
                 + [const_spec(a) for a in consts],
        out_specs=pl.BlockSpec((tb, MIX_W), lambda b: (b, 0)),
        scratch_shapes=[pltpu.VMEM((tb, MIX_W), f32), pltpu.VMEM((C_HEADS, 2 * C_DV, 2 * C_DK), f32)],
        compiler_params=_cp(("arbitrary",)), name="gla",
    )(c_qk, c_v, c_la, *consts)


def _rope_tables(seq, ctx_len):
    n_f = B_DH // 4
    t = np.arange(seq)
    freqs = ROPE_THETA ** (-np.arange(n_f, dtype=np.float32) / n_f)
    hd = np.arange(B_DH)
    pos = np.where(hd[None, :] < B_DH // 2, (t // GRID_W)[:, None], (t % GRID_W)[:, None]).astype(np.float32)
    ang = jnp.asarray(pos * freqs[hd % n_f][None, :], f32)
    sign = np.where((hd % (2 * n_f)) < n_f, -1.0, 1.0).astype(np.float32)
    cos = jnp.concatenate([jnp.ones((ctx_len, B_DH), f32), jnp.cos(ang)], axis=0)
    sin = jnp.concatenate([jnp.zeros((ctx_len, B_DH), f32), jnp.sin(ang) * sign[None, :]], axis=0)
    return jnp.tile(cos, (1, 2)), jnp.tile(sin, (1, 2))


def _attn_prep(bq, bkv, cos, sin, qg, kg, qo_ref, ko_ref, vo_ref):
    def norm_rope(x, g):
        w = x.shape[1]
        bd = (_iota((w, w), 0) // B_DH == _iota((w, w), 1) // B_DH).astype(bf16)
        ss = _dot((x * x).astype(bf16), bd)
        xn = x * lax.rsqrt(ss * (1.0 / B_DH) + EPS) * g
        first = (_iota(x.shape, 1) % (B_DH // 2)) < (B_DH // 4)
        swapped = jnp.where(first, pltpu.roll(xn, w - B_DH // 4, 1), pltpu.roll(xn, B_DH // 4, 1))
        reps = w // LANES
        return xn * jnp.tile(cos, (1, reps)) + swapped * jnp.tile(sin, (1, reps))

    q = (norm_rope(bq, qg) * (B_DH ** -0.5)).astype(bf16)
    for h in range(B_HEADS):
        qo_ref[h] = q[:, B_DH * h:B_DH * (h + 1)]
    k = norm_rope(bkv[:, :B_KV * B_DH], kg).astype(bf16)
    v = bkv[:, B_KV * B_DH:].astype(bf16)
    ones_col = (_iota((bkv.shape[0], LANES - B_DH), 1) == 0).astype(bf16)
    for h in range(B_KV):
        ko_ref[h] = k[:, B_DH * h:B_DH * (h + 1)]
        vo_ref[h] = jnp.concatenate([v[:, B_DH * h:B_DH * (h + 1)], ones_col], axis=1)


def _attn_kernel(q_ref, k_ref, v_ref, o_ref, *, ctx_len):
    tb = k_ref.shape[1]
    q = q_ref[...].reshape(B_GROUP * TQ, B_DH)

    def attend(klen):
        starts = [0] + list(range(ctx_len, klen, KEY_CHUNK))
        m = acc = None
        for s0, s1 in zip(starts, starts[1:] + [klen]):
            s = _dot_nt(q, k_ref[0, s0:s1, :])
            smax = jnp.max(s, axis=-1, keepdims=True)
            if m is None:
                m = smax
                acc = _dot(jnp.exp((s - m).astype(bf16)), v_ref[0, s0:s1, :])
            else:
                m_new = jnp.maximum(m, smax)
                acc = jnp.exp(m - m_new) * acc + _dot(jnp.exp((s - m_new).astype(bf16)), v_ref[0, s0:s1, :])
                m = m_new
        o = acc[:, :B_DH] / acc[:, B_DH:B_DH + 1]
        o_ref[...] = o.reshape(B_GROUP, TQ, B_DH).astype(bf16)

    is_ctx = pl.program_id(2) < ctx_len // TQ

    @pl.when(is_ctx)
    def _():
        attend(ctx_len)

    @pl.when(jnp.logical_not(is_ctx))
    def _():
        attend(tb)


def _attn(q, k, v, nb, ctx_len):
    t = q.shape[1]
    tb = t // nb
    nq = tb // TQ
    return pl.pallas_call(
        functools.partial(_attn_kernel, ctx_len=ctx_len),
        out_shape=SDS((B_HEADS, t, B_DH), bf16), grid=(nb, B_KV, nq),
        in_specs=[pl.BlockSpec((B_GROUP, TQ, B_DH), lambda b, g, i: (g, b * nq + i, 0)),
                  pl.BlockSpec((1, tb, B_DH), lambda b, g, i: (g, b, 0)),
                  pl.BlockSpec((1, tb, LANES), lambda b, g, i: (g, b, 0))],
        out_specs=pl.BlockSpec((B_GROUP, TQ, B_DH), lambda b, g, i: (g, b * nq + i, 0)),
        compiler_params=_cp(("arbitrary", "arbitrary", "arbitrary")), name="attn",
    )(q, k, v)


def _head_rms(y, g, dv):
    parts = []
    for h in range(y.shape[1] // dv):
        yh = y[:, dv * h:dv * (h + 1)]
        parts.append(yh * lax.rsqrt(jnp.mean(yh * yh, axis=-1, keepdims=True) + EPS))
    return jnp.concatenate(parts, axis=1) * g


def _merge_kernel(ha_ref, ao_ref, att_ref, hc_ref, cr_ref, gate_ref, x_ref, g1a_ref, g1b_ref, sh2a_ref, sh2b_ref,
                  sc2a_ref, sc2b_ref, ag_ref, cg_ref, wb_ref, wo_ref, gn2_ref, wr_ref, br_ref,
                  xo_ref, h2_ref, ti_ref, tw_ref, rk_ref, cnt_ref, cnt_s):
    i = pl.program_id(0)

    @pl.when(i == 0)
    def _():
        cnt_s[...] = jnp.zeros_like(cnt_s)

    d = x_ref.shape[1]
    ya = _head_rms(ha_ref[...].astype(f32), ag_ref[...], A_DV) * _sigmoid(ao_ref[...].astype(f32))
    cr = cr_ref[...].astype(f32)
    yc = _head_rms(hc_ref[...].astype(f32), cg_ref[...], C_DV) * (cr * _sigmoid(cr))
    yb = jnp.concatenate([att_ref[h] for h in range(B_HEADS)], axis=1)
    merged = jnp.zeros((TM, d), f32)
    for n, y in enumerate((ya.astype(bf16), yb, yc.astype(bf16))):
        merged = merged + _sigmoid(gate_ref[:, d * n:d * (n + 1)]) * _dot(y, wb_ref[n])
    x = x_ref[...] + _rows(g1a_ref, g1b_ref) * _dot(merged.astype(bf16), wo_ref[...])
    xo_ref[...] = x
    h2 = _rms_mod(x, gn2_ref[...], _rows(sh2a_ref, sh2b_ref), _rows(sc2a_ref, sc2b_ref))
    h2_ref[...] = _pack_bf16_pairs(h2)

    logits = _dot_nt(wr_ref[...], h2, HI) + br_ref[...]
    eid = _iota((N_EXPERTS, TM), 0)
    work = logits
    onehot = jnp.zeros((N_EXPERTS, TM), f32)
    vals, sels = [], []
    for k in range(TOP_K):
        mk = jnp.max(work, axis=0, keepdims=True)
        ik = jnp.min(jnp.where(work == mk, eid, N_EXPERTS), axis=0, keepdims=True)
        sel = eid == ik
        work = jnp.where(sel, -jnp.inf, work)
        onehot = onehot + sel.astype(f32)
        ti_ref[k:k + 1, :] = ik
        vals.append(mk)
        sels.append(sel)
    ex = [jnp.exp(vk - vals[0]) for vk in vals]
    tot = ex[0] + ex[1] + ex[2] + ex[3]
    for k in range(TOP_K):
        tw_ref[k:k + 1, :] = ex[k] / tot

    ut = (_iota((TM, TM), 0) <= _iota((TM, TM), 1)).astype(bf16)
    incl = _dot(onehot.astype(bf16), ut)
    rank = cnt_s[...][:, 0:1] + incl - onehot
    for k in range(TOP_K):
        rk_ref[k:k + 1, :] = jnp.sum(jnp.where(sels[k], rank, 0.0), axis=0, keepdims=True).astype(jnp.int32)
    cnt_s[...] = cnt_s[...] + incl[:, TM - 1:TM]
    cnt_ref[...] = cnt_s[...]


def _merge(p, h_a, att, h_c, x, mods, a_norm_g, c_norm_g, w_branch, w_out, g_norm2, w_router, b_router,
           tpb, ctx_tiles, nb):
    t, d = x.shape
    row = lambda w: pl.BlockSpec((TM, w), lambda i: (i, 0))
    const = lambda shape: pl.BlockSpec(shape, lambda i: tuple(0 for _ in shape))
    return pl.pallas_call(
        _merge_kernel,
        out_shape=(SDS((t, d), f32), SDS((t, d // 2), jnp.int32), SDS((TOP_K, t), jnp.int32), SDS((TOP_K, t), f32),
                   SDS((TOP_K, t), jnp.int32), SDS((N_EXPERTS, LANES), f32)),
        grid=(t // TM,),
        in_specs=[row(512), row(512), pl.BlockSpec((B_HEADS, TM, B_DH), lambda i: (0, i, 0)), row(512), row(512),
                  row(3 * d), row(d)]
                 + _mod_specs(2, tpb, ctx_tiles, nb) + _mod_specs(3, tpb, ctx_tiles, nb) + _mod_specs(4, tpb, ctx_tiles, nb)
                 + [const((1, 512)), const((1, 512)), const((3, MIX_W, d)), const((d, d)), const((1, d)),
                  const((N_EXPERTS, d)), const((N_EXPERTS, 1))],
        out_specs=(row(d), row(d // 2), pl.BlockSpec((TOP_K, TM), lambda i: (0, i)), pl.BlockSpec((TOP_K, TM), lambda i: (0, i)),
                   pl.BlockSpec((TOP_K, TM), lambda i: (0, i)), const((N_EXPERTS, LANES))),
        scratch_shapes=[pltpu.VMEM((N_EXPERTS, LANES), f32)],
        compiler_params=_cp(("arbitrary",)), name="merge",
    )(h_a, p["a_o"], att, h_c, p["c_r"], p["gate"], x, mods, mods, mods, mods, mods, mods,
      a_norm_g.reshape(1, -1), c_norm_g.reshape(1, -1), w_branch, w_out, g_norm2.reshape(1, -1),
      w_router.T, b_router.reshape(-1, 1))


def _sc_mesh():
    return plsc.VectorSubcoreMesh(core_axis_name="c", subcore_axis_name="s")


def _sc_scatter_rows(src, idx, n_out):
    v, d = src.shape
    reps = idx.shape[0] // v
    per_w = v // (SC_NUM_CORES * SC_NUM_SUBCORES)
    assert per_w % SC_WINDOW == 0 and idx.shape[0] == reps * v

    @functools.partial(pl.kernel, out_type=SDS((n_out, d), src.dtype), mesh=_sc_mesh(),
                       scratch_types=[pltpu.VMEM((SC_WINDOW,), jnp.int32), pltpu.VMEM((SC_WINDOW, d), src.dtype),
                                      pltpu.SemaphoreType.DMA])
    def k(x_hbm, i_hbm, o_hbm, idx_v, rows_v, sem):
        wid = lax.axis_index("s") * SC_NUM_CORES + lax.axis_index("c")

        @pl.loop(0, per_w // SC_WINDOW)
        def _(j):
            base = wid * per_w + j * SC_WINDOW
            pltpu.sync_copy(x_hbm.at[pl.ds(base, SC_WINDOW)], rows_v)
            for r in range(reps):
                pltpu.sync_copy(i_hbm.at[pl.ds(r * v + base, SC_WINDOW)], idx_v)
                pltpu.async_copy(rows_v, o_hbm.at[idx_v], sem).wait()

    return k(src, idx)


def _sc_gather_rows(table, idx):
    d = table.shape[1]
    n = idx.shape[0]
    per_w = n // (SC_NUM_CORES * SC_NUM_SUBCORES)
    win = 2 * SC_WINDOW if per_w % (2 * SC_WINDOW) == 0 else SC_WINDOW
    assert per_w % win == 0 and win <= LANES

    @functools.partial(pl.kernel, out_type=SDS((n, d), table.dtype), mesh=_sc_mesh(),
                       scratch_types=[pltpu.VMEM((win,), jnp.int32), pltpu.VMEM((win, d), table.dtype),
                                      pltpu.SemaphoreType.DMA])
    def k(x_hbm, i_hbm, o_hbm, idx_v, rows_v, sem):
        wid = lax.axis_index("s") * SC_NUM_CORES + lax.axis_index("c")

        @pl.loop(0, per_w // win)
        def _(j):
            base = wid * per_w + j * win
            pltpu.sync_copy(i_hbm.at[pl.ds(base, win)], idx_v)
            pltpu.async_copy(x_hbm.at[idx_v], rows_v, sem).wait()
            pltpu.sync_copy(rows_v, o_hbm.at[pl.ds(base, win)])

    return k(table, idx)


GU_BLOCK = 2 * LANES


def _deinterleave_perm():
    n = np.arange(GU_BLOCK)
    src = np.where(n < LANES, 2 * n, 2 * (n - LANES) + 1)
    return jnp.asarray(np.arange(GU_BLOCK)[:, None] == src[None, :], bf16)


def _expert_kernel(te_ref, nv_ref, x_ref, w1_ref, b1_ref, w2_ref, b2_ref, perm_ref, y_ref, w1_s, w2_s):
    i = pl.program_id(0)
    valid = i < nv_ref[0]
    new_expert = jnp.logical_or(i == 0, te_ref[i] != te_ref[jnp.maximum(i - 1, 0)])
    dff2 = w1_ref.shape[2]

    @pl.when(jnp.logical_and(valid, new_expert))
    def _():
        for cb in range(dff2 // GU_BLOCK):
            cs = slice(GU_BLOCK * cb, GU_BLOCK * (cb + 1))
            w1_s[:, cs] = _dot(w1_ref[0, :, cs].astype(bf16), perm_ref[...]).astype(bf16)
        w2_s[...] = w2_ref[0].astype(bf16)

    @pl.when(valid)
    def _():
        x = jnp.concatenate(_unpack_bf16_pairs(x_ref[...]), axis=1).astype(bf16)
        gu = _dot(x, w1_s[...]) + b1_ref[0]
        nblk = dff2 // GU_BLOCK
        g = jnp.concatenate([gu[:, GU_BLOCK * cb:GU_BLOCK * cb + LANES] for cb in range(nblk)], axis=1)
        u = jnp.concatenate([gu[:, GU_BLOCK * cb + LANES:GU_BLOCK * (cb + 1)] for cb in range(nblk)], axis=1)
        gate = jnp.minimum(g, SWIGLU_LIMIT)
        up = jnp.clip(u, -SWIGLU_LIMIT, SWIGLU_LIMIT)
        a = (up + 1.0) * gate * _sigmoid(SWIGLU_ALPHA * gate)
        y_ref[...] = _pack_bf16_pairs(_dot(a.astype(bf16), w2_s[...]) + b2_ref[0])


def _experts(xs, tile_e, n_valid, layer, w1, b1, w2, b2):
    p, dw = xs.shape
    d = 2 * dw
    dff2 = w1.shape[-1]
    dff = w2.shape[2]
    row = lambda i, te, nv: (jnp.minimum(i, nv[0] - 1), 0)
    wsel = lambda i, te, nv: (layer, te[i], 0, 0)
    sq = pl.Squeezed()
    return pl.pallas_call(
        _expert_kernel, out_shape=SDS((p, dw), jnp.int32),
        grid_spec=pltpu.PrefetchScalarGridSpec(
            num_scalar_prefetch=2, grid=(p // TME,),
            in_specs=[pl.BlockSpec((TME, dw), row),
                      pl.BlockSpec((sq, 1, d, dff2), wsel), pl.BlockSpec((sq, 1, 1, dff2), wsel),
                      pl.BlockSpec((sq, 1, dff, d), wsel), pl.BlockSpec((sq, 1, 1, d), wsel),
                      pl.BlockSpec((GU_BLOCK, GU_BLOCK), lambda i, te, nv: (0, 0))],
            out_specs=pl.BlockSpec((TME, dw), row),
            scratch_shapes=[pltpu.VMEM((d, dff2), bf16), pltpu.VMEM((dff, d), bf16)]),
        compiler_params=_cp(("arbitrary",)), name="experts",
    )(tile_e, n_valid, xs, w1, b1, w2, b2, _deinterleave_perm())


def _final_kernel(x_ref, yg_ref, w_ref, g2_ref, gf_ref, o_ref):
    x = x_ref[...] + g2_ref[0] * _weighted_expert_sum(yg_ref, w_ref[...])
    o_ref[...] = x * lax.rsqrt(jnp.mean(x * x, axis=-1, keepdims=True) + EPS) * gf_ref[...]


def _final(x, yg, wcol, mods, tpb, ctx_tiles, nb, g_final):
    d = x.shape[1]
    lat = tpb - ctx_tiles
    rmap = lambda i: ((i // lat) * tpb + ctx_tiles + i % lat)
    return pl.pallas_call(
        _final_kernel, out_shape=SDS((nb * lat * TMH, d), f32), grid=(nb * lat,),
        in_specs=[pl.BlockSpec((TMH, d), lambda i: (rmap(i), 0)),
                  pl.BlockSpec((TOP_K, TMH, d // 2), lambda i: (0, rmap(i), 0)),
                  pl.BlockSpec((TMH, TOP_K), lambda i: (rmap(i), 0)),
                  pl.BlockSpec((1, 1, 1024), lambda i: (i // lat, 0, 5)),
                  pl.BlockSpec((1, d), lambda i: (0, 0))],
        out_specs=pl.BlockSpec((TMH, d), lambda i: (i, 0)),
        compiler_params=_cp(("arbitrary",)), name="final",
    )(x, yg, wcol, mods, g_final.reshape(1, -1))


def _routing_tables(top_i, rank, counts, n_tiles):
    cnt = counts[:, 0].astype(jnp.int32)
    padded = ((cnt + TME - 1) // TME) * TME
    ends = jnp.cumsum(padded)
    starts = ends - padded
    eids = jnp.arange(N_EXPERTS, dtype=jnp.int32)
    start_of = jnp.sum(jnp.where(top_i[..., None] == eids, starts, 0), axis=-1)
    pos = (start_of + rank).reshape(-1)
    n_valid = ends[-1] // TME
    tile_start = jnp.arange(n_tiles, dtype=jnp.int32) * TME
    tile_e = jnp.sum(tile_start[:, None] >= ends[None, :], axis=1).astype(jnp.int32)
    tile_e = jnp.where(jnp.arange(n_tiles) < n_valid, tile_e, tile_e[jnp.maximum(n_valid - 1, 0)])
    tile_e = jnp.minimum(tile_e, N_EXPERTS - 1)
    return pos, tile_e, n_valid.reshape(1).astype(jnp.int32)


def kernel(x, c, ctx, c_ctx, w_ada, b_ada, g_norm1, w_in, a_conv_w, a_conv_b, a_gate_b, a_norm_g, b_q_norm_g, b_k_norm_g, c_w_up, c_b_up, c_norm_g, w_branch, w_out, g_norm2, w_router, b_router, w_e1, b_e1, w_e2, b_e2, g_final):
    nb, seq, d = x.shape
    ctx_len = ctx.shape[1]
    depth = w_ada.shape[0]
    tb = ctx_len + seq
    t = nb * tb
    tpb, ctx_tiles = tb // TMH, ctx_len // TMH
    assert d == 1024 and nb < 16 and seq % TMH == 0 and ctx_len % TMH == 0 and ctx_len % TQ == 0 and t % TM == 0
    n_assign = TOP_K * t
    n_sorted = n_assign + N_EXPERTS * TME
    n_tiles = n_sorted // TME

    xs = jnp.concatenate([ctx, x], axis=1).reshape(t, d)
    cc = jnp.zeros((16, d), f32).at[:nb].set(c).at[nb].set(c_ctx)
    mods_all = _adaln(cc, w_ada, b_ada)
    cos, sin = _rope_tables(seq, ctx_len)
    w_in_r = _arrange_w_in(w_in)
    w_branch_b, w_out_b = w_branch.astype(bf16), w_out.astype(bf16)
    col = np.arange(b_e1.shape[-1])
    within = col % GU_BLOCK
    src = (col // GU_BLOCK) * GU_BLOCK + np.where(within < LANES, 2 * within, 2 * (within - LANES) + 1)
    b1 = b_e1[..., src][..., None, :]
    b2 = b_e2[..., None, :]

    out = None
    prev = None
    for l in range(depth):
        mods = mods_all[l].reshape(16, 1, 6 * d)
        xs, p, (qn, kn, vn) = _inproj(xs, mods, g_norm1[l].reshape(1, -1), w_in_r[l], cos, sin,
                                      b_q_norm_g[l], b_k_norm_g[l], a_gate_b[l], c_w_up[l], c_b_up[l],
                                      tpb, ctx_tiles, nb, prev=prev)
        h_a = _mlstm(p["a_qk"], p["a_v"], p["small"], a_conv_w[l], a_conv_b[l], nb, ctx_len)
        h_c = _gla(p["c_qk"], p["c_v"], p["c_la"], nb, ctx_len)
        att = _attn(qn, kn, vn, nb, ctx_len)
        xs, h2, top_i, top_w, rank, counts = _merge(
            p, h_a, att, h_c, xs, mods, a_norm_g[l], c_norm_g[l], w_branch_b[l], w_out_b[l], g_norm2[l],
            w_router[l], b_router[l], tpb, ctx_tiles, nb)
        pos, tile_e, n_valid = _routing_tables(top_i, rank, counts, n_tiles)
        x_sorted = _sc_scatter_rows(h2, pos, n_sorted)
        y_sorted = _experts(x_sorted, tile_e, n_valid, l, w_e1, b1, w_e2, b2)
        yg = _sc_gather_rows(y_sorted, pos).reshape(TOP_K, t, d // 2)
        if l == depth - 1:
            out = _final(xs, yg, top_w.T, mods, tpb, ctx_tiles, nb, g_final)
        else:
            prev = (yg, top_w.T, mods)
    return out.reshape(nb, seq, d)
```
